```python
import math
import jax, jax.numpy as jnp
from jax import lax
import numpy as np

D_MODEL = 1024
BATCH = 8
SEQ = 2048
DEPTH = 1
DEC_BATCH = 128
DEC_SEQ = 4
PAST_LEN = 16384
PAGE_SIZE = 128

N_MEM = 256
XA_HEADS = 4
XA_HEAD_DIM = D_MODEL // XA_HEADS
SSD_WIDTH = D_MODEL // 2
SSD_HEAD_DIM = 64
SSD_HEADS = SSD_WIDTH // SSD_HEAD_DIM
SSD_GROUPS = 2
SSD_STATE = 128
SSD_CONV = 4
SSD_CHUNK = 128
SSD_XBC = SSD_WIDTH + 2 * SSD_GROUPS * SSD_STATE
SSD_PROJ = SSD_WIDTH + SSD_XBC + SSD_HEADS
RWKV_WIDTH = D_MODEL - SSD_WIDTH
RWKV_HEAD_DIM = 64
RWKV_HEADS = RWKV_WIDTH // RWKV_HEAD_DIM
DECAY_LORA = 64
ICLR_LORA = 64
GATE_LORA = 128
RWKV_PROJ = 3 * RWKV_WIDTH + DECAY_LORA + ICLR_LORA + GATE_LORA
IN_PROJ = SSD_PROJ + RWKV_PROJ
D_FF = 2816
FFN_CONV = 3
EPS = 1e-6
GN_EPS = 64e-5

kernel_name = "hymba_ssd_rwkv7_memxattn_convffn_step"


def _split_last(x, sizes):
    idx = np.cumsum(sizes)[:-1].tolist()
    return jnp.split(x, idx, axis=-1)


def rmsnorm(x, g):
    xf = x.astype(jnp.float32)
    y = xf * lax.rsqrt(jnp.mean(xf * xf, axis=-1, keepdims=True) + EPS)
    return (y * g.astype(jnp.float32)).astype(x.dtype)


def causal_dwconv(x, buf, w, b):
    width = w.shape[0]
    l = x.shape[1]
    xp = jnp.concatenate([buf.astype(x.dtype), x], axis=1)
    y = b + sum(xp[:, j:j + l] * w[j] for j in range(width))
    return y, xp[:, xp.shape[1] - (width - 1):]


def ssd_chunked(x, dt, a, bm, cm, h0):
    b, l = x.shape[:2]
    q = SSD_CHUNK if l % SSD_CHUNK == 0 else l
    c = l // q
    g, e, p, n = SSD_GROUPS, SSD_HEADS // SSD_GROUPS, SSD_HEAD_DIM, SSD_STATE
    xdt = (x * dt[..., None]).reshape(b, c, q, g, e, p)
    a_cs = jnp.cumsum((dt * a).reshape(b, c, q, g, e), axis=2)
    bc = bm.reshape(b, c, q, g, n)
    cc = cm.reshape(b, c, q, g, n)
    a_t = jnp.moveaxis(a_cs, 2, -1)
    causal = jnp.tril(jnp.ones((q, q), dtype=bool))
    seg = jnp.exp(jnp.where(causal, a_t[..., :, None] - a_t[..., None, :], -jnp.inf))
    cb = jnp.einsum("bcqgn,bcsgn->bcgqs", cc, bc)
    y_diag = jnp.einsum("bcgeqs,bcsgep->bcqgep", cb[:, :, :, None] * seg, xdt)
    decay_end = jnp.exp(a_cs[:, :, -1:] - a_cs)
    chunk_states = jnp.einsum("bcqgn,bcqgep->bcgepn", bc, xdt * decay_end[..., None])
    chunk_decay = jnp.exp(a_cs[:, :, -1])

    def step(h, inp):
        s_c, d_c = inp
        return h * d_c[..., None, None] + s_c, h

    h_last, h_prev = lax.scan(step, h0.reshape(b, g, e, p, n),
                              (jnp.moveaxis(chunk_states, 1, 0), jnp.moveaxis(chunk_decay, 1, 0)))
    h_prev = jnp.moveaxis(h_prev, 0, 1)
    y_off = jnp.einsum("bcqgn,bcgepn->bcqgep", cc, h_prev) * jnp.exp(a_cs)[..., None]
    y = (y_diag + y_off).reshape(b, l, SSD_HEADS, p)
    return y, h_last.reshape(b, SSD_HEADS, p, n)


def ssd_branch(u, conv_buf, h0, conv_w, conv_b, dt_bias, a_log, d_skip, norm_w):
    b, l, _ = u.shape
    f32 = jnp.float32
    z, xbc, dt_raw = _split_last(u, [SSD_WIDTH, SSD_XBC, SSD_HEADS])
    xbc, new_buf = causal_dwconv(xbc, conv_buf, conv_w, conv_b)
    xbc = jax.nn.silu(xbc)
    xs, bm, cm = _split_last(xbc, [SSD_WIDTH, SSD_GROUPS * SSD_STATE, SSD_GROUPS * SSD_STATE])
    dt = jax.nn.softplus((dt_raw + dt_bias).astype(f32))
    a = -jnp.exp(a_log.astype(f32))
    xh = xs.astype(f32).reshape(b, l, SSD_HEADS, SSD_HEAD_DIM)
    y, h = ssd_chunked(xh, dt, a,
                       bm.astype(f32).reshape(b, l, SSD_GROUPS, SSD_STATE),
                       cm.astype(f32).reshape(b, l, SSD_GROUPS, SSD_STATE),
                       h0.astype(f32))
    y = y + d_skip.astype(f32)[:, None] * xh
    y = y.reshape(b, l, SSD_WIDTH) * jax.nn.silu(z.astype(f32))
    yg = y.reshape(b, l, SSD_GROUPS, SSD_WIDTH // SSD_GROUPS)
    yg = yg * lax.rsqrt(jnp.mean(yg * yg, axis=-1, keepdims=True) + EPS)
    y = yg.reshape(b, l, SSD_WIDTH) * norm_w.astype(f32)
    return y.astype(u.dtype), new_buf, h.astype(h0.dtype)


def wkv7_scan(r, w, k, v, kk, a, s0):
    def step(s, inp):
        r_t, w_t, k_t, v_t, kk_t, a_t = inp
        sa = jnp.einsum("bhij,bhj->bhi", s, -kk_t)
        s = (s * w_t[:, :, None, :] + sa[..., None] * (kk_t * a_t)[:, :, None, :]
             + v_t[..., None] * k_t[:, :, None, :])
        return s, jnp.einsum("bhij,bhj->bhi", s, r_t)

    xs = tuple(jnp.moveaxis(t, 1, 0) for t in (r, w, k, v, kk, a))
    s_last, ys = lax.scan(step, s0, xs)
    return jnp.moveaxis(ys, 0, 1), s_last


def rwkv_branch(u, shift_buf, s0, mu, w0, w2, a0, a2, g2, k_k, k_a, r_k, ln_w, ln_b):
    b, l, _ = u.shape
    f32 = jnp.float32
    hs = (b, l, RWKV_HEADS, RWKV_HEAD_DIM)
    hd = (RWKV_HEADS, RWKV_HEAD_DIM)
    u_prev = jnp.concatenate([shift_buf[:, None, :].astype(u.dtype), u[:, :-1]], axis=1)
    um = u + (u_prev - u) * mu
    r, k, v, lw, la, lg = _split_last(um, [RWKV_WIDTH] * 3 + [DECAY_LORA, ICLR_LORA, GATE_LORA])
    w_log = -jax.nn.softplus(-(w0 + jnp.tanh(lw) @ w2).astype(f32)) - 0.5
    decay = jnp.exp(-jnp.exp(w_log)).reshape(hs)
    a = jax.nn.sigmoid((a0 + la @ a2).astype(f32)).reshape(hs)
    g = jax.nn.sigmoid(lg) @ g2
    r = r.astype(f32).reshape(hs)
    k = k.astype(f32).reshape(hs)
    v = v.astype(f32).reshape(hs)
    kk = k * k_k.astype(f32).reshape(hd)
    kk = kk / jnp.maximum(jnp.sqrt(jnp.sum(kk * kk, axis=-1, keepdims=True)), 1e-12)
    k = k * (1.0 + (a - 1.0) * k_a.astype(f32).reshape(hd))
    y, s_new = wkv7_scan(r, decay, k, v, kk, a, s0.astype(f32))
    mean = jnp.mean(y, axis=-1, keepdims=True)
    var = jnp.mean(jnp.square(y - mean), axis=-1, keepdims=True)
    y = (y - mean) * lax.rsqrt(var + GN_EPS) * ln_w.astype(f32).reshape(hd) + ln_b.astype(f32).reshape(hd)
    y = y + jnp.sum(r * k * r_k.astype(f32), axis=-1, keepdims=True) * v
    y = y.reshape(b, l, RWKV_WIDTH).astype(u.dtype) * g
    return y, u[:, -1], s_new.astype(s0.dtype)


def mem_kv(mem, g, w_k, w_v):
    b = mem.shape[0]
    m = rmsnorm(mem, g)
    k = (m @ w_k).reshape(b, mem.shape[1], XA_HEADS, XA_HEAD_DIM)
    v = (m @ w_v).reshape(b, mem.shape[1], XA_HEADS, XA_HEAD_DIM)
    return k, v


def cross_attn(h, mk, mv, w_q, w_o):
    b, l, _ = h.shape
    q = (h @ w_q).reshape(b, l, XA_HEADS, XA_HEAD_DIM)
    s = jnp.einsum("blhd,bmhd->bhlm", q, mk.astype(q.dtype)).astype(jnp.float32) * (XA_HEAD_DIM ** -0.5)
    pr = jax.nn.softmax(s, axis=-1).astype(h.dtype)
    o = jnp.einsum("bhlm,bmhd->blhd", pr, mv.astype(h.dtype)).reshape(b, l, D_MODEL)
    return o @ w_o


def conv_ffn(h, buf, w_up, conv_w, conv_b, w_down):
    up = h @ w_up
    up, new_buf = causal_dwconv(up, buf, conv_w, conv_b)
    gate, val = jnp.split(up, 2, axis=-1)
    return (jax.nn.silu(gate) * val) @ w_down, new_buf


def _layer(x, mem_k, mem_v, ssm_conv, ssm, shift, wkv, ffn_conv, p):
    h = rmsnorm(x, p["norm_mix_w"])
    u = h @ p["w_in"]
    y_ssd, ssm_conv_n, ssm_n = ssd_branch(u[..., :SSD_PROJ], ssm_conv, ssm, p["ssd_conv_w"], p["ssd_conv_b"],
                                          p["ssd_dt_bias"], p["ssd_a_log"], p["ssd_d"], p["ssd_norm_w"])
    y_rw, shift_n, wkv_n = rwkv_branch(u[..., SSD_PROJ:], shift, wkv, p["rwkv_mu"], p["rwkv_w0"], p["rwkv_w2"],
                                       p["rwkv_a0"], p["rwkv_a2"], p["rwkv_g2"], p["rwkv_k_k"], p["rwkv_k_a"],
                                       p["rwkv_r_k"], p["rwkv_ln_w"], p["rwkv_ln_b"])
    x = x + jnp.concatenate([y_ssd, y_rw], axis=-1) @ p["w_out"]
    x = x + cross_attn(rmsnorm(x, p["norm_xa_w"]), mem_k, mem_v, p["xa_w_q"], p["xa_w_o"])
    f, ffn_conv_n = conv_ffn(rmsnorm(x, p["norm_ffn_w"]), ffn_conv, p["ffn_w_up"], p["ffn_conv_w"],
                             p["ffn_conv_b"], p["ffn_w_down"])
    return x + f, ssm_conv_n, ssm_n, shift_n, wkv_n, ffn_conv_n


def setup_inputs(seed: int = 0) -> dict:
    key = jax.random.key(seed)
    ks = iter(jax.random.split(key, 64))
    f32 = jnp.float32

    def nrm(shape, scale):
        return jax.random.normal(next(ks), shape, f32) * scale

    def gain(shape):
        return 1.0 + nrm(shape, 0.02)

    L = DEPTH
    dt0 = jnp.exp(jax.random.uniform(next(ks), (L, SSD_HEADS), f32, math.log(1e-3), math.log(1e-1)))
    return {
        "x_prompt": nrm((BATCH, SEQ, D_MODEL), 1.0),
        "x_sample": nrm((DEC_BATCH, DEC_SEQ, D_MODEL), 1.0),
        "mem_prompt": nrm((BATCH, N_MEM, D_MODEL), 1.0),
        "state_ssm_conv": nrm((L, DEC_BATCH, SSD_CONV - 1, SSD_XBC), 1.0),
        "state_ssm": nrm((L, DEC_BATCH, SSD_HEADS, SSD_HEAD_DIM, SSD_STATE), 0.1),
        "state_shift": nrm((L, DEC_BATCH, RWKV_PROJ), 1.0),
        "state_wkv": nrm((L, DEC_BATCH, RWKV_HEADS, RWKV_HEAD_DIM, RWKV_HEAD_DIM), 0.1),
        "state_ffn_conv": nrm((L, DEC_BATCH, FFN_CONV - 1, 2 * D_FF), 1.0),
        "cache_mem_k": nrm((L, DEC_BATCH, N_MEM, XA_HEADS, XA_HEAD_DIM), 1.0),
        "cache_mem_v": nrm((L, DEC_BATCH, N_MEM, XA_HEADS, XA_HEAD_DIM), 1.0),
        "norm_mix_w": gain((L, D_MODEL)),
        "w_in": nrm((L, D_MODEL, IN_PROJ), D_MODEL ** -0.5),
        "ssd_conv_w": nrm((L, SSD_CONV, SSD_XBC), SSD_CONV ** -0.5),
        "ssd_conv_b": nrm((L, SSD_XBC), 0.01),
        "ssd_dt_bias": dt0 + jnp.log(-jnp.expm1(-dt0)),
        "ssd_a_log": jnp.log(jax.random.uniform(next(ks), (L, SSD_HEADS), f32, 1.0, 16.0)),
        "ssd_d": gain((L, SSD_HEADS)),
        "ssd_norm_w": gain((L, SSD_WIDTH)),
        "rwkv_mu": jax.random.uniform(next(ks), (L, RWKV_PROJ), f32, 0.0, 1.0),
        "rwkv_w0": jax.random.uniform(next(ks), (L, RWKV_WIDTH), f32, -6.0, 1.0),
        "rwkv_w2": nrm((L, DECAY_LORA, RWKV_WIDTH), 0.1 * DECAY_LORA ** -0.5),
        "rwkv_a0": nrm((L, RWKV_WIDTH), 0.1),
        "rwkv_a2": nrm((L, ICLR_LORA, RWKV_WIDTH), 0.5 * ICLR_LORA ** -0.5),
        "rwkv_g2": nrm((L, GATE_LORA, RWKV_WIDTH), GATE_LORA ** -0.5),
        "rwkv_k_k": 0.85 + nrm((L, RWKV_WIDTH), 0.02),
        "rwkv_k_a": gain((L, RWKV_WIDTH)),
        "rwkv_r_k": nrm((L, RWKV_HEADS, RWKV_HEAD_DIM), 0.1),
        "rwkv_ln_w": gain((L, RWKV_WIDTH)),
        "rwkv_ln_b": nrm((L, RWKV_WIDTH), 0.01),
        "w_out": nrm((L, D_MODEL, D_MODEL), D_MODEL ** -0.5),
        "norm_xa_w": gain((L, D_MODEL)),
        "mem_norm_w": gain((L, D_MODEL)),
        "xa_w_q": nrm((L, D_MODEL, D_MODEL), D_MODEL ** -0.5),
        "xa_w_k": nrm((L, D_MODEL, D_MODEL), D_MODEL ** -0.5),
        "xa_w_v": nrm((L, D_MODEL, D_MODEL), D_MODEL ** -0.5),
        "xa_w_o": nrm((L, D_MODEL, D_MODEL), D_MODEL ** -0.5),
        "norm_ffn_w": gain((L, D_MODEL)),
        "ffn_w_up": nrm((L, D_MODEL, 2 * D_FF), D_MODEL ** -0.5),
        "ffn_conv_w": nrm((L, FFN_CONV, 2 * D_FF), FFN_CONV ** -0.5),
        "ffn_conv_b": nrm((L, 2 * D_FF), 0.01),
        "ffn_w_down": nrm((L, D_FF, D_MODEL), D_FF ** -0.5),
        "final_norm_w": gain((D_MODEL,)),
    }


def reference(x_prompt, x_sample, mem_prompt, state_ssm_conv, state_ssm, state_shift, state_wkv,
              state_ffn_conv, cache_mem_k, cache_mem_v, norm_mix_w, w_in, ssd_conv_w, ssd_conv_b,
              ssd_dt_bias, ssd_a_log, ssd_d, ssd_norm_w, rwkv_mu, rwkv_w0, rwkv_w2, rwkv_a0, rwkv_a2,
              rwkv_g2, rwkv_k_k, rwkv_k_a, rwkv_r_k, rwkv_ln_w, rwkv_ln_b, w_out, norm_xa_w, mem_norm_w,
              xa_w_q, xa_w_k, xa_w_v, xa_w_o, norm_ffn_w, ffn_w_up, ffn_conv_w, ffn_conv_b, ffn_w_down,
              final_norm_w):
    bp = x_prompt.shape[0]
    dtp = x_prompt.dtype
    xp, xs = x_prompt, x_sample
    ssm_conv_p, ssm_conv_s, ssm_p, ssm_s = [], [], [], []
    shift_p, shift_s, wkv_p, wkv_s = [], [], [], []
    ffn_conv_p, ffn_conv_s, mem_k_p, mem_v_p = [], [], [], []
    for i in range(DEPTH):
        p = dict(norm_mix_w=norm_mix_w[i], w_in=w_in[i], ssd_conv_w=ssd_conv_w[i], ssd_conv_b=ssd_conv_b[i],
                 ssd_dt_bias=ssd_dt_bias[i], ssd_a_log=ssd_a_log[i], ssd_d=ssd_d[i], ssd_norm_w=ssd_norm_w[i],
                 rwkv_mu=rwkv_mu[i], rwkv_w0=rwkv_w0[i], rwkv_w2=rwkv_w2[i], rwkv_a0=rwkv_a0[i],
                 rwkv_a2=rwkv_a2[i], rwkv_g2=rwkv_g2[i], rwkv_k_k=rwkv_k_k[i], rwkv_k_a=rwkv_k_a[i],
                 rwkv_r_k=rwkv_r_k[i], rwkv_ln_w=rwkv_ln_w[i], rwkv_ln_b=rwkv_ln_b[i], w_out=w_out[i],
                 norm_xa_w=norm_xa_w[i], xa_w_q=xa_w_q[i], xa_w_o=xa_w_o[i], norm_ffn_w=norm_ffn_w[i],
                 ffn_w_up=ffn_w_up[i], ffn_conv_w=ffn_conv_w[i], ffn_conv_b=ffn_conv_b[i],
                 ffn_w_down=ffn_w_down[i])
        mk, mv = mem_kv(mem_prompt, mem_norm_w[i], xa_w_k[i], xa_w_v[i])
        xp, c0, s0, h0, w0_, f0 = _layer(
            xp, mk, mv,
            jnp.zeros((bp, SSD_CONV - 1, SSD_XBC), dtp),
            jnp.zeros((bp, SSD_HEADS, SSD_HEAD_DIM, SSD_STATE), dtp),
            jnp.zeros((bp, RWKV_PROJ), dtp),
            jnp.zeros((bp, RWKV_HEADS, RWKV_HEAD_DIM, RWKV_HEAD_DIM), dtp),
            jnp.zeros((bp, FFN_CONV - 1, 2 * D_FF), dtp), p)
        xs, c1, s1, h1, w1_, f1 = _layer(
            xs, cache_mem_k[i], cache_mem_v[i], state_ssm_conv[i], state_ssm[i], state_shift[i],
            state_wkv[i], state_ffn_conv[i], p)
        ssm_conv_p.append(c0); ssm_conv_s.append(c1)
        ssm_p.append(s0); ssm_s.append(s1)
        shift_p.append(h0); shift_s.append(h1)
        wkv_p.append(w0_); wkv_s.append(w1_)
        ffn_conv_p.append(f0); ffn_conv_s.append(f1)
        mem_k_p.append(mk); mem_v_p.append(mv)
    y_prompt = rmsnorm(xp, final_norm_w)
    y_sample = rmsnorm(xs, final_norm_w)
    return (y_prompt, y_sample,
            jnp.stack(ssm_conv_p), jnp.stack(ssm_conv_s),
            jnp.stack(ssm_p), jnp.stack(ssm_s),
            jnp.stack(shift_p), jnp.stack(shift_s),
            jnp.stack(wkv_p), jnp.stack(wkv_s),
            jnp.stack(ffn_conv_p), jnp.stack(ffn_conv_s),
            jnp.stack(mem_k_p), jnp.stack(mem_v_p))
```

```python
import functools
import math

import jax
import jax.numpy as jnp
from jax import lax
from jax.experimental import pallas as pl
from jax.experimental.pallas import tpu as pltpu

F32 = jnp.float32
BF16 = jnp.bfloat16

D_MODEL = 1024
N_MEM = 256
XA_HEADS = 4
XA_HEAD_DIM = D_MODEL // XA_HEADS
SSD_WIDTH = 512
SSD_HEADS = 8
SSD_STATE = 128
SSD_XBC = 1024
SSD_PROJ = SSD_WIDTH + SSD_XBC + SSD_HEADS
RWKV_WIDTH = 512
RWKV_PROJ = 1792
D_FF = 2816
EPS = 1e-6
GN_EPS = 64e-5

LANES = 128
SEG = 1792
DT_OFF = SSD_WIDTH + SSD_XBC
VMEM_LIMIT_BYTES = 56 * 1024 * 1024


def _cparams(*sem):
    return pltpu.CompilerParams(dimension_semantics=sem, vmem_limit_bytes=VMEM_LIMIT_BYTES)


_NN = (((1,), (0,)), ((), ()))
_NT = (((1,), (1,)), ((), ()))
_TN = (((0,), (0,)), ((), ()))


def _bdot(a, b, dims=_NN):
    return lax.dot_general(a.astype(BF16), b.astype(BF16), dims, preferred_element_type=F32)


def _split2(x):
    hi = x.astype(BF16)
    lo = (x - hi.astype(F32)).astype(BF16)
    return hi, lo


def _dot3(a, b, dims=_NN):
    ah, al = _split2(a)
    bh, bl = _split2(b)
    d = functools.partial(lax.dot_general, dimension_numbers=dims, preferred_element_type=F32)
    return d(ah, bh) + (d(ah, bl) + d(al, bh))


def _maskdot(m_bf16, x):
    hi = x.astype(BF16)
    r1 = x - hi.astype(F32)
    mid = r1.astype(BF16)
    lo = (r1 - mid.astype(F32)).astype(BF16)
    d = functools.partial(jnp.dot, preferred_element_type=F32)
    return d(m_bf16, hi) + (d(m_bf16, mid) + d(m_bf16, lo))


def _xmask(x, m_bf16):
    hi, lo = _split2(x)
    d = functools.partial(jnp.dot, preferred_element_type=F32)
    return d(hi, m_bf16) + d(lo, m_bf16)


def _sigmoid(x):
    return 1.0 / (1.0 + jnp.exp(-x))


def _softplus(x):
    return jnp.maximum(x, 0.0) + jnp.log1p(jnp.exp(-jnp.abs(x)))


def _rms(x, g):
    return x * lax.rsqrt(jnp.mean(x * x, axis=-1, keepdims=True) + EPS) * g


def _iota2(shape, dim):
    return lax.broadcasted_iota(jnp.int32, shape, dim)


def _shifted(x, k, tpos, hist):
    return jnp.where(tpos >= k, pltpu.roll(x, k, 0), hist)


def _norm_proj_kernel(x_ref, g_ref, w_ref, o_ref, hn_sc):
    @pl.when(pl.program_id(1) == 0)
    def _():
        hn_sc[...] = _rms(x_ref[...], g_ref[...]).astype(BF16)

    o_ref[...] = jnp.dot(hn_sc[...], w_ref[...], preferred_element_type=F32)


def _norm_proj(x, g, w_bf16, tm, tn):
    n, d = x.shape
    f = w_bf16.shape[1]
    return pl.pallas_call(
        _norm_proj_kernel,
        grid=(n // tm, f // tn),
        in_specs=[pl.BlockSpec((tm, d), lambda i, j: (i, 0)),
                  pl.BlockSpec((1, d), lambda i, j: (0, 0)),
                  pl.BlockSpec((d, tn), lambda i, j: (0, j))],
        out_specs=pl.BlockSpec((tm, tn), lambda i, j: (i, j)),
        out_shape=jax.ShapeDtypeStruct((n, f), F32),
        scratch_shapes=[pltpu.VMEM((tm, d), BF16)],
        compiler_params=_cparams("parallel", "arbitrary"),
        name="norm_proj",
    )(x, g.reshape(1, d), w_bf16)


def _ssd_kernel(*refs, Q, Lb, prompt):
    ns = Q // Lb
    lb = int(math.log2(Lb))
    if prompt:
        (u_ref, cw_ref, cb_ref, dtb_ref, an_ref, dsk_ref, nw_ref,
         y_ref, hout_ref, carry_sc, h_sc) = refs
        c = pl.program_id(1)

        @pl.when(c == 0)
        def _():
            carry_sc[...] = jnp.zeros_like(carry_sc)
            h_sc[...] = jnp.zeros_like(h_sc)
    else:
        (u_ref, hist_ref, hin_ref, cw_ref, cb_ref, dtb_ref, an_ref, dsk_ref, nw_ref,
         y_ref, hout_ref) = refs

    u = u_ref[...]
    z = u[:, 0:SSD_WIDTH]
    x = u[:, SSD_WIDTH:SSD_WIDTH + SSD_XBC]
    dtr = u[:, DT_OFF:DT_OFF + LANES]

    rows = _iota2((Q, 1), 0)
    tpos = rows & (Lb - 1)
    if prompt:
        c8 = carry_sc[...]
        h1 = c8[7:8]
        h2 = jnp.where(rows == 0, c8[6:7], c8[7:8])
        h3 = jnp.where(rows == 0, c8[5:6], jnp.where(rows == 1, c8[6:7], c8[7:8]))
        carry_sc[...] = x[Q - 8:Q]
    else:
        h1, h2, h3 = hist_ref[0], hist_ref[1], hist_ref[2]
    cw = cw_ref[...]
    xc = (cw[0:1] * _shifted(x, 3, tpos, h3) + cw[1:2] * _shifted(x, 2, tpos, h2)
          + cw[2:3] * _shifted(x, 1, tpos, h1) + cw[3:4] * x) + cb_ref[...]
    xc = xc * _sigmoid(xc)
    xs = xc[:, 0:SSD_WIDTH]
    bm = xc[:, SSD_WIDTH:SSD_WIDTH + 2 * SSD_STATE]
    cm = xc[:, SSD_WIDTH + 2 * SSD_STATE:]

    dt = _softplus(dtr + dtb_ref[...])
    da = dt * an_ref[...]

    ri = _iota2((Q, Q), 0)
    ci = _iota2((Q, Q), 1)
    same = (ri >> lb) == (ci >> lb)
    tril = same & (ci <= ri)
    sel = ci == (((ri >> lb) << lb) + (Lb - 1))
    acs = _maskdot(tril.astype(BF16), da)
    acs_t = acs.T
    acs_last = _maskdot(sel.astype(BF16), acs)
    dec_end = jnp.exp(acs_last - acs)
    eacs = jnp.exp(acs)
    seqid = rows >> lb

    lane = _iota2((Q, LANES), 1)
    lo_half = lane < 64
    prow = _iota2((LANES, LANES), 0)
    dsk = dsk_ref[...]

    ys = []
    for q in range(4):
        g = q // 2
        h0, h1i = 2 * q, 2 * q + 1
        if q % 2 == 0:
            cg = cm[:, g * SSD_STATE:(g + 1) * SSD_STATE]
            bg = bm[:, g * SSD_STATE:(g + 1) * SSD_STATE]
            cb_g = _bdot(cg, bg, _NT)
            if ns == 1:
                cexp, bexp = cg.astype(BF16), bg.astype(BF16)
            else:
                cexp = jnp.concatenate([jnp.where(seqid == b, cg, 0.0).astype(BF16) for b in range(ns)], axis=1)
                bexp = jnp.concatenate([jnp.where(seqid == b, bg, 0.0).astype(BF16) for b in range(ns)], axis=1)
        m0 = jnp.where(tril, cb_g * jnp.exp(acs[:, h0:h0 + 1] - acs_t[h0:h0 + 1, :]), 0.0)
        m1 = jnp.where(tril, cb_g * jnp.exp(acs[:, h1i:h1i + 1] - acs_t[h1i:h1i + 1, :]), 0.0)
        xp = xs[:, q * LANES:(q + 1) * LANES]
        xdt = xp * jnp.where(lo_half, dt[:, h0:h0 + 1], dt[:, h1i:h1i + 1])
        xdt0 = jnp.where(lo_half, xdt, 0.0)
        xdt1 = jnp.where(lo_half, 0.0, xdt)
        ydiag = _bdot(jnp.concatenate([m0, m1], axis=1), jnp.concatenate([xdt0, xdt1], axis=0))
        if prompt:
            hst = h_sc[q]
        else:
            hst = jnp.concatenate([hin_ref[b, q] for b in range(ns)], axis=1)
        ecs = jnp.where(lo_half, eacs[:, h0:h0 + 1], eacs[:, h1i:h1i + 1])
        yoff = _bdot(cexp, hst, _NT) * ecs
        xd = xdt * jnp.where(lo_half, dec_end[:, h0:h0 + 1], dec_end[:, h1i:h1i + 1])
        incr = _bdot(xd, bexp, _TN)
        scales = []
        for b in range(ns):
            r = b * Lb + Lb - 1
            e = eacs[r:r + 1, :]
            scales.append(jnp.where(prow < 64, e[:, h0:h0 + 1], e[:, h1i:h1i + 1]))
        scale = scales[0] if ns == 1 else jnp.concatenate(scales, axis=1)
        hnew = hst * scale + incr
        if prompt:
            h_sc[q] = hnew
            hout_ref[0, q] = hnew
        else:
            for b in range(ns):
                hout_ref[b, q] = hnew[:, b * LANES:(b + 1) * LANES]
        ys.append(ydiag + yoff + dsk[:, q * LANES:(q + 1) * LANES] * xp)

    y = jnp.concatenate(ys, axis=1)
    y = y * (z * _sigmoid(z))
    half = SSD_WIDTH // 2
    outs = []
    for g in range(2):
        yg = y[:, g * half:(g + 1) * half]
        outs.append(yg * lax.rsqrt(jnp.mean(yg * yg, axis=-1, keepdims=True) + EPS))
    y_ref[...] = jnp.concatenate(outs, axis=1) * nw_ref[...]


def _ssd_params(p):
    an = jnp.zeros((1, LANES), F32).at[0, :SSD_HEADS].set(-jnp.exp(p["ssd_a_log"]))
    dtb = jnp.zeros((1, LANES), F32).at[0, :SSD_HEADS].set(p["ssd_dt_bias"])
    dsk = jnp.repeat(p["ssd_d"], SSD_WIDTH // SSD_HEADS).reshape(1, SSD_WIDTH)
    return [p["ssd_conv_w"], p["ssd_conv_b"].reshape(1, SSD_XBC), dtb, an, dsk,
            p["ssd_norm_w"].reshape(1, SSD_WIDTH)]


def _const_spec(a):
    nd = a.ndim
    return pl.BlockSpec(a.shape, lambda *_: (0,) * nd)


def _ssd_prompt(u, p, batch, seq, Q=128):
    n = u.shape[0]
    nc = seq // Q
    params = _ssd_params(p)
    y, hout = pl.pallas_call(
        functools.partial(_ssd_kernel, Q=Q, Lb=Q, prompt=True),
        grid=(batch, nc),
        in_specs=[pl.BlockSpec((Q, SEG), lambda b, c: (b * nc + c, 0))] + [_const_spec(a) for a in params],
        out_specs=[pl.BlockSpec((Q, SSD_WIDTH), lambda b, c: (b * nc + c, 0)),
                   pl.BlockSpec((1, 4, LANES, LANES), lambda b, c: (b, 0, 0, 0))],
        out_shape=[jax.ShapeDtypeStruct((n, SSD_WIDTH), F32),
                   jax.ShapeDtypeStruct((batch, 4, LANES, LANES), F32)],
        scratch_shapes=[pltpu.VMEM((8, SSD_XBC), F32), pltpu.VMEM((4, LANES, LANES), F32)],
        compiler_params=_cparams("parallel", "arbitrary"),
        name="ssd_prompt",
    )(u, *params)
    return y, hout


def _ssd_sample(u, hist, hin, p, Lb, Q=64):
    n = u.shape[0]
    ns = Q // Lb
    params = _ssd_params(p)
    y, hout = pl.pallas_call(
        functools.partial(_ssd_kernel, Q=Q, Lb=Lb, prompt=False),
        grid=(n // Q,),
        in_specs=[pl.BlockSpec((Q, SEG), lambda i: (i, 0)),
                  pl.BlockSpec((3, Q, SSD_XBC), lambda i: (0, i, 0)),
                  pl.BlockSpec((ns, 4, LANES, LANES), lambda i: (i, 0, 0, 0))]
                 + [_const_spec(a) for a in params],
        out_specs=[pl.BlockSpec((Q, SSD_WIDTH), lambda i: (i, 0)),
                   pl.BlockSpec((ns, 4, LANES, LANES), lambda i: (i, 0, 0, 0))],
        out_shape=[jax.ShapeDtypeStruct((n, SSD_WIDTH), F32),
                   jax.ShapeDtypeStruct(hin.shape, F32)],
        compiler_params=_cparams("parallel"),
        name="ssd_sample",
    )(u, hist, hin, *params)
    return y, hout


def _tri_inverse(a, lb):
    n = a.shape[0]
    ri = _iota2((n, n), 0)
    ci = _iota2((n, n), 1)
    t = jnp.where(ri == ci, 1.0, 0.0) + jnp.where(((ri >> 1) == (ci >> 1)) & ((ri & 1) == 1) & ((ci & 1) == 0), a, 0.0)
    for lvl in range(1, lb):
        m = 1 << lvl
        off = ((ri >> (lvl + 1)) == (ci >> (lvl + 1))) & ((ri & (2 * m - 1)) >= m) & ((ci & (2 * m - 1)) < m)
        t = t + _dot3(t, _dot3(jnp.where(off, a, 0.0), t))
    return t


def _wkv_kernel(*refs, C, Lb, prompt):
    nb = C // Lb
    lb = int(math.log2(Lb))
    R = 2 * C
    if prompt:
        (u_ref, mu_ref, w0_ref, w2_ref, a0_ref, a2_ref, g2_ref, kk_ref, ka_ref, rk_ref, lnw_ref, lnb_ref,
         y_ref, sout_ref, carry_sc, s_sc) = refs
        c = pl.program_id(1)

        @pl.when(c == 0)
        def _():
            carry_sc[...] = jnp.zeros_like(carry_sc)
            s_sc[...] = jnp.zeros_like(s_sc)
    else:
        (u_ref, hist_ref, sin_ref, mu_ref, w0_ref, w2_ref, a0_ref, a2_ref, g2_ref, kk_ref, ka_ref, rk_ref,
         lnw_ref, lnb_ref, y_ref, sout_ref) = refs

    u = u_ref[...]
    rows = _iota2((C, 1), 0)
    tpos = rows & (Lb - 1)
    if prompt:
        hist = carry_sc[7:8]
        carry_sc[...] = u[C - 8:C]
    else:
        hist = hist_ref[...]
    um = u + (_shifted(u, 1, tpos, hist) - u) * mu_ref[...]

    W = RWKV_WIDTH
    r = um[:, 0:W]
    k = um[:, W:2 * W]
    v = um[:, 2 * W:3 * W]
    t12 = um[:, 3 * W:3 * W + LANES]
    lg = um[:, 3 * W + LANES:3 * W + 2 * LANES]

    wl = w0_ref[...] + _bdot(jnp.tanh(t12), w2_ref[...])
    logw = -jnp.exp(-_softplus(-wl) - 0.5)
    a = _sigmoid(a0_ref[...] + _bdot(t12, a2_ref[...]))
    g = _bdot(_sigmoid(lg), g2_ref[...])

    hi = _iota2((W, W), 0)
    hj = _iota2((W, W), 1)
    headsum = ((hi >> 6) == (hj >> 6)).astype(BF16)

    kk = k * kk_ref[...]
    kk = kk / jnp.maximum(jnp.sqrt(_xmask(kk * kk, headsum)), 1e-12)
    kmod = k * (1.0 + (a - 1.0) * ka_ref[...])
    beta = kk * a

    ri = _iota2((C, C), 0)
    ci = _iota2((C, C), 1)
    same_c = (ri >> lb) == (ci >> lb)
    lc = _maskdot((same_c & (ci <= ri)).astype(BF16), logw)
    lc_last = _maskdot((ci == (((ri >> lb) << lb) + (Lb - 1))).astype(BF16), lc)
    e_neg = jnp.exp(-lc)
    e_end = jnp.exp(lc_last - lc)
    at = -kk * jnp.exp(lc - logw)
    rt = r * jnp.exp(lc)
    bt = beta * e_neg
    kt = kmod * e_neg
    bh = beta * e_end
    kh = kmod * e_end
    p_last = jnp.exp(lc_last)

    si = _iota2((R, R), 0)
    sj = _iota2((R, R), 1)
    same_s = (si >> lb) == (sj >> lb)
    strict = same_s & (sj < si)
    incl = same_s & (sj <= si)
    lane = _iota2((C, LANES), 1)
    lo_half = lane < 64
    seq_s = (_iota2((R, 1), 0) & (C - 1)) >> lb

    def stack(xp):
        return jnp.concatenate([jnp.where(lo_half, xp, 0.0), jnp.where(lo_half, 0.0, xp)], axis=0)

    def expand(xs_):
        if nb == 1:
            return xs_
        return jnp.concatenate([jnp.where(seq_s == b, xs_, 0.0) for b in range(nb)], axis=1)

    ys = []
    for p in range(4):
        sl = slice(p * LANES, (p + 1) * LANES)
        a_s, r_s, b_s, k_s = stack(at[:, sl]), stack(rt[:, sl]), stack(bt[:, sl]), stack(kt[:, sl])
        v_s, bh_s, kh_s = stack(v[:, sl]), stack(bh[:, sl]), stack(kh[:, sl])
        gm = _dot3(jnp.concatenate([a_s, r_s], axis=0), jnp.concatenate([b_s, k_s], axis=0), _NT)
        a_ab = jnp.where(strict, gm[0:R, 0:R], 0.0)
        a_ak = jnp.where(strict, gm[0:R, R:2 * R], 0.0)
        a_rb = jnp.where(incl, gm[R:2 * R, 0:R], 0.0)
        a_rk = jnp.where(incl, gm[R:2 * R, R:2 * R], 0.0)
        if prompt:
            sst = s_sc[p]
        else:
            sst = jnp.concatenate([sin_ref[b, p] for b in range(nb)], axis=1)
        ar0 = _dot3(jnp.concatenate([expand(a_s), expand(r_s)], axis=0), sst, _NT)
        rhs = ar0[0:R] + _dot3(a_ak, v_s)
        us = _dot3(_tri_inverse(a_ab, lb), rhs)
        uv = jnp.concatenate([us, v_s], axis=0)
        yst = ar0[R:2 * R] + _dot3(jnp.concatenate([a_rb, a_rk], axis=1), uv)
        ys.append(yst[0:C] + yst[C:R])
        pl_lanes = p_last[:, sl]
        if nb == 1:
            plast = pl_lanes[0:1]
        else:
            plast = jnp.concatenate([pl_lanes[b * Lb:b * Lb + 1] for b in range(nb)], axis=1)
        snew = sst * plast + _dot3(uv, jnp.concatenate([expand(bh_s), expand(kh_s)], axis=0), _TN)
        if prompt:
            s_sc[p] = snew
            sout_ref[0, p] = snew
        else:
            for b in range(nb):
                sout_ref[b, p] = snew[:, b * LANES:(b + 1) * LANES]

    y = jnp.concatenate(ys, axis=1)
    inv_d = 1.0 / 64.0
    mean = _xmask(y, headsum) * inv_d
    yc = y - mean
    var = _xmask(yc * yc, headsum) * inv_d
    yn = yc * lax.rsqrt(var + GN_EPS) * lnw_ref[...] + lnb_ref[...]
    yn = yn + _xmask(r * kmod * rk_ref[...], headsum) * v
    y_ref[...] = yn * g


def _wkv_params(p):
    z64 = jnp.zeros((64, RWKV_WIDTH), F32)
    w2 = jnp.concatenate([p["rwkv_w2"], z64], axis=0).astype(BF16)
    a2 = jnp.concatenate([z64, p["rwkv_a2"]], axis=0).astype(BF16)
    row = lambda a: a.reshape(1, -1)
    return [row(p["rwkv_mu"]), row(p["rwkv_w0"]), w2, row(p["rwkv_a0"]), a2, p["rwkv_g2"].astype(BF16),
            row(p["rwkv_k_k"]), row(p["rwkv_k_a"]), row(p["rwkv_r_k"]), row(p["rwkv_ln_w"]), row(p["rwkv_ln_b"])]


def _wkv_prompt(u, p, batch, seq, C=64):
    n = u.shape[0]
    nc = seq // C
    params = _wkv_params(p)
    return pl.pallas_call(
        functools.partial(_wkv_kernel, C=C, Lb=C, prompt=True),
        grid=(batch, nc),
        in_specs=[pl.BlockSpec((C, SEG), lambda b, c: (b * nc + c, 1))] + [_const_spec(a) for a in params],
        out_specs=[pl.BlockSpec((C, RWKV_WIDTH), lambda b, c: (b * nc + c, 0)),
                   pl.BlockSpec((1, 4, LANES, LANES), lambda b, c: (b, 0, 0, 0))],
        out_shape=[jax.ShapeDtypeStruct((n, RWKV_WIDTH), F32),
                   jax.ShapeDtypeStruct((batch, 4, LANES, LANES), F32)],
        scratch_shapes=[pltpu.VMEM((8, SEG), F32), pltpu.VMEM((4, LANES, LANES), F32)],
        compiler_params=_cparams("parallel", "arbitrary"),
        name="wkv_prompt",
    )(u, *params)


def _wkv_sample(u, hist, sin, p, Lb, C=64):
    n = u.shape[0]
    nb = C // Lb
    params = _wkv_params(p)
    return pl.pallas_call(
        functools.partial(_wkv_kernel, C=C, Lb=Lb, prompt=False),
        grid=(n // C,),
        in_specs=[pl.BlockSpec((C, SEG), lambda i: (i, 1)),
                  pl.BlockSpec((C, SEG), lambda i: (i, 0)),
                  pl.BlockSpec((nb, 4, LANES, LANES), lambda i: (i, 0, 0, 0))]
                 + [_const_spec(a) for a in params],
        out_specs=[pl.BlockSpec((C, RWKV_WIDTH), lambda i: (i, 0)),
                   pl.BlockSpec((nb, 4, LANES, LANES), lambda i: (i, 0, 0, 0))],
        out_shape=[jax.ShapeDtypeStruct((n, RWKV_WIDTH), F32),
                   jax.ShapeDtypeStruct(sin.shape, F32)],
        compiler_params=_cparams("parallel"),
        name="wkv_sample",
    )(u, hist, sin, *params)


def _outproj_kernel(x_ref, ys_ref, yr_ref, wo_ref, g_ref, wq_ref, x1_ref, q_ref):
    wo = wo_ref[...]
    x1 = x_ref[...] + (_bdot(ys_ref[...], wo[0:SSD_WIDTH]) + _bdot(yr_ref[...], wo[SSD_WIDTH:]))
    x1_ref[...] = x1
    q_ref[...] = _bdot(_rms(x1, g_ref[...]), wq_ref[...])


def _outproj(x, y_ssd, y_rw, w_out, g, w_q, tm):
    n, d = x.shape
    return pl.pallas_call(
        _outproj_kernel,
        grid=(n // tm,),
        in_specs=[pl.BlockSpec((tm, d), lambda i: (i, 0)),
                  pl.BlockSpec((tm, SSD_WIDTH), lambda i: (i, 0)),
                  pl.BlockSpec((tm, RWKV_WIDTH), lambda i: (i, 0)),
                  _const_spec(w_out), pl.BlockSpec((1, d), lambda i: (0, 0)), _const_spec(w_q)],
        out_specs=[pl.BlockSpec((tm, d), lambda i: (i, 0)), pl.BlockSpec((tm, d), lambda i: (i, 0))],
        out_shape=[jax.ShapeDtypeStruct((n, d), F32), jax.ShapeDtypeStruct((n, d), F32)],
        compiler_params=_cparams("parallel"),
        name="outproj_q",
    )(x, y_ssd, y_rw, w_out, g.reshape(1, d), w_q)


def _xattn_kernel(q_ref, x1_ref, k_ref, v_ref, wo_ref, o_ref, *, rows_per_mem):
    tq = q_ref.shape[0]
    nmem = tq // rows_per_mem
    q = q_ref[...]
    row_mem = _iota2((tq, 1), 0) >> int(math.log2(rows_per_mem)) if nmem > 1 else None
    scale = XA_HEAD_DIM ** -0.5
    outs = []
    for h in range(XA_HEADS):
        sl = slice(h * XA_HEAD_DIM, (h + 1) * XA_HEAD_DIM)
        qh = q[:, sl].astype(BF16)
        oh = None
        for m in range(nmem):
            s = lax.dot_general(qh, k_ref[m][:, sl].astype(BF16), _NT, preferred_element_type=F32) * scale
            s = s - jnp.max(s, axis=-1, keepdims=True)
            e = jnp.exp(s)
            pr = e / jnp.sum(e, axis=-1, keepdims=True)
            om = _bdot(pr, v_ref[m][:, sl])
            oh = om if oh is None else jnp.where(row_mem == m, om, oh)
        outs.append(oh)
    o = jnp.concatenate(outs, axis=1)
    o_ref[...] = x1_ref[...] + _bdot(o, wo_ref[...])


def _xattn(q, x1, mk, mv, w_o, tq, rows_per_mem):
    n, d = q.shape
    nmem = tq // rows_per_mem
    if rows_per_mem >= tq:
        per = rows_per_mem // tq
        kv_spec = pl.BlockSpec((1, N_MEM, d), lambda i: (i // per, 0, 0))
        rpm = tq
    else:
        kv_spec = pl.BlockSpec((nmem, N_MEM, d), lambda i: (i, 0, 0))
        rpm = rows_per_mem
    return pl.pallas_call(
        functools.partial(_xattn_kernel, rows_per_mem=rpm),
        grid=(n // tq,),
        in_specs=[pl.BlockSpec((tq, d), lambda i: (i, 0)), pl.BlockSpec((tq, d), lambda i: (i, 0)),
                  kv_spec, kv_spec, _const_spec(w_o)],
        out_specs=pl.BlockSpec((tq, d), lambda i: (i, 0)),
        out_shape=jax.ShapeDtypeStruct((n, d), F32),
        compiler_params=_cparams("parallel"),
        name="xattn",
    )(q, x1, mk, mv, w_o)


def _ffn_kernel(*refs, tm, Lb, prompt, blocks_per_seq):
    if prompt:
        (x_ref, g_ref, wg_ref, wv_ref, cwg_ref, cwv_ref, cbg_ref, cbv_ref, wd_ref, fg_ref,
         y_ref, sg_ref, sv_ref, hn_sc, acc_sc, carry_sc) = refs
    else:
        (x_ref, hg_ref, hv_ref, g_ref, wg_ref, wv_ref, cwg_ref, cwv_ref, cbg_ref, cbv_ref, wd_ref, fg_ref,
         y_ref, sg_ref, sv_ref, hn_sc, acc_sc) = refs
    i = pl.program_id(0)
    j = pl.program_id(1)
    nj = pl.num_programs(1)

    @pl.when(j == 0)
    def _():
        hn_sc[...] = _rms(x_ref[...], g_ref[...]).astype(BF16)
        acc_sc[...] = jnp.zeros_like(acc_sc)

    rows = _iota2((tm, 1), 0)
    tpos = rows & (Lb - 1)
    hn = hn_sc[...]

    def branch(w_ref, cw_ref, cb_ref, part, s_ref):
        up = jnp.dot(hn, w_ref[...], preferred_element_type=F32)
        if prompt:
            @pl.when(i % blocks_per_seq == 0)
            def _():
                carry_sc[j, part] = jnp.zeros(carry_sc.shape[2:], F32)
            c8 = carry_sc[j, part]
            h1 = c8[7:8]
            h2 = jnp.where(rows == 0, c8[6:7], c8[7:8])
            carry_sc[j, part] = up[tm - 8:tm]
            s_ref[0] = up[tm - 2:tm]
        else:
            hist = hg_ref if part == 0 else hv_ref
            h1, h2 = hist[0], hist[1]
            for b in range(tm // Lb):
                s_ref[b] = up[b * Lb + Lb - 2:b * Lb + Lb]
        cw = cw_ref[...]
        return (cw[0:1] * _shifted(up, 2, tpos, h2) + cw[1:2] * _shifted(up, 1, tpos, h1)
                + cw[2:3] * up) + cb_ref[...]

    gate = branch(wg_ref, cwg_ref, cbg_ref, 0, sg_ref)
    val = branch(wv_ref, cwv_ref, cbv_ref, 1, sv_ref)
    act = gate * _sigmoid(gate) * val
    acc_sc[...] += _bdot(act, wd_ref[...])

    @pl.when(j == nj - 1)
    def _():
        y_ref[...] = _rms(x_ref[...] + acc_sc[...], fg_ref[...])


def _ffn(x, hist, p, final_g, tm, tf, Lb, prompt, seq_len):
    n, d = x.shape
    nj = D_FF // tf
    w_up = p["ffn_w_up"].astype(BF16)
    w_down = p["ffn_w_down"].astype(BF16)
    cw, cb = p["ffn_conv_w"], p["ffn_conv_b"].reshape(1, 2 * D_FF)
    nseq = n // seq_len
    if prompt:
        bps = seq_len // tm
        nseq = n // tm
        s_spec = pl.BlockSpec((1, 2, tf), lambda i, j: (i, 0, j))
        extra_in, extra_specs = [], []
        scratch = [pltpu.VMEM((tm, d), BF16), pltpu.VMEM((tm, d), F32), pltpu.VMEM((nj, 2, 8, tf), F32)]
    else:
        bps = 1
        spb = tm // Lb
        s_spec = pl.BlockSpec((spb, 2, tf), lambda i, j: (i, 0, j))
        extra_in = [hist, hist]
        extra_specs = [pl.BlockSpec((2, tm, tf), lambda i, j: (0, i, j)),
                       pl.BlockSpec((2, tm, tf), lambda i, j: (0, i, j + nj))]
        scratch = [pltpu.VMEM((tm, d), BF16), pltpu.VMEM((tm, d), F32)]
    y, sg, sv = pl.pallas_call(
        functools.partial(_ffn_kernel, tm=tm, Lb=Lb, prompt=prompt, blocks_per_seq=bps),
        grid=(n // tm, nj),
        in_specs=[pl.BlockSpec((tm, d), lambda i, j: (i, 0))] + extra_specs + [
            pl.BlockSpec((1, d), lambda i, j: (0, 0)),
            pl.BlockSpec((d, tf), lambda i, j: (0, j)),
            pl.BlockSpec((d, tf), lambda i, j: (0, j + nj)),
            pl.BlockSpec((3, tf), lambda i, j: (0, j)),
            pl.BlockSpec((3, tf), lambda i, j: (0, j + nj)),
            pl.BlockSpec((1, tf), lambda i, j: (0, j)),
            pl.BlockSpec((1, tf), lambda i, j: (0, j + nj)),
            pl.BlockSpec((tf, d), lambda i, j: (j, 0)),
            pl.BlockSpec((1, d), lambda i, j: (0, 0))],
        out_specs=[pl.BlockSpec((tm, d), lambda i, j: (i, 0)), s_spec, s_spec],
        out_shape=[jax.ShapeDtypeStruct((n, d), F32),
                   jax.ShapeDtypeStruct((nseq, 2, D_FF), F32),
                   jax.ShapeDtypeStruct((nseq, 2, D_FF), F32)],
        scratch_shapes=scratch,
        compiler_params=_cparams("arbitrary", "arbitrary"),
        name="convffn",
    )(x, *extra_in, p["norm_ffn_w"].reshape(1, d), w_up, w_up, cw, cw, cb, cb, w_down,
      final_g.reshape(1, d))
    if prompt:
        sg, sv = sg[bps - 1::bps], sv[bps - 1::bps]
    return y, jnp.concatenate([sg, sv], axis=-1)


def _pair_blockdiag(s):
    b = s.shape[0]
    s = s.reshape(b, 4, 2, 64, 64)
    z = jnp.zeros_like(s[:, :, 0])
    top = jnp.concatenate([s[:, :, 0], z], axis=-1)
    bot = jnp.concatenate([z, s[:, :, 1]], axis=-1)
    return jnp.concatenate([top, bot], axis=-2)


def _pair_blocks(sbd):
    b = sbd.shape[0]
    s0 = sbd[:, :, 0:64, 0:64]
    s1 = sbd[:, :, 64:128, 64:128]
    return jnp.stack([s0, s1], axis=2).reshape(b, 8, 64, 64)


def _hist_rows(state, seq_len, k):
    b, w, c = state.shape
    cols = [state[:, w + t - k] if t < k else jnp.zeros((b, c), state.dtype) for t in range(seq_len)]
    return jnp.stack(cols, axis=1).reshape(b * seq_len, c)


def kernel(x_prompt, x_sample, mem_prompt, state_ssm_conv, state_ssm, state_shift, state_wkv, state_ffn_conv, cache_mem_k, cache_mem_v, norm_mix_w, w_in, ssd_conv_w, ssd_conv_b, ssd_dt_bias, ssd_a_log, ssd_d, ssd_norm_w, rwkv_mu, rwkv_w0, rwkv_w2, rwkv_a0, rwkv_a2, rwkv_g2, rwkv_k_k, rwkv_k_a, rwkv_r_k, rwkv_ln_w, rwkv_ln_b, w_out, norm_xa_w, mem_norm_w, xa_w_q, xa_w_k, xa_w_v, xa_w_o, norm_ffn_w, ffn_w_up, ffn_conv_w, ffn_conv_b, ffn_w_down, final_norm_w):
    depth = w_in.shape[0]
    assert depth == 1, "final rmsnorm is fused into the (single) layer's ConvFFN kernel"
    bp, lp, d = x_prompt.shape
    bs, ls, _ = x_sample.shape
    i = 0
    p = dict(ssd_conv_w=ssd_conv_w[i], ssd_conv_b=ssd_conv_b[i], ssd_dt_bias=ssd_dt_bias[i],
             ssd_a_log=ssd_a_log[i], ssd_d=ssd_d[i], ssd_norm_w=ssd_norm_w[i], rwkv_mu=rwkv_mu[i],
             rwkv_w0=rwkv_w0[i], rwkv_w2=rwkv_w2[i], rwkv_a0=rwkv_a0[i], rwkv_a2=rwkv_a2[i],
             rwkv_g2=rwkv_g2[i], rwkv_k_k=rwkv_k_k[i], rwkv_k_a=rwkv_k_a[i],
             rwkv_r_k=rwkv_r_k[i].reshape(-1), rwkv_ln_w=rwkv_ln_w[i], rwkv_ln_b=rwkv_ln_b[i],
             norm_ffn_w=norm_ffn_w[i], ffn_w_up=ffn_w_up[i], ffn_conv_w=ffn_conv_w[i],
             ffn_conv_b=ffn_conv_b[i], ffn_w_down=ffn_w_down[i])

    w_in_p = jnp.concatenate([w_in[i][:, :SSD_PROJ], jnp.zeros((d, SEG - SSD_PROJ), F32),
                              w_in[i][:, SSD_PROJ:]], axis=1).astype(BF16)
    w_out_b = w_out[i].astype(BF16)
    w_q_b = (xa_w_q[i]).astype(BF16)
    w_o_b = xa_w_o[i].astype(BF16)
    w_kv_b = jnp.concatenate([xa_w_k[i], xa_w_v[i]], axis=1).astype(BF16)

    xp = x_prompt.reshape(bp * lp, d)
    xs = x_sample.reshape(bs * ls, d)

    kv = _norm_proj(mem_prompt.reshape(bp * N_MEM, d), mem_norm_w[i], w_kv_b, 512, 1024)
    mk = kv[:, :d].reshape(bp, N_MEM, d)
    mv = kv[:, d:].reshape(bp, N_MEM, d)
    u_p = _norm_proj(xp, norm_mix_w[i], w_in_p, 1024, 512)
    y_ssd_p, ssm_p = _ssd_prompt(u_p, p, bp, lp)
    y_rw_p, wkv_bd_p = _wkv_prompt(u_p, p, bp, lp)
    x1_p, q_p = _outproj(xp, y_ssd_p, y_rw_p, w_out_b, norm_xa_w[i], w_q_b, 512)
    x2_p = _xattn(q_p, x1_p, mk, mv, w_o_b, 512, lp)
    y_p, ffn_conv_p = _ffn(x2_p, None, p, final_norm_w, 512, 1408, lp, True, lp)
    u3 = u_p.reshape(bp, lp, 2 * SEG)
    ssm_conv_p = u3[:, lp - 3:, SSD_WIDTH:SSD_WIDTH + SSD_XBC]
    shift_p = u3[:, lp - 1, SEG:]

    u_s = _norm_proj(xs, norm_mix_w[i], w_in_p, 512, 512)
    u3s = u_s.reshape(bs, ls, 2 * SEG)
    xbc_s = u3s[:, :, SSD_WIDTH:SSD_WIDTH + SSD_XBC]
    conv_state = state_ssm_conv[i]
    hist_conv = jnp.stack([_hist_rows(conv_state, ls, k) for k in (1, 2, 3)], axis=0)
    y_ssd_s, ssm_s = _ssd_sample(u_s, hist_conv, state_ssm[i].reshape(bs, 4, LANES, LANES), p, ls)
    hist_shift = _hist_rows(state_shift[i][:, None, :], ls, 1)
    y_rw_s, wkv_bd_s = _wkv_sample(u_s, hist_shift, _pair_blockdiag(state_wkv[i]), p, ls)
    x1_s, q_s = _outproj(xs, y_ssd_s, y_rw_s, w_out_b, norm_xa_w[i], w_q_b, 512)
    ck = cache_mem_k[i].reshape(bs, N_MEM, d)
    cv = cache_mem_v[i].reshape(bs, N_MEM, d)
    x2_s = _xattn(q_s, x1_s, ck, cv, w_o_b, 8, ls)
    fstate = state_ffn_conv[i]
    hist_ffn = jnp.stack([_hist_rows(fstate, ls, k) for k in (1, 2)], axis=0)
    y_s, ffn_conv_s = _ffn(x2_s, hist_ffn, p, final_norm_w, 256, 1408, ls, False, ls)
    ssm_conv_s = jnp.concatenate([conv_state, xbc_s], axis=1)[:, -3:]
    shift_s = u3s[:, ls - 1, SEG:]

    e = lambda a: a[None]
    return (y_p.reshape(bp, lp, d), y_s.reshape(bs, ls, d),
            e(ssm_conv_p), e(ssm_conv_s),
            e(ssm_p.reshape(bp, SSD_HEADS, 64, SSD_STATE)), e(ssm_s.reshape(bs, SSD_HEADS, 64, SSD_STATE)),
            e(shift_p), e(shift_s),
            e(_pair_blocks(wkv_bd_p)), e(_pair_blocks(wkv_bd_s)),
            e(ffn_conv_p), e(ffn_conv_s),
            e(mk.reshape(bp, N_MEM, XA_HEADS, XA_HEAD_DIM)), e(mv.reshape(bp, N_MEM, XA_HEADS, XA_HEAD_DIM)))
```

```python
import functools
import math

import jax
import jax.numpy as jnp
from jax import lax
from jax.experimental import pallas as pl
from jax.experimental.pallas import tpu as pltpu

F32 = jnp.float32
BF16 = jnp.bfloat16

D_MODEL = 1024
N_MEM = 256
XA_HEADS = 4
XA_HEAD_DIM = D_MODEL // XA_HEADS
SSD_WIDTH = 512
SSD_HEADS = 8
SSD_STATE = 128
SSD_XBC = 1024
SSD_PROJ = SSD_WIDTH + SSD_XBC + SSD_HEADS
RWKV_WIDTH = 512
RWKV_PROJ = 1792
D_FF = 2816
EPS = 1e-6
GN_EPS = 64e-5

LANES = 128
SEG = 1792
DT_OFF = SSD_WIDTH + SSD_XBC
VMEM_LIMIT_BYTES = 56 * 1024 * 1024


def _cparams(*sem):
    return pltpu.CompilerParams(dimension_semantics=sem, vmem_limit_bytes=VMEM_LIMIT_BYTES)


_NN = (((1,), (0,)), ((), ()))
_NT = (((1,), (1,)), ((), ()))
_TN = (((0,), (0,)), ((), ()))


def _bdot(a, b, dims=_NN):
    return lax.dot_general(a.astype(BF16), b.astype(BF16), dims, preferred_element_type=F32)


def _split2(x):
    hi = x.astype(BF16)
    lo = (x - hi.astype(F32)).astype(BF16)
    return hi, lo


def _dot3(a, b, dims=_NN):
    ah, al = _split2(a)
    bh, bl = _split2(b)
    d = functools.partial(lax.dot_general, dimension_numbers=dims, preferred_element_type=F32)
    return d(ah, bh) + (d(ah, bl) + d(al, bh))


def _maskdot(m_bf16, x):
    hi = x.astype(BF16)
    r1 = x - hi.astype(F32)
    mid = r1.astype(BF16)
    lo = (r1 - mid.astype(F32)).astype(BF16)
    d = functools.partial(jnp.dot, preferred_element_type=F32)
    return d(m_bf16, hi) + (d(m_bf16, mid) + d(m_bf16, lo))


def _xmask(x, m_bf16):
    hi, lo = _split2(x)
    d = functools.partial(jnp.dot, preferred_element_type=F32)
    return d(hi, m_bf16) + d(lo, m_bf16)


def _sigmoid(x):
    return 0.5 + 0.5 * jnp.tanh(0.5 * x)


def _softplus(x):
    return jnp.maximum(x, 0.0) + jnp.log1p(jnp.exp(-jnp.abs(x)))


def _rms(x, g):
    return x * lax.rsqrt(jnp.mean(x * x, axis=-1, keepdims=True) + EPS) * g


def _iota2(shape, dim):
    return lax.broadcasted_iota(jnp.int32, shape, dim)


def _shifted(x, k, tpos, hist):
    return jnp.where(tpos >= k, pltpu.roll(x, k, 0), hist)


def _norm_proj_kernel(x_ref, g_ref, w_ref, o_ref, hn_sc):
    @pl.when(pl.program_id(1) == 0)
    def _():
        hn_sc[...] = _rms(x_ref[...], g_ref[...]).astype(BF16)

    o_ref[...] = jnp.dot(hn_sc[...], w_ref[...], preferred_element_type=F32)


def _norm_proj(x, g, w_bf16, tm, tn):
    n, d = x.shape
    f = w_bf16.shape[1]
    return pl.pallas_call(
        _norm_proj_kernel,
        grid=(n // tm, f // tn),
        in_specs=[pl.BlockSpec((tm, d), lambda i, j: (i, 0)),
                  pl.BlockSpec((1, d), lambda i, j: (0, 0)),
                  pl.BlockSpec((d, tn), lambda i, j: (0, j))],
        out_specs=pl.BlockSpec((tm, tn), lambda i, j: (i, j)),
        out_shape=jax.ShapeDtypeStruct((n, f), F32),
        scratch_shapes=[pltpu.VMEM((tm, d), BF16)],
        compiler_params=_cparams("parallel", "arbitrary"),
        name="norm_proj",
    )(x, g.reshape(1, d), w_bf16)


def _mem_kv_kernel(x_ref, g_ref, w_ref, k_ref, v_ref):
    hn = _rms(x_ref[...], g_ref[...]).astype(BF16)
    d = k_ref.shape[1]
    k_ref[...] = jnp.dot(hn, w_ref[:, 0:d], preferred_element_type=F32)
    v_ref[...] = jnp.dot(hn, w_ref[:, d:2 * d], preferred_element_type=F32)


def _mem_kv(x, g, w_kv_bf16, tm):
    n, d = x.shape
    row_spec = pl.BlockSpec((tm, d), lambda i: (i, 0))
    return pl.pallas_call(
        _mem_kv_kernel,
        grid=(n // tm,),
        in_specs=[row_spec, pl.BlockSpec((1, d), lambda i: (0, 0)), _const_spec(w_kv_bf16)],
        out_specs=[row_spec, row_spec],
        out_shape=[jax.ShapeDtypeStruct((n, d), F32)] * 2,
        compiler_params=_cparams("parallel"),
        name="mem_kv",
    )(x, g.reshape(1, d), w_kv_bf16)


def _ssd_kernel(*refs, Q, Lb, prompt):
    ns = Q // Lb
    lb = int(math.log2(Lb))
    if prompt:
        (u_ref, cw_ref, cb_ref, dtb_ref, an_ref, dsk_ref, nw_ref,
         y_ref, hout_ref, carry_sc, h_sc) = refs
        c = pl.program_id(1)

        @pl.when(c == 0)
        def _():
            carry_sc[...] = jnp.zeros_like(carry_sc)
            h_sc[...] = jnp.zeros_like(h_sc)
    else:
        (u_ref, hist_ref, hin_ref, cw_ref, cb_ref, dtb_ref, an_ref, dsk_ref, nw_ref,
         y_ref, hout_ref) = refs

    u = u_ref[...]
    z = u[:, 0:SSD_WIDTH]
    x = u[:, SSD_WIDTH:SSD_WIDTH + SSD_XBC]
    dtr = u[:, DT_OFF:DT_OFF + LANES]

    rows = _iota2((Q, 1), 0)
    tpos = rows & (Lb - 1)
    if prompt:
        c8 = carry_sc[...]
        h1 = c8[7:8]
        h2 = jnp.where(rows == 0, c8[6:7], c8[7:8])
        h3 = jnp.where(rows == 0, c8[5:6], jnp.where(rows == 1, c8[6:7], c8[7:8]))
        carry_sc[...] = x[Q - 8:Q]
    else:
        h1, h2, h3 = hist_ref[0], hist_ref[1], hist_ref[2]
    cw = cw_ref[...]
    xc = (cw[0:1] * _shifted(x, 3, tpos, h3) + cw[1:2] * _shifted(x, 2, tpos, h2)
          + cw[2:3] * _shifted(x, 1, tpos, h1) + cw[3:4] * x) + cb_ref[...]
    xc = xc * _sigmoid(xc)
    xs = xc[:, 0:SSD_WIDTH]
    bm = xc[:, SSD_WIDTH:SSD_WIDTH + 2 * SSD_STATE]
    cm = xc[:, SSD_WIDTH + 2 * SSD_STATE:]

    dt = _softplus(dtr + dtb_ref[...])
    da = dt * an_ref[...]

    ri = _iota2((Q, Q), 0)
    ci = _iota2((Q, Q), 1)
    same = (ri >> lb) == (ci >> lb)
    tril = same & (ci <= ri)
    sel = ci == (((ri >> lb) << lb) + (Lb - 1))
    acs = _maskdot(tril.astype(BF16), da)
    acs_t = acs.T
    acs_last = _maskdot(sel.astype(BF16), acs)
    dec_end = jnp.exp(acs_last - acs)
    eacs = jnp.exp(acs)
    seqid = rows >> lb

    lane = _iota2((Q, LANES), 1)
    lo_half = lane < 64
    prow = _iota2((LANES, LANES), 0)
    dsk = dsk_ref[...]

    ys = []
    for q in range(4):
        g = q // 2
        h0, h1i = 2 * q, 2 * q + 1
        if q % 2 == 0:
            cg = cm[:, g * SSD_STATE:(g + 1) * SSD_STATE]
            bg = bm[:, g * SSD_STATE:(g + 1) * SSD_STATE]
            cb_g = _bdot(cg, bg, _NT)
            if ns == 1:
                cexp, bexp = cg.astype(BF16), bg.astype(BF16)
            else:
                cexp = jnp.concatenate([jnp.where(seqid == b, cg, 0.0).astype(BF16) for b in range(ns)], axis=1)
                bexp = jnp.concatenate([jnp.where(seqid == b, bg, 0.0).astype(BF16) for b in range(ns)], axis=1)
        m0 = jnp.where(tril, cb_g * jnp.exp(acs[:, h0:h0 + 1] - acs_t[h0:h0 + 1, :]), 0.0)
        m1 = jnp.where(tril, cb_g * jnp.exp(acs[:, h1i:h1i + 1] - acs_t[h1i:h1i + 1, :]), 0.0)
        xp = xs[:, q * LANES:(q + 1) * LANES]
        xdt = xp * jnp.where(lo_half, dt[:, h0:h0 + 1], dt[:, h1i:h1i + 1])
        xdt0 = jnp.where(lo_half, xdt, 0.0)
        xdt1 = jnp.where(lo_half, 0.0, xdt)
        ydiag = _bdot(jnp.concatenate([m0, m1], axis=1), jnp.concatenate([xdt0, xdt1], axis=0))
        if prompt:
            hst = h_sc[q]
        else:
            hst = jnp.concatenate([hin_ref[b, q] for b in range(ns)], axis=1)
        ecs = jnp.where(lo_half, eacs[:, h0:h0 + 1], eacs[:, h1i:h1i + 1])
        yoff = _bdot(cexp, hst, _NT) * ecs
        xd = xdt * jnp.where(lo_half, dec_end[:, h0:h0 + 1], dec_end[:, h1i:h1i + 1])
        incr = _bdot(xd, bexp, _TN)
        scales = []
        for b in range(ns):
            r = b * Lb + Lb - 1
            e = eacs[r:r + 1, :]
            scales.append(jnp.where(prow < 64, e[:, h0:h0 + 1], e[:, h1i:h1i + 1]))
        scale = scales[0] if ns == 1 else jnp.concatenate(scales, axis=1)
        hnew = hst * scale + incr
        if prompt:
            h_sc[q] = hnew
            hout_ref[0, q] = hnew
        else:
            for b in range(ns):
                hout_ref[b, q] = hnew[:, b * LANES:(b + 1) * LANES]
        ys.append(ydiag + yoff + dsk[:, q * LANES:(q + 1) * LANES] * xp)

    y = jnp.concatenate(ys, axis=1)
    y = y * (z * _sigmoid(z))
    half = SSD_WIDTH // 2
    outs = []
    for g in range(2):
        yg = y[:, g * half:(g + 1) * half]
        outs.append(yg * lax.rsqrt(jnp.mean(yg * yg, axis=-1, keepdims=True) + EPS))
    y_ref[...] = jnp.concatenate(outs, axis=1) * nw_ref[...]


def _ssd_params(p):
    an = jnp.zeros((1, LANES), F32).at[0, :SSD_HEADS].set(-jnp.exp(p["ssd_a_log"]))
    dtb = jnp.zeros((1, LANES), F32).at[0, :SSD_HEADS].set(p["ssd_dt_bias"])
    dsk = jnp.repeat(p["ssd_d"], SSD_WIDTH // SSD_HEADS).reshape(1, SSD_WIDTH)
    return [p["ssd_conv_w"], p["ssd_conv_b"].reshape(1, SSD_XBC), dtb, an, dsk,
            p["ssd_norm_w"].reshape(1, SSD_WIDTH)]


def _const_spec(a):
    nd = a.ndim
    return pl.BlockSpec(a.shape, lambda *_: (0,) * nd)


def _ssd_prompt(u, p, batch, seq, Q=128):
    n = u.shape[0]
    nc = seq // Q
    params = _ssd_params(p)
    y, hout = pl.pallas_call(
        functools.partial(_ssd_kernel, Q=Q, Lb=Q, prompt=True),
        grid=(batch, nc),
        in_specs=[pl.BlockSpec((Q, SEG), lambda b, c: (b * nc + c, 0))] + [_const_spec(a) for a in params],
        out_specs=[pl.BlockSpec((Q, SSD_WIDTH), lambda b, c: (b * nc + c, 0)),
                   pl.BlockSpec((1, 4, LANES, LANES), lambda b, c: (b, 0, 0, 0))],
        out_shape=[jax.ShapeDtypeStruct((n, SSD_WIDTH), F32),
                   jax.ShapeDtypeStruct((batch, 4, LANES, LANES), F32)],
        scratch_shapes=[pltpu.VMEM((8, SSD_XBC), F32), pltpu.VMEM((4, LANES, LANES), F32)],
        compiler_params=_cparams("parallel", "arbitrary"),
        name="ssd_prompt",
    )(u, *params)
    return y, hout


def _ssd_sample(u, hist, hin, p, Lb, Q=64):
    n = u.shape[0]
    ns = Q // Lb
    params = _ssd_params(p)
    y, hout = pl.pallas_call(
        functools.partial(_ssd_kernel, Q=Q, Lb=Lb, prompt=False),
        grid=(n // Q,),
        in_specs=[pl.BlockSpec((Q, SEG), lambda i: (i, 0)),
                  pl.BlockSpec((3, Q, SSD_XBC), lambda i: (0, i, 0)),
                  pl.BlockSpec((ns, 4, LANES, LANES), lambda i: (i, 0, 0, 0))]
                 + [_const_spec(a) for a in params],
        out_specs=[pl.BlockSpec((Q, SSD_WIDTH), lambda i: (i, 0)),
                   pl.BlockSpec((ns, 4, LANES, LANES), lambda i: (i, 0, 0, 0))],
        out_shape=[jax.ShapeDtypeStruct((n, SSD_WIDTH), F32),
                   jax.ShapeDtypeStruct(hin.shape, F32)],
        compiler_params=_cparams("parallel"),
        name="ssd_sample",
    )(u, hist, hin, *params)
    return y, hout


def _dot3s(a, b, dims=_NN):
    d = functools.partial(lax.dot_general, dimension_numbers=dims, preferred_element_type=F32)
    return d(a[0], b[0]) + (d(a[0], b[1]) + d(a[1], b[0]))


def _cat2(parts, axis):
    return (jnp.concatenate([p[0] for p in parts], axis=axis), jnp.concatenate([p[1] for p in parts], axis=axis))


def _tri_inverse(mats, lb):
    n = mats[0].shape[0]
    ri = _iota2((n, n), 0)
    ci = _iota2((n, n), 1)
    off1 = ((ri >> 1) == (ci >> 1)) & ((ri & 1) == 1) & ((ci & 1) == 0)
    eye = jnp.where(ri == ci, 1.0, 0.0)
    ts = [eye + jnp.where(off1, a, 0.0) for a in mats]
    for lvl in range(1, lb):
        m = 1 << lvl
        off = ((ri >> (lvl + 1)) == (ci >> (lvl + 1))) & ((ri & (2 * m - 1)) >= m) & ((ci & (2 * m - 1)) < m)
        tsp = [_split2(t) for t in ts]
        ws = [_dot3s(_split2(jnp.where(off, a, 0.0)), tp) for a, tp in zip(mats, tsp)]
        ts = [t + _dot3s(tp, _split2(w)) for t, tp, w in zip(ts, tsp, ws)]
    return ts


def _wkv_kernel(*refs, C, Lb, prompt):
    nb = C // Lb
    lb = int(math.log2(Lb))
    R = 2 * C
    if prompt:
        (u_ref, mu_ref, w0_ref, w2_ref, a0_ref, a2_ref, g2_ref, kk_ref, ka_ref, rk_ref, lnw_ref, lnb_ref,
         y_ref, sout_ref, carry_sc, s_sc) = refs
        c = pl.program_id(1)

        @pl.when(c == 0)
        def _():
            carry_sc[...] = jnp.zeros_like(carry_sc)
            s_sc[...] = jnp.zeros_like(s_sc)
    else:
        (u_ref, hist_ref, sin_ref, mu_ref, w0_ref, w2_ref, a0_ref, a2_ref, g2_ref, kk_ref, ka_ref, rk_ref,
         lnw_ref, lnb_ref, y_ref, sout_ref) = refs

    u = u_ref[...]
    rows = _iota2((C, 1), 0)
    tpos = rows & (Lb - 1)
    if prompt:
        hist = carry_sc[7:8]
        carry_sc[...] = u[C - 8:C]
    else:
        hist = hist_ref[...]
    um = u + (_shifted(u, 1, tpos, hist) - u) * mu_ref[...]

    W = RWKV_WIDTH
    r = um[:, 0:W]
    k = um[:, W:2 * W]
    v = um[:, 2 * W:3 * W]
    t12 = um[:, 3 * W:3 * W + LANES]
    lg = um[:, 3 * W + LANES:3 * W + 2 * LANES]

    wl = w0_ref[...] + _bdot(jnp.tanh(t12), w2_ref[...])
    logw = -jnp.exp(-_softplus(-wl) - 0.5)
    a = _sigmoid(a0_ref[...] + _bdot(t12, a2_ref[...]))
    g = _bdot(_sigmoid(lg), g2_ref[...])

    hi = _iota2((W, W), 0)
    hj = _iota2((W, W), 1)
    headsum = ((hi >> 6) == (hj >> 6)).astype(BF16)

    kk = k * kk_ref[...]
    kk = kk / jnp.maximum(jnp.sqrt(_xmask(kk * kk, headsum)), 1e-12)
    kmod = k * (1.0 + (a - 1.0) * ka_ref[...])
    beta = kk * a

    ri = _iota2((C, C), 0)
    ci = _iota2((C, C), 1)
    same_c = (ri >> lb) == (ci >> lb)
    lc = _maskdot((same_c & (ci <= ri)).astype(BF16), logw)
    lc_last = _maskdot((ci == (((ri >> lb) << lb) + (Lb - 1))).astype(BF16), lc)
    e_neg = jnp.exp(-lc)
    e_end = jnp.exp(lc_last - lc)
    at = -kk * jnp.exp(lc - logw)
    rt = r * jnp.exp(lc)
    bt = beta * e_neg
    kt = kmod * e_neg
    bh = beta * e_end
    kh = kmod * e_end
    p_last = jnp.exp(lc_last)

    si = _iota2((R, R), 0)
    sj = _iota2((R, R), 1)
    same_s = (si >> lb) == (sj >> lb)
    strict = same_s & (sj < si)
    incl = same_s & (sj <= si)
    lane = _iota2((C, LANES), 1)
    lo_half = lane < 64
    seq_s = (_iota2((R, 1), 0) & (C - 1)) >> lb

    def stack(xp):
        return jnp.concatenate([jnp.where(lo_half, xp, 0.0), jnp.where(lo_half, 0.0, xp)], axis=0)

    def expand(xs_):
        if nb == 1:
            return xs_
        return jnp.concatenate([jnp.where(seq_s == b, xs_, 0.0) for b in range(nb)], axis=1)

    P4 = range(4)
    sls = [slice(p * LANES, (p + 1) * LANES) for p in P4]
    a_s = [_split2(stack(at[:, s])) for s in sls]
    r_s = [_split2(stack(rt[:, s])) for s in sls]
    v_s = [_split2(stack(v[:, s])) for s in sls]
    bk_s = [_split2(jnp.concatenate([stack(bt[:, s]), stack(kt[:, s])], axis=0)) for s in sls]
    gm = [_dot3s(_cat2([a_s[p], r_s[p]], 0), bk_s[p], _NT) for p in P4]
    a_ab = [jnp.where(strict, gm[p][0:R, 0:R], 0.0) for p in P4]
    a_ak = [_split2(jnp.where(strict, gm[p][0:R, R:2 * R], 0.0)) for p in P4]
    a_r = [_split2(jnp.concatenate([jnp.where(incl, gm[p][R:2 * R, 0:R], 0.0),
                                    jnp.where(incl, gm[p][R:2 * R, R:2 * R], 0.0)], axis=1)) for p in P4]
    akv = [_dot3s(a_ak[p], v_s[p]) for p in P4]
    tinv = _tri_inverse(a_ab, lb)

    if prompt:
        sst = [s_sc[p] for p in P4]
    else:
        z64 = jnp.zeros((64, 64), F32)

        def pair_blockdiag(b, p):
            top = jnp.concatenate([sin_ref[b, 2 * p], z64], axis=1)
            bot = jnp.concatenate([z64, sin_ref[b, 2 * p + 1]], axis=1)
            return jnp.concatenate([top, bot], axis=0)

        sst = [jnp.concatenate([pair_blockdiag(b, p) for b in range(nb)], axis=1) for p in P4]
    if nb == 1:
        ar_x = [_cat2([a_s[p], r_s[p]], 0) for p in P4]
    else:
        ar_x = [_split2(jnp.concatenate([expand(stack(at[:, s])), expand(stack(rt[:, s]))], axis=0)) for s in sls]
    ar0 = [_dot3s(ar_x[p], _split2(sst[p]), _NT) for p in P4]
    us = [_split2(_dot3s(_split2(tinv[p]), _split2(ar0[p][0:R] + akv[p]))) for p in P4]
    uv = [_cat2([us[p], v_s[p]], 0) for p in P4]
    yst = [ar0[p][R:2 * R] + _dot3s(a_r[p], uv[p]) for p in P4]
    ys = [yst[p][0:C] + yst[p][C:R] for p in P4]
    for p in P4:
        s = sls[p]
        pl_lanes = p_last[:, s]
        if nb == 1:
            plast = pl_lanes[0:1]
        else:
            plast = jnp.concatenate([pl_lanes[b * Lb:b * Lb + 1] for b in range(nb)], axis=1)
        bkh = _split2(jnp.concatenate([expand(stack(bh[:, s])), expand(stack(kh[:, s]))], axis=0))
        snew = sst[p] * plast + _dot3s(uv[p], bkh, _TN)
        if prompt:
            s_sc[p] = snew
            sout_ref[0, p] = snew
        else:
            for b in range(nb):
                sout_ref[b, 2 * p] = snew[0:64, b * LANES:b * LANES + 64]
                sout_ref[b, 2 * p + 1] = snew[64:128, b * LANES + 64:(b + 1) * LANES]

    y = jnp.concatenate(ys, axis=1)
    inv_d = 1.0 / 64.0
    mean = _xmask(y, headsum) * inv_d
    yc = y - mean
    var = _xmask(yc * yc, headsum) * inv_d
    yn = yc * lax.rsqrt(var + GN_EPS) * lnw_ref[...] + lnb_ref[...]
    yn = yn + _xmask(r * kmod * rk_ref[...], headsum) * v
    y_ref[...] = yn * g


def _wkv_params(p):
    z64 = jnp.zeros((64, RWKV_WIDTH), F32)
    w2 = jnp.concatenate([p["rwkv_w2"], z64], axis=0).astype(BF16)
    a2 = jnp.concatenate([z64, p["rwkv_a2"]], axis=0).astype(BF16)
    row = lambda a: a.reshape(1, -1)
    return [row(p["rwkv_mu"]), row(p["rwkv_w0"]), w2, row(p["rwkv_a0"]), a2, p["rwkv_g2"].astype(BF16),
            row(p["rwkv_k_k"]), row(p["rwkv_k_a"]), row(p["rwkv_r_k"]), row(p["rwkv_ln_w"]), row(p["rwkv_ln_b"])]


def _wkv_prompt(u, p, batch, seq, C=64):
    n = u.shape[0]
    nc = seq // C
    params = _wkv_params(p)
    return pl.pallas_call(
        functools.partial(_wkv_kernel, C=C, Lb=C, prompt=True),
        grid=(batch, nc),
        in_specs=[pl.BlockSpec((C, SEG), lambda b, c: (b * nc + c, 1))] + [_const_spec(a) for a in params],
        out_specs=[pl.BlockSpec((C, RWKV_WIDTH), lambda b, c: (b * nc + c, 0)),
                   pl.BlockSpec((1, 4, LANES, LANES), lambda b, c: (b, 0, 0, 0))],
        out_shape=[jax.ShapeDtypeStruct((n, RWKV_WIDTH), F32),
                   jax.ShapeDtypeStruct((batch, 4, LANES, LANES), F32)],
        scratch_shapes=[pltpu.VMEM((8, SEG), F32), pltpu.VMEM((4, LANES, LANES), F32)],
        compiler_params=_cparams("parallel", "arbitrary"),
        name="wkv_prompt",
    )(u, *params)


def _wkv_sample(u, hist, sin, p, Lb, C=64):
    n = u.shape[0]
    nb = C // Lb
    params = _wkv_params(p)
    return pl.pallas_call(
        functools.partial(_wkv_kernel, C=C, Lb=Lb, prompt=False),
        grid=(n // C,),
        in_specs=[pl.BlockSpec((C, SEG), lambda i: (i, 1)),
                  pl.BlockSpec((C, SEG), lambda i: (i, 0)),
                  pl.BlockSpec((nb, 8, 64, 64), lambda i: (i, 0, 0, 0))]
                 + [_const_spec(a) for a in params],
        out_specs=[pl.BlockSpec((C, RWKV_WIDTH), lambda i: (i, 0)),
                   pl.BlockSpec((nb, 8, 64, 64), lambda i: (i, 0, 0, 0))],
        out_shape=[jax.ShapeDtypeStruct((n, RWKV_WIDTH), F32),
                   jax.ShapeDtypeStruct(sin.shape, F32)],
        compiler_params=_cparams("parallel"),
        name="wkv_sample",
    )(u, hist, sin, *params)


def _outproj_kernel(x_ref, ys_ref, yr_ref, wo_ref, g_ref, wq_ref, x1_ref, q_ref):
    wo = wo_ref[...]
    x1 = x_ref[...] + (_bdot(ys_ref[...], wo[0:SSD_WIDTH]) + _bdot(yr_ref[...], wo[SSD_WIDTH:]))
    x1_ref[...] = x1
    q_ref[...] = _bdot(_rms(x1, g_ref[...]), wq_ref[...])


def _outproj(x, y_ssd, y_rw, w_out, g, w_q, tm):
    n, d = x.shape
    return pl.pallas_call(
        _outproj_kernel,
        grid=(n // tm,),
        in_specs=[pl.BlockSpec((tm, d), lambda i: (i, 0)),
                  pl.BlockSpec((tm, SSD_WIDTH), lambda i: (i, 0)),
                  pl.BlockSpec((tm, RWKV_WIDTH), lambda i: (i, 0)),
                  _const_spec(w_out), pl.BlockSpec((1, d), lambda i: (0, 0)), _const_spec(w_q)],
        out_specs=[pl.BlockSpec((tm, d), lambda i: (i, 0)), pl.BlockSpec((tm, d), lambda i: (i, 0))],
        out_shape=[jax.ShapeDtypeStruct((n, d), F32), jax.ShapeDtypeStruct((n, d), F32)],
        compiler_params=_cparams("parallel"),
        name="outproj_q",
    )(x, y_ssd, y_rw, w_out, g.reshape(1, d), w_q)


def _xattn_kernel(q_ref, x1_ref, k_ref, v_ref, wo_ref, o_ref):
    q = q_ref[...]
    scale = XA_HEAD_DIM ** -0.5
    outs = []
    for h in range(XA_HEADS):
        sl = slice(h * XA_HEAD_DIM, (h + 1) * XA_HEAD_DIM)
        s = _bdot(q[:, sl], k_ref[0, :, sl], _NT) * scale
        s = s - jnp.max(s, axis=-1, keepdims=True)
        e = jnp.exp(s)
        pr = e / jnp.sum(e, axis=-1, keepdims=True)
        outs.append(_bdot(pr, v_ref[0, :, sl]))
    o_ref[...] = x1_ref[...] + _bdot(jnp.concatenate(outs, axis=1), wo_ref[...])


def _xattn(q, x1, mk, mv, w_o, tq, seq_len):
    n, d = q.shape
    per = seq_len // tq
    row_spec = pl.BlockSpec((tq, d), lambda i: (i, 0))
    kv_spec = pl.BlockSpec((1, N_MEM, d), lambda i: (i // per, 0, 0))
    return pl.pallas_call(
        _xattn_kernel,
        grid=(n // tq,),
        in_specs=[row_spec, row_spec, kv_spec, kv_spec, _const_spec(w_o)],
        out_specs=row_spec,
        out_shape=jax.ShapeDtypeStruct((n, d), F32),
        compiler_params=_cparams("parallel"),
        name="xattn",
    )(q, x1, mk, mv, w_o)


def _xattn_cache_kernel(q_ref, x1_ref, k_ref, v_ref, wo_ref, o_ref, *, seq_len):
    tq = q_ref.shape[0]
    nmem = tq // seq_len
    rows = N_MEM * XA_HEADS
    cols = XA_HEADS * tq
    q = q_ref[...] * (XA_HEAD_DIM ** -0.5)
    qblk = jnp.concatenate([q[:, h * XA_HEAD_DIM:(h + 1) * XA_HEAD_DIM] for h in range(XA_HEADS)], axis=0)
    key_head = _iota2((cols, rows), 1) & (XA_HEADS - 1)
    qrow = _iota2((cols, 1), 0)
    head_ok = key_head == (qrow >> int(math.log2(tq)))
    out_mem = _iota2((tq, 1), 0) >> int(math.log2(seq_len))
    o = None
    for m in range(nmem):
        kall = k_ref[m].reshape(rows, XA_HEAD_DIM)
        vall = v_ref[m].reshape(rows, XA_HEAD_DIM)
        s = jnp.where(head_ok, _bdot(qblk, kall, _NT), -jnp.inf)
        e = jnp.exp(s - jnp.max(s, axis=-1, keepdims=True))
        pr = e / jnp.sum(e, axis=-1, keepdims=True)
        om = _bdot(pr, vall)
        om = jnp.concatenate([om[h * tq:(h + 1) * tq] for h in range(XA_HEADS)], axis=1)
        o = om if o is None else jnp.where(out_mem == m, om, o)
    o_ref[...] = x1_ref[...] + _bdot(o, wo_ref[...])


def _xattn_cache(q, x1, ck, cv, w_o, tq, seq_len):
    n, d = q.shape
    nmem = tq // seq_len
    row_spec = pl.BlockSpec((tq, d), lambda i: (i, 0))
    kv_spec = pl.BlockSpec((nmem, N_MEM, XA_HEADS, XA_HEAD_DIM), lambda i: (i, 0, 0, 0))
    return pl.pallas_call(
        functools.partial(_xattn_cache_kernel, seq_len=seq_len),
        grid=(n // tq,),
        in_specs=[row_spec, row_spec, kv_spec, kv_spec, _const_spec(w_o)],
        out_specs=row_spec,
        out_shape=jax.ShapeDtypeStruct((n, d), F32),
        compiler_params=_cparams("parallel"),
        name="xattn_cache",
    )(q, x1, ck, cv, w_o)


FF_SUB = 256
FF_GROUP = 4


def _ffn_kernel(*refs, tm, Lb, prompt, blocks_per_seq):
    if prompt:
        x_ref, g_ref, wu_ref, cw_ref, cb_ref, wd_ref, fg_ref, y_ref, s_ref, carry_sc = refs

        @pl.when(pl.program_id(0) % blocks_per_seq == 0)
        def _():
            carry_sc[...] = jnp.zeros_like(carry_sc)
    else:
        x_ref, hist_ref, g_ref, wu_ref, cw_ref, cb_ref, wd_ref, fg_ref, y_ref, s_ref = refs

    x = x_ref[...]
    hn = _rms(x, g_ref[...]).astype(BF16)
    nsub = D_FF // FF_SUB
    r8 = _iota2((8, 1), 0)
    tpos = _iota2((tm, 1), 0) & (Lb - 1)

    def conv(up, cols):
        cw = cw_ref[:, cols]
        cb = cb_ref[:, cols]
        if prompt:
            c8 = carry_sc[:, cols]
            carry_sc[:, cols] = up[tm - 8:tm]
            s_ref[0, :, cols] = up[tm - 2:tm]
            top = up[0:8]
            p1 = jnp.where(r8 >= 1, pltpu.roll(top, 1, 0), c8[7:8])
            p2 = jnp.where(r8 >= 2, pltpu.roll(top, 2, 0), jnp.where(r8 == 0, c8[6:7], c8[7:8]))
            head = (cw[0:1] * p2 + cw[1:2] * p1 + cw[2:3] * top) + cb
            body = (cw[0:1] * pltpu.roll(up, 2, 0) + cw[1:2] * pltpu.roll(up, 1, 0) + cw[2:3] * up) + cb
            return jnp.concatenate([head, body[8:]], axis=0)
        for b in range(tm // Lb):
            s_ref[b, :, cols] = up[b * Lb + Lb - 2:b * Lb + Lb]
        return (cw[0:1] * _shifted(up, 2, tpos, hist_ref[1, :, cols])
                + cw[1:2] * _shifted(up, 1, tpos, hist_ref[0, :, cols]) + cw[2:3] * up) + cb

    def up_proj(k):
        cg = slice(k * FF_SUB, (k + 1) * FF_SUB)
        cv = slice(D_FF + k * FF_SUB, D_FF + (k + 1) * FF_SUB)
        return (jnp.dot(hn, wu_ref[:, cg], preferred_element_type=F32),
                jnp.dot(hn, wu_ref[:, cv], preferred_element_type=F32))

    def gate_mul(k, ug, uv):
        cg = slice(k * FF_SUB, (k + 1) * FF_SUB)
        cv = slice(D_FF + k * FF_SUB, D_FF + (k + 1) * FF_SUB)
        gate = conv(ug, cg)
        return (gate * _sigmoid(gate) * conv(uv, cv)).astype(BF16)

    acc = x
    ups = {}
    acts = []
    for k in range(nsub + 1):
        if k < nsub:
            ups[k] = up_proj(k)
        if k >= 1:
            acts.append(gate_mul(k - 1, *ups.pop(k - 1)))
            if len(acts) == FF_GROUP or k == nsub:
                lo = (k - len(acts)) * FF_SUB
                acc = acc + jnp.dot(jnp.concatenate(acts, axis=1), wd_ref[lo:k * FF_SUB, :],
                                    preferred_element_type=F32)
                acts = []
    y_ref[...] = _rms(acc, fg_ref[...])


def _resident_spec(a):
    nd = a.ndim
    return pl.BlockSpec(a.shape, lambda *_: (0,) * nd, pipeline_mode=pl.Buffered(1))


def _ffn(x, hist, p, final_g, tm, Lb, prompt, seq_len):
    n, d = x.shape
    w_up = p["ffn_w_up"].astype(BF16)
    w_down = p["ffn_w_down"].astype(BF16)
    cw, cb = p["ffn_conv_w"], p["ffn_conv_b"].reshape(1, 2 * D_FF)
    row_spec = pl.BlockSpec((tm, d), lambda i: (i, 0))
    if prompt:
        bps = seq_len // tm
        nstate = n // tm
        s_spec = pl.BlockSpec((1, 2, 2 * D_FF), lambda i: (i, 0, 0))
        extra_in, extra_specs = [], []
        scratch = [pltpu.VMEM((8, 2 * D_FF), F32)]
    else:
        bps = 1
        nstate = n // seq_len
        s_spec = pl.BlockSpec((tm // Lb, 2, 2 * D_FF), lambda i: (i, 0, 0))
        extra_in = [hist]
        extra_specs = [pl.BlockSpec((2, tm, 2 * D_FF), lambda i: (0, i, 0))]
        scratch = []
    consts = [p["norm_ffn_w"].reshape(1, d), w_up, cw, cb, w_down, final_g.reshape(1, d)]
    y, st = pl.pallas_call(
        functools.partial(_ffn_kernel, tm=tm, Lb=Lb, prompt=prompt, blocks_per_seq=bps),
        grid=(n // tm,),
        in_specs=[row_spec] + extra_specs + [_resident_spec(a) for a in consts],
        out_specs=[row_spec, s_spec],
        out_shape=[jax.ShapeDtypeStruct((n, d), F32), jax.ShapeDtypeStruct((nstate, 2, 2 * D_FF), F32)],
        scratch_shapes=scratch,
        compiler_params=_cparams("arbitrary"),
        name="convffn",
    )(x, *extra_in, *consts)
    if prompt:
        st = st[bps - 1::bps]
    return y, st


def _pair_blocks(sbd):
    b = sbd.shape[0]
    s0 = sbd[:, :, 0:64, 0:64]
    s1 = sbd[:, :, 64:128, 64:128]
    return jnp.stack([s0, s1], axis=2).reshape(b, 8, 64, 64)


def _hist_rows(state, seq_len, k):
    b, w, c = state.shape
    cols = [state[:, w + t - k] if t < k else jnp.zeros((b, c), state.dtype) for t in range(seq_len)]
    return jnp.stack(cols, axis=1).reshape(b * seq_len, c)


def kernel(x_prompt, x_sample, mem_prompt, state_ssm_conv, state_ssm, state_shift, state_wkv, state_ffn_conv, cache_mem_k, cache_mem_v, norm_mix_w, w_in, ssd_conv_w, ssd_conv_b, ssd_dt_bias, ssd_a_log, ssd_d, ssd_norm_w, rwkv_mu, rwkv_w0, rwkv_w2, rwkv_a0, rwkv_a2, rwkv_g2, rwkv_k_k, rwkv_k_a, rwkv_r_k, rwkv_ln_w, rwkv_ln_b, w_out, norm_xa_w, mem_norm_w, xa_w_q, xa_w_k, xa_w_v, xa_w_o, norm_ffn_w, ffn_w_up, ffn_conv_w, ffn_conv_b, ffn_w_down, final_norm_w):
    depth = w_in.shape[0]
    assert depth == 1, "final rmsnorm is fused into the (single) layer's ConvFFN kernel"
    bp, lp, d = x_prompt.shape
    bs, ls, _ = x_sample.shape
    i = 0
    p = dict(ssd_conv_w=ssd_conv_w[i], ssd_conv_b=ssd_conv_b[i], ssd_dt_bias=ssd_dt_bias[i],
             ssd_a_log=ssd_a_log[i], ssd_d=ssd_d[i], ssd_norm_w=ssd_norm_w[i], rwkv_mu=rwkv_mu[i],
             rwkv_w0=rwkv_w0[i], rwkv_w2=rwkv_w2[i], rwkv_a0=rwkv_a0[i], rwkv_a2=rwkv_a2[i],
             rwkv_g2=rwkv_g2[i], rwkv_k_k=rwkv_k_k[i], rwkv_k_a=rwkv_k_a[i],
             rwkv_r_k=rwkv_r_k[i].reshape(-1), rwkv_ln_w=rwkv_ln_w[i], rwkv_ln_b=rwkv_ln_b[i],
             norm_ffn_w=norm_ffn_w[i], ffn_w_up=ffn_w_up[i], ffn_conv_w=ffn_conv_w[i],
             ffn_conv_b=ffn_conv_b[i], ffn_w_down=ffn_w_down[i])

    w_in_p = jnp.concatenate([w_in[i][:, :SSD_PROJ], jnp.zeros((d, SEG - SSD_PROJ), F32),
                              w_in[i][:, SSD_PROJ:]], axis=1).astype(BF16)
    w_out_b = w_out[i].astype(BF16)
    w_q_b = (xa_w_q[i]).astype(BF16)
    w_o_b = xa_w_o[i].astype(BF16)
    w_kv_b = jnp.concatenate([xa_w_k[i], xa_w_v[i]], axis=1).astype(BF16)

    xp = x_prompt.reshape(bp * lp, d)
    xs = x_sample.reshape(bs * ls, d)

    mk, mv = _mem_kv(mem_prompt.reshape(bp * N_MEM, d), mem_norm_w[i], w_kv_b, 512)
    mk = mk.reshape(bp, N_MEM, d)
    mv = mv.reshape(bp, N_MEM, d)
    u_p = _norm_proj(xp, norm_mix_w[i], w_in_p, 1024, 512)
    y_ssd_p, ssm_p = _ssd_prompt(u_p, p, bp, lp)
    y_rw_p, wkv_bd_p = _wkv_prompt(u_p, p, bp, lp)
    x1_p, q_p = _outproj(xp, y_ssd_p, y_rw_p, w_out_b, norm_xa_w[i], w_q_b, 512)
    x2_p = _xattn(q_p, x1_p, mk, mv, w_o_b, 512, lp)
    y_p, ffn_conv_p = _ffn(x2_p, None, p, final_norm_w, 1024, lp, True, lp)
    u3 = u_p.reshape(bp, lp, 2 * SEG)
    ssm_conv_p = u3[:, lp - 3:, SSD_WIDTH:SSD_WIDTH + SSD_XBC]
    shift_p = u3[:, lp - 1, SEG:]

    u_s = _norm_proj(xs, norm_mix_w[i], w_in_p, 512, 512)
    u3s = u_s.reshape(bs, ls, 2 * SEG)
    xbc_s = u3s[:, :, SSD_WIDTH:SSD_WIDTH + SSD_XBC]
    conv_state = state_ssm_conv[i]
    hist_conv = jnp.stack([_hist_rows(conv_state, ls, k) for k in (1, 2, 3)], axis=0)
    y_ssd_s, ssm_s = _ssd_sample(u_s, hist_conv, state_ssm[i].reshape(bs, 4, LANES, LANES), p, ls)
    hist_shift = _hist_rows(state_shift[i][:, None, :], ls, 1)
    y_rw_s, wkv_s = _wkv_sample(u_s, hist_shift, state_wkv[i], p, ls)
    x1_s, q_s = _outproj(xs, y_ssd_s, y_rw_s, w_out_b, norm_xa_w[i], w_q_b, 512)
    x2_s = _xattn_cache(q_s, x1_s, cache_mem_k[i], cache_mem_v[i], w_o_b, 8, ls)
    fstate = state_ffn_conv[i]
    hist_ffn = jnp.stack([_hist_rows(fstate, ls, k) for k in (1, 2)], axis=0)
    y_s, ffn_conv_s = _ffn(x2_s, hist_ffn, p, final_norm_w, 256, ls, False, ls)
    ssm_conv_s = jnp.concatenate([conv_state, xbc_s], axis=1)[:, -3:]
    shift_s = u3s[:, ls - 1, SEG:]

    e = lambda a: a[None]
    return (y_p.reshape(bp, lp, d), y_s.reshape(bs, ls, d),
            e(ssm_conv_p), e(ssm_conv_s),
            e(ssm_p.reshape(bp, SSD_HEADS, 64, SSD_STATE)), e(ssm_s.reshape(bs, SSD_HEADS, 64, SSD_STATE)),
            e(shift_p), e(shift_s),
            e(_pair_blocks(wkv_bd_p)), e(wkv_s),
            e(ffn_conv_p), e(ffn_conv_s),
            e(mk.reshape(bp, N_MEM, XA_HEADS, XA_HEAD_DIM)), e(mv.reshape(bp, N_MEM, XA_HEADS, XA_HEAD_DIM)))
```

```python
import functools
import math

import jax
import jax.numpy as jnp
from jax import lax
from jax.experimental import pallas as pl
from jax.experimental.pallas import tpu as pltpu

F32 = jnp.float32
BF16 = jnp.bfloat16

D_MODEL = 1024
N_MEM = 256
XA_HEADS = 4
XA_HEAD_DIM = D_MODEL // XA_HEADS
SSD_WIDTH = 512
SSD_HEADS = 8
SSD_STATE = 128
SSD_XBC = 1024
SSD_PROJ = SSD_WIDTH + SSD_XBC + SSD_HEADS
RWKV_WIDTH = 512
RWKV_PROJ = 1792
D_FF = 2816
EPS = 1e-6
GN_EPS = 64e-5

LANES = 128
SEG = 1792
DT_OFF = SSD_WIDTH + SSD_XBC
VMEM_LIMIT_BYTES = 56 * 1024 * 1024


def _cparams(*sem):
    return pltpu.CompilerParams(dimension_semantics=sem, vmem_limit_bytes=VMEM_LIMIT_BYTES)


_NN = (((1,), (0,)), ((), ()))
_NT = (((1,), (1,)), ((), ()))
_TN = (((0,), (0,)), ((), ()))


def _bdot(a, b, dims=_NN):
    return lax.dot_general(a.astype(BF16), b.astype(BF16), dims, preferred_element_type=F32)


def _split2(x):
    hi = x.astype(BF16)
    lo = (x - hi.astype(F32)).astype(BF16)
    return hi, lo


def _maskdot(m_bf16, x):
    hi = x.astype(BF16)
    r1 = x - hi.astype(F32)
    mid = r1.astype(BF16)
    lo = (r1 - mid.astype(F32)).astype(BF16)
    return jnp.dot(jnp.concatenate([m_bf16, m_bf16, m_bf16], axis=1),
                   jnp.concatenate([hi, mid, lo], axis=0), preferred_element_type=F32)


def _xmask(x, m2_bf16):
    hi, lo = _split2(x)
    return jnp.dot(jnp.concatenate([hi, lo], axis=1), m2_bf16, preferred_element_type=F32)


def _sigmoid(x):
    return 0.5 + 0.5 * jnp.tanh(0.5 * x)


def _softplus(x):
    return jnp.maximum(x, 0.0) + jnp.log1p(jnp.exp(-jnp.abs(x)))


def _rms(x, g):
    return x * lax.rsqrt(jnp.mean(x * x, axis=-1, keepdims=True) + EPS) * g


def _iota2(shape, dim):
    return lax.broadcasted_iota(jnp.int32, shape, dim)


def _shifted(x, k, tpos, hist):
    return jnp.where(tpos >= k, pltpu.roll(x, k, 0), hist)


def _norm_proj_kernel(x_ref, g_ref, w_ref, o_ref, hn_sc):
    @pl.when(pl.program_id(1) == 0)
    def _():
        hn_sc[...] = _rms(x_ref[...], g_ref[...]).astype(BF16)

    o_ref[...] = jnp.dot(hn_sc[...], w_ref[...], preferred_element_type=F32)


def _norm_proj(x, g, w_bf16, tm, tn):
    n, d = x.shape
    f = w_bf16.shape[1]
    return pl.pallas_call(
        _norm_proj_kernel,
        grid=(n // tm, f // tn),
        in_specs=[pl.BlockSpec((tm, d), lambda i, j: (i, 0)),
                  pl.BlockSpec((1, d), lambda i, j: (0, 0)),
                  pl.BlockSpec((d, tn), lambda i, j: (0, j))],
        out_specs=pl.BlockSpec((tm, tn), lambda i, j: (i, j)),
        out_shape=jax.ShapeDtypeStruct((n, f), F32),
        scratch_shapes=[pltpu.VMEM((tm, d), BF16)],
        compiler_params=_cparams("parallel", "arbitrary"),
        name="norm_proj",
    )(x, g.reshape(1, d), w_bf16)


def _mem_kv_kernel(x_ref, g_ref, w_ref, k_ref, v_ref):
    hn = _rms(x_ref[...], g_ref[...]).astype(BF16)
    d = k_ref.shape[1]
    k_ref[...] = jnp.dot(hn, w_ref[:, 0:d], preferred_element_type=F32)
    v_ref[...] = jnp.dot(hn, w_ref[:, d:2 * d], preferred_element_type=F32)


def _mem_kv(x, g, w_kv_bf16, tm):
    n, d = x.shape
    row_spec = pl.BlockSpec((tm, d), lambda i: (i, 0))
    return pl.pallas_call(
        _mem_kv_kernel,
        grid=(n // tm,),
        in_specs=[row_spec, pl.BlockSpec((1, d), lambda i: (0, 0)), _const_spec(w_kv_bf16)],
        out_specs=[row_spec, row_spec],
        out_shape=[jax.ShapeDtypeStruct((n, d), F32)] * 2,
        compiler_params=_cparams("parallel"),
        name="mem_kv",
    )(x, g.reshape(1, d), w_kv_bf16)


def _ssd_kernel(*refs, Q, Lb, prompt):
    ns = Q // Lb
    lb = int(math.log2(Lb))
    if prompt:
        (u_ref, cw_ref, cb_ref, dtb_ref, an_ref, dsk_ref, nw_ref,
         y_ref, hout_ref, carry_sc, h_sc) = refs
        c = pl.program_id(1)

        @pl.when(c == 0)
        def _():
            carry_sc[...] = jnp.zeros_like(carry_sc)
            h_sc[...] = jnp.zeros_like(h_sc)
    else:
        (u_ref, hist_ref, hin_ref, cw_ref, cb_ref, dtb_ref, an_ref, dsk_ref, nw_ref,
         y_ref, hout_ref) = refs

    u = u_ref[...]
    z = u[:, 0:SSD_WIDTH]
    x = u[:, SSD_WIDTH:SSD_WIDTH + SSD_XBC]
    dtr = u[:, DT_OFF:DT_OFF + LANES]

    rows = _iota2((Q, 1), 0)
    tpos = rows & (Lb - 1)
    if prompt:
        c8 = carry_sc[...]
        h1 = c8[7:8]
        h2 = jnp.where(rows == 0, c8[6:7], c8[7:8])
        h3 = jnp.where(rows == 0, c8[5:6], jnp.where(rows == 1, c8[6:7], c8[7:8]))
        carry_sc[...] = x[Q - 8:Q]
    else:
        h1, h2, h3 = hist_ref[0], hist_ref[1], hist_ref[2]
    cw = cw_ref[...]
    xc = (cw[0:1] * _shifted(x, 3, tpos, h3) + cw[1:2] * _shifted(x, 2, tpos, h2)
          + cw[2:3] * _shifted(x, 1, tpos, h1) + cw[3:4] * x) + cb_ref[...]
    xc = xc * _sigmoid(xc)
    xs = xc[:, 0:SSD_WIDTH]
    bm = xc[:, SSD_WIDTH:SSD_WIDTH + 2 * SSD_STATE]
    cm = xc[:, SSD_WIDTH + 2 * SSD_STATE:]

    dt = _softplus(dtr + dtb_ref[...])
    da = dt * an_ref[...]

    ri = _iota2((Q, Q), 0)
    ci = _iota2((Q, Q), 1)
    same = (ri >> lb) == (ci >> lb)
    tril = same & (ci <= ri)
    sel = ci == (((ri >> lb) << lb) + (Lb - 1))
    acs = _maskdot(tril.astype(BF16), da)
    acs_t = acs.T
    acs_last = _maskdot(sel.astype(BF16), acs)
    dec_end = jnp.exp(acs_last - acs)
    eacs = jnp.exp(acs)
    seqid = rows >> lb

    lane = _iota2((Q, LANES), 1)
    lo_half = lane < 64
    prow = _iota2((LANES, LANES), 0)
    dsk = dsk_ref[...]

    ys = []
    for q in range(4):
        g = q // 2
        h0, h1i = 2 * q, 2 * q + 1
        if q % 2 == 0:
            cg = cm[:, g * SSD_STATE:(g + 1) * SSD_STATE]
            bg = bm[:, g * SSD_STATE:(g + 1) * SSD_STATE]
            cb_g = _bdot(cg, bg, _NT)
            if ns == 1:
                cexp, bexp = cg.astype(BF16), bg.astype(BF16)
            else:
                cexp = jnp.concatenate([jnp.where(seqid == b, cg, 0.0).astype(BF16) for b in range(ns)], axis=1)
                bexp = jnp.concatenate([jnp.where(seqid == b, bg, 0.0).astype(BF16) for b in range(ns)], axis=1)
        m0 = jnp.where(tril, cb_g * jnp.exp(acs[:, h0:h0 + 1] - acs_t[h0:h0 + 1, :]), 0.0)
        m1 = jnp.where(tril, cb_g * jnp.exp(acs[:, h1i:h1i + 1] - acs_t[h1i:h1i + 1, :]), 0.0)
        xp = xs[:, q * LANES:(q + 1) * LANES]
        xdt = xp * jnp.where(lo_half, dt[:, h0:h0 + 1], dt[:, h1i:h1i + 1])
        xdt0 = jnp.where(lo_half, xdt, 0.0)
        xdt1 = jnp.where(lo_half, 0.0, xdt)
        ydiag = _bdot(jnp.concatenate([m0, m1], axis=1), jnp.concatenate([xdt0, xdt1], axis=0))
        if prompt:
            hst = h_sc[q]
        else:
            hst = jnp.concatenate([hin_ref[b, q] for b in range(ns)], axis=1)
        ecs = jnp.where(lo_half, eacs[:, h0:h0 + 1], eacs[:, h1i:h1i + 1])
        yoff = _bdot(cexp, hst, _NT) * ecs
        xd = xdt * jnp.where(lo_half, dec_end[:, h0:h0 + 1], dec_end[:, h1i:h1i + 1])
        incr = _bdot(xd, bexp, _TN)
        scales = []
        for b in range(ns):
            r = b * Lb + Lb - 1
            e = eacs[r:r + 1, :]
            scales.append(jnp.where(prow < 64, e[:, h0:h0 + 1], e[:, h1i:h1i + 1]))
        scale = scales[0] if ns == 1 else jnp.concatenate(scales, axis=1)
        hnew = hst * scale + incr
        if prompt:
            h_sc[q] = hnew
            hout_ref[0, q] = hnew
        else:
            for b in range(ns):
                hout_ref[b, q] = hnew[:, b * LANES:(b + 1) * LANES]
        ys.append(ydiag + yoff + dsk[:, q * LANES:(q + 1) * LANES] * xp)

    y = jnp.concatenate(ys, axis=1)
    y = y * (z * _sigmoid(z))
    half = SSD_WIDTH // 2
    outs = []
    for g in range(2):
        yg = y[:, g * half:(g + 1) * half]
        outs.append(yg * lax.rsqrt(jnp.mean(yg * yg, axis=-1, keepdims=True) + EPS))
    y_ref[...] = (jnp.concatenate(outs, axis=1) * nw_ref[...]).astype(y_ref.dtype)


def _ssd_params(p):
    an = jnp.zeros((1, LANES), F32).at[0, :SSD_HEADS].set(-jnp.exp(p["ssd_a_log"]))
    dtb = jnp.zeros((1, LANES), F32).at[0, :SSD_HEADS].set(p["ssd_dt_bias"])
    dsk = jnp.repeat(p["ssd_d"], SSD_WIDTH // SSD_HEADS).reshape(1, SSD_WIDTH)
    return [p["ssd_conv_w"], p["ssd_conv_b"].reshape(1, SSD_XBC), dtb, an, dsk,
            p["ssd_norm_w"].reshape(1, SSD_WIDTH)]


def _const_spec(a):
    nd = a.ndim
    return pl.BlockSpec(a.shape, lambda *_: (0,) * nd)


def _ssd_prompt(u, p, batch, seq, Q=128):
    n = u.shape[0]
    nc = seq // Q
    params = _ssd_params(p)
    y, hout = pl.pallas_call(
        functools.partial(_ssd_kernel, Q=Q, Lb=Q, prompt=True),
        grid=(batch, nc),
        in_specs=[pl.BlockSpec((Q, SEG), lambda b, c: (b * nc + c, 0))] + [_const_spec(a) for a in params],
        out_specs=[pl.BlockSpec((Q, SSD_WIDTH), lambda b, c: (b * nc + c, 0)),
                   pl.BlockSpec((1, 4, LANES, LANES), lambda b, c: (b, 0, 0, 0))],
        out_shape=[jax.ShapeDtypeStruct((n, SSD_WIDTH), BF16),
                   jax.ShapeDtypeStruct((batch, 4, LANES, LANES), F32)],
        scratch_shapes=[pltpu.VMEM((8, SSD_XBC), F32), pltpu.VMEM((4, LANES, LANES), F32)],
        compiler_params=_cparams("parallel", "arbitrary"),
        name="ssd_prompt",
    )(u, *params)
    return y, hout


def _ssd_sample(u, hist, hin, p, Lb, Q=64):
    n = u.shape[0]
    ns = Q // Lb
    params = _ssd_params(p)
    y, hout = pl.pallas_call(
        functools.partial(_ssd_kernel, Q=Q, Lb=Lb, prompt=False),
        grid=(n // Q,),
        in_specs=[pl.BlockSpec((Q, SEG), lambda i: (i, 0)),
                  pl.BlockSpec((3, Q, SSD_XBC), lambda i: (0, i, 0)),
                  pl.BlockSpec((ns, 4, LANES, LANES), lambda i: (i, 0, 0, 0))]
                 + [_const_spec(a) for a in params],
        out_specs=[pl.BlockSpec((Q, SSD_WIDTH), lambda i: (i, 0)),
                   pl.BlockSpec((ns, 4, LANES, LANES), lambda i: (i, 0, 0, 0))],
        out_shape=[jax.ShapeDtypeStruct((n, SSD_WIDTH), BF16),
                   jax.ShapeDtypeStruct(hin.shape, F32)],
        compiler_params=_cparams("parallel"),
        name="ssd_sample",
    )(u, hist, hin, *params)
    return y, hout


def _dot3s(a, b, dims=_NN):
    ka = dims[0][0][0]
    kb = dims[0][1][0]
    lhs = jnp.concatenate([a[0], a[0], a[1]], axis=ka)
    rhs = jnp.concatenate([b[0], b[1], b[0]], axis=kb)
    return lax.dot_general(lhs, rhs, dims, preferred_element_type=F32)


def _cat2(parts, axis):
    return (jnp.concatenate([p[0] for p in parts], axis=axis), jnp.concatenate([p[1] for p in parts], axis=axis))


def _tri_inverse(mats, lb):
    n = mats[0].shape[0]
    ri = _iota2((n, n), 0)
    ci = _iota2((n, n), 1)
    off1 = ((ri >> 1) == (ci >> 1)) & ((ri & 1) == 1) & ((ci & 1) == 0)
    eye = jnp.where(ri == ci, 1.0, 0.0)
    ts = [eye + jnp.where(off1, a, 0.0) for a in mats]
    for lvl in range(1, lb):
        m = 1 << lvl
        off = ((ri >> (lvl + 1)) == (ci >> (lvl + 1))) & ((ri & (2 * m - 1)) >= m) & ((ci & (2 * m - 1)) < m)
        tsb = [t.astype(BF16) for t in ts]
        ws = [jnp.dot(jnp.where(off, a, 0.0).astype(BF16), tb, preferred_element_type=F32)
              for a, tb in zip(mats, tsb)]
        ts = [t + jnp.dot(tb, w.astype(BF16), preferred_element_type=F32) for t, tb, w in zip(ts, tsb, ws)]
    return [t.astype(BF16) for t in ts]


def _refined_solve(tinvs, mats, rhss):
    d = functools.partial(jnp.dot, preferred_element_type=F32)
    n = range(len(mats))
    rs = [_split2(r) for r in rhss]
    u0 = [d(jnp.concatenate([tinvs[i], tinvs[i]], axis=1), jnp.concatenate(rs[i], axis=0)) for i in n]
    au = [_dot3s(_split2(mats[i]), _split2(u0[i])) for i in n]
    res = [((rhss[i] - u0[i]) + au[i]).astype(BF16) for i in n]
    return [u0[i] + d(tinvs[i], res[i]) for i in n]


def _wkv_kernel(*refs, C, Lb, prompt, G):
    nb = C // Lb
    lb = int(math.log2(Lb))
    R = 2 * C
    T = G * C
    if prompt:
        (u_ref, mu_ref, w0_ref, w2_ref, a0_ref, a2_ref, g2_ref, kk_ref, ka_ref, rk_ref, lnw_ref, lnb_ref,
         y_ref, sout_ref, carry_sc, s_sc) = refs
        c = pl.program_id(1)

        @pl.when(c == 0)
        def _():
            carry_sc[...] = jnp.zeros_like(carry_sc)
            s_sc[...] = jnp.zeros_like(s_sc)
    else:
        (u_ref, hist_ref, sin_ref, mu_ref, w0_ref, w2_ref, a0_ref, a2_ref, g2_ref, kk_ref, ka_ref, rk_ref,
         lnw_ref, lnb_ref, y_ref, sout_ref) = refs

    rows = _iota2((T, 1), 0)
    tpos = rows & (Lb - 1)
    if prompt:
        u = u_ref[...].reshape(T, SEG)
        hist = jnp.concatenate([jnp.broadcast_to(carry_sc[g, 7:8], (C, SEG)) for g in range(G)], axis=0)
        for g in range(G):
            carry_sc[g] = u[g * C + C - 8:(g + 1) * C]
    else:
        u = u_ref[...]
        hist = hist_ref[...]
    um = u + (_shifted(u, 1, tpos, hist) - u) * mu_ref[...]

    W = RWKV_WIDTH
    r = um[:, 0:W]
    k = um[:, W:2 * W]
    v = um[:, 2 * W:3 * W]
    t12 = um[:, 3 * W:3 * W + LANES]
    lg = um[:, 3 * W + LANES:3 * W + 2 * LANES]

    wl = w0_ref[...] + _bdot(jnp.tanh(t12), w2_ref[...])
    logw = -jnp.exp(-_softplus(-wl) - 0.5)
    a = _sigmoid(a0_ref[...] + _bdot(t12, a2_ref[...]))
    out_gate = _bdot(_sigmoid(lg), g2_ref[...])

    hi = _iota2((2 * W, W), 0) & (W - 1)
    hj = _iota2((2 * W, W), 1)
    headsum = ((hi >> 6) == (hj >> 6)).astype(BF16)

    kk = k * kk_ref[...]
    kk = kk / jnp.maximum(jnp.sqrt(_xmask(kk * kk, headsum)), 1e-12)
    kmod = k * (1.0 + (a - 1.0) * ka_ref[...])
    beta = kk * a

    ri = _iota2((T, T), 0)
    ci = _iota2((T, T), 1)
    same_c = (ri >> lb) == (ci >> lb)
    lc = _maskdot((same_c & (ci <= ri)).astype(BF16), logw)
    lc_last = _maskdot((ci == (((ri >> lb) << lb) + (Lb - 1))).astype(BF16), lc)
    e_neg = jnp.exp(-lc)
    e_end = jnp.exp(lc_last - lc)
    at = -kk * jnp.exp(lc - logw)
    rt = r * jnp.exp(lc)
    bt = beta * e_neg
    kt = kmod * e_neg
    bh = beta * e_end
    kh = kmod * e_end
    p_last = jnp.exp(lc_last)

    si = _iota2((R, R), 0)
    sj = _iota2((R, R), 1)
    same_s = (si >> lb) == (sj >> lb)
    strict = same_s & (sj < si)
    incl = same_s & (sj <= si)
    incl2 = jnp.concatenate([incl, incl], axis=1)
    lane_row = _iota2((1, LANES), 1)
    m_lo = jnp.where(lane_row < 64, 1.0, 0.0).astype(BF16)
    m_hi = jnp.where(lane_row < 64, 0.0, 1.0).astype(BF16)
    if nb > 1:
        seq_s = (_iota2((R, LANES), 0) & (C - 1)) >> lb
        seq_masks = [jnp.where(seq_s == b, 1.0, 0.0).astype(BF16) for b in range(nb)]

    def stack(xp):
        return jnp.concatenate([xp * m_lo, xp * m_hi], axis=0)

    def expand(xs_):
        if nb == 1:
            return xs_
        return jnp.concatenate([xs_ * seq_masks[b] for b in range(nb)], axis=1)

    probs = [(g, p) for g in range(G) for p in range(4)]
    NP = range(len(probs))

    def tile(x, i):
        g, p = probs[i]
        return x[g * C:(g + 1) * C, p * LANES:(p + 1) * LANES]

    def stacked(pair, i):
        return (stack(tile(pair[0], i)), stack(tile(pair[1], i)))

    at2, v2, bt2, kt2, bh2, kh2 = (_split2(x) for x in (at, v, bt, kt, bh, kh))
    rt_b = rt.astype(BF16)
    a_s = [stacked(at2, i) for i in NP]
    r_s = [stack(tile(rt_b, i)) for i in NP]
    v_s = [stacked(v2, i) for i in NP]
    bk_s = [_cat2([stacked(bt2, i), stacked(kt2, i)], 0) for i in NP]
    gm_a = [_dot3s(a_s[i], bk_s[i], _NT) for i in NP]
    gm_r = [lax.dot_general(r_s[i], bk_s[i][0], _NT, preferred_element_type=F32) for i in NP]
    a_ab = [jnp.where(strict, gm_a[i][:, 0:R], 0.0) for i in NP]
    a_ak = [_split2(jnp.where(strict, gm_a[i][:, R:2 * R], 0.0)) for i in NP]
    a_r = [jnp.where(incl2, gm_r[i], 0.0).astype(BF16) for i in NP]
    akv = [_dot3s(a_ak[i], v_s[i]) for i in NP]
    tinv = _tri_inverse(a_ab, lb)

    if prompt:
        sst = [s_sc[i] for i in NP]
    else:
        z64 = jnp.zeros((64, 64), F32)

        def pair_blockdiag(b, p):
            top = jnp.concatenate([sin_ref[b, 2 * p], z64], axis=1)
            bot = jnp.concatenate([z64, sin_ref[b, 2 * p + 1]], axis=1)
            return jnp.concatenate([top, bot], axis=0)

        sst = [jnp.concatenate([pair_blockdiag(g * nb + b, p) for b in range(nb)], axis=1) for g, p in probs]
    ss = [_split2(s) for s in sst]
    ar0_a = [_dot3s((expand(a_s[i][0]), expand(a_s[i][1])), ss[i], _NT) for i in NP]
    ar0_r = [lax.dot_general(expand(r_s[i]), ss[i][0], _NT, preferred_element_type=F32) for i in NP]
    us = [_split2(x) for x in _refined_solve(tinv, a_ab, [ar0_a[i] + akv[i] for i in NP])]
    uv = [_cat2([us[i], v_s[i]], 0) for i in NP]
    yst = [ar0_r[i] + jnp.dot(a_r[i], uv[i][0], preferred_element_type=F32) for i in NP]
    ys = [yst[i][0:C] + yst[i][C:R] for i in NP]
    for i in NP:
        g, p = probs[i]
        pl_lanes = tile(p_last, i)
        if nb == 1:
            plast = pl_lanes[0:1]
        else:
            plast = jnp.concatenate([pl_lanes[b * Lb:b * Lb + 1] for b in range(nb)], axis=1)
        bh_s, kh_s = stacked(bh2, i), stacked(kh2, i)
        bkh = (jnp.concatenate([expand(bh_s[0]), expand(kh_s[0])], axis=0),
               jnp.concatenate([expand(bh_s[1]), expand(kh_s[1])], axis=0))
        snew = sst[i] * plast + _dot3s(uv[i], bkh, _TN)
        if prompt:
            s_sc[i] = snew
            sout_ref[g, p] = snew
        else:
            for b in range(nb):
                sout_ref[g * nb + b, 2 * p] = snew[0:64, b * LANES:b * LANES + 64]
                sout_ref[g * nb + b, 2 * p + 1] = snew[64:128, b * LANES + 64:(b + 1) * LANES]

    y = jnp.concatenate([jnp.concatenate(ys[4 * g:4 * g + 4], axis=1) for g in range(G)], axis=0)
    inv_d = 1.0 / 64.0
    mean = _xmask(y, headsum) * inv_d
    yc = y - mean
    var = _xmask(yc * yc, headsum) * inv_d
    yn = yc * lax.rsqrt(var + GN_EPS) * lnw_ref[...] + lnb_ref[...]
    yn = yn + _xmask(r * kmod * rk_ref[...], headsum) * v
    y_ref[...] = (yn * out_gate).astype(y_ref.dtype).reshape(y_ref.shape)


def _wkv_params(p):
    z64 = jnp.zeros((64, RWKV_WIDTH), F32)
    w2 = jnp.concatenate([p["rwkv_w2"], z64], axis=0).astype(BF16)
    a2 = jnp.concatenate([z64, p["rwkv_a2"]], axis=0).astype(BF16)
    row = lambda a: a.reshape(1, -1)
    return [row(p["rwkv_mu"]), row(p["rwkv_w0"]), w2, row(p["rwkv_a0"]), a2, p["rwkv_g2"].astype(BF16),
            row(p["rwkv_k_k"]), row(p["rwkv_k_a"]), row(p["rwkv_r_k"]), row(p["rwkv_ln_w"]), row(p["rwkv_ln_b"])]


def _wkv_prompt(u, p, batch, seq, C=64, G=4):
    nc = seq // C
    params = _wkv_params(p)
    y, sout = pl.pallas_call(
        functools.partial(_wkv_kernel, C=C, Lb=C, prompt=True, G=G),
        grid=(batch // G, nc),
        in_specs=[pl.BlockSpec((G, C, SEG), lambda b, c: (b, c, 1))] + [_const_spec(a) for a in params],
        out_specs=[pl.BlockSpec((G, C, RWKV_WIDTH), lambda b, c: (b, c, 0)),
                   pl.BlockSpec((G, 4, LANES, LANES), lambda b, c: (b, 0, 0, 0))],
        out_shape=[jax.ShapeDtypeStruct((batch, seq, RWKV_WIDTH), BF16),
                   jax.ShapeDtypeStruct((batch, 4, LANES, LANES), F32)],
        scratch_shapes=[pltpu.VMEM((G, 8, SEG), F32), pltpu.VMEM((4 * G, LANES, LANES), F32)],
        compiler_params=_cparams("parallel", "arbitrary"),
        name="wkv_prompt",
    )(u.reshape(batch, seq, 2 * SEG), *params)
    return y.reshape(batch * seq, RWKV_WIDTH), sout


def _wkv_sample(u, hist, sin, p, Lb, C=64, G=2):
    n = u.shape[0]
    nb = C // Lb
    T = G * C
    params = _wkv_params(p)
    return pl.pallas_call(
        functools.partial(_wkv_kernel, C=C, Lb=Lb, prompt=False, G=G),
        grid=(n // T,),
        in_specs=[pl.BlockSpec((T, SEG), lambda i: (i, 1)),
                  pl.BlockSpec((T, SEG), lambda i: (i, 0)),
                  pl.BlockSpec((G * nb, 8, 64, 64), lambda i: (i, 0, 0, 0))]
                 + [_const_spec(a) for a in params],
        out_specs=[pl.BlockSpec((T, RWKV_WIDTH), lambda i: (i, 0)),
                   pl.BlockSpec((G * nb, 8, 64, 64), lambda i: (i, 0, 0, 0))],
        out_shape=[jax.ShapeDtypeStruct((n, RWKV_WIDTH), BF16),
                   jax.ShapeDtypeStruct(sin.shape, F32)],
        compiler_params=_cparams("parallel"),
        name="wkv_sample",
    )(u, hist, sin, *params)


def _outproj_kernel(x_ref, ys_ref, yr_ref, wo_ref, g_ref, wq_ref, x1_ref, q_ref):
    wo = wo_ref[...]
    x1 = x_ref[...] + (_bdot(ys_ref[...], wo[0:SSD_WIDTH]) + _bdot(yr_ref[...], wo[SSD_WIDTH:]))
    x1_ref[...] = x1
    q_ref[...] = _bdot(_rms(x1, g_ref[...]), wq_ref[...])


def _outproj(x, y_ssd, y_rw, w_out, g, w_q, tm):
    n, d = x.shape
    return pl.pallas_call(
        _outproj_kernel,
        grid=(n // tm,),
        in_specs=[pl.BlockSpec((tm, d), lambda i: (i, 0)),
                  pl.BlockSpec((tm, SSD_WIDTH), lambda i: (i, 0)),
                  pl.BlockSpec((tm, RWKV_WIDTH), lambda i: (i, 0)),
                  _const_spec(w_out), pl.BlockSpec((1, d), lambda i: (0, 0)), _const_spec(w_q)],
        out_specs=[pl.BlockSpec((tm, d), lambda i: (i, 0)), pl.BlockSpec((tm, d), lambda i: (i, 0))],
        out_shape=[jax.ShapeDtypeStruct((n, d), F32), jax.ShapeDtypeStruct((n, d), F32)],
        compiler_params=_cparams("parallel"),
        name="outproj_q",
    )(x, y_ssd, y_rw, w_out, g.reshape(1, d), w_q)


def _xattn_kernel(q_ref, x1_ref, k_ref, v_ref, wo_ref, o_ref):
    q = q_ref[...]
    scale = XA_HEAD_DIM ** -0.5
    outs = []
    for h in range(XA_HEADS):
        sl = slice(h * XA_HEAD_DIM, (h + 1) * XA_HEAD_DIM)
        s = _bdot(q[:, sl], k_ref[0, :, sl], _NT) * scale
        s = s - jnp.max(s, axis=-1, keepdims=True)
        e = jnp.exp(s)
        pr = e / jnp.sum(e, axis=-1, keepdims=True)
        outs.append(_bdot(pr, v_ref[0, :, sl]))
    o_ref[...] = x1_ref[...] + _bdot(jnp.concatenate(outs, axis=1), wo_ref[...])


def _xattn(q, x1, mk, mv, w_o, tq, seq_len):
    n, d = q.shape
    per = seq_len // tq
    row_spec = pl.BlockSpec((tq, d), lambda i: (i, 0))
    kv_spec = pl.BlockSpec((1, N_MEM, d), lambda i: (i // per, 0, 0))
    return pl.pallas_call(
        _xattn_kernel,
        grid=(n // tq,),
        in_specs=[row_spec, row_spec, kv_spec, kv_spec, _const_spec(w_o)],
        out_specs=row_spec,
        out_shape=jax.ShapeDtypeStruct((n, d), F32),
        compiler_params=_cparams("parallel"),
        name="xattn",
    )(q, x1, mk, mv, w_o)


def _xattn_cache_kernel(q_ref, x1_ref, k_ref, v_ref, wo_ref, o_ref, *, seq_len):
    tq = q_ref.shape[0]
    nmem = tq // seq_len
    rows = N_MEM * XA_HEADS
    cols = XA_HEADS * tq
    q = q_ref[...] * (XA_HEAD_DIM ** -0.5)
    qblk = jnp.concatenate([q[:, h * XA_HEAD_DIM:(h + 1) * XA_HEAD_DIM] for h in range(XA_HEADS)], axis=0)
    key_head = _iota2((cols, rows), 1) & (XA_HEADS - 1)
    qrow = _iota2((cols, 1), 0)
    head_ok = key_head == (qrow >> int(math.log2(tq)))
    out_mem = _iota2((tq, 1), 0) >> int(math.log2(seq_len))
    o = None
    for m in range(nmem):
        kall = k_ref[m].reshape(rows, XA_HEAD_DIM)
        vall = v_ref[m].reshape(rows, XA_HEAD_DIM)
        s = jnp.where(head_ok, _bdot(qblk, kall, _NT), -jnp.inf)
        e = jnp.exp(s - jnp.max(s, axis=-1, keepdims=True))
        pr = e / jnp.sum(e, axis=-1, keepdims=True)
        om = _bdot(pr, vall)
        om = jnp.concatenate([om[h * tq:(h + 1) * tq] for h in range(XA_HEADS)], axis=1)
        o = om if o is None else jnp.where(out_mem == m, om, o)
    o_ref[...] = x1_ref[...] + _bdot(o, wo_ref[...])


def _xattn_cache(q, x1, ck, cv, w_o, tq, seq_len):
    n, d = q.shape
    nmem = tq // seq_len
    row_spec = pl.BlockSpec((tq, d), lambda i: (i, 0))
    kv_spec = pl.BlockSpec((nmem, N_MEM, XA_HEADS, XA_HEAD_DIM), lambda i: (i, 0, 0, 0))
    return pl.pallas_call(
        functools.partial(_xattn_cache_kernel, seq_len=seq_len),
        grid=(n // tq,),
        in_specs=[row_spec, row_spec, kv_spec, kv_spec, _const_spec(w_o)],
        out_specs=row_spec,
        out_shape=jax.ShapeDtypeStruct((n, d), F32),
        compiler_params=_cparams("parallel"),
        name="xattn_cache",
    )(q, x1, ck, cv, w_o)


FF_SUB = 256
FF_GROUP = 4


def _ffn_kernel(*refs, tm, Lb, prompt, blocks_per_seq):
    if prompt:
        x_ref, g_ref, wu_ref, cw_ref, cb_ref, wd_ref, fg_ref, y_ref, s_ref, carry_sc = refs

        @pl.when(pl.program_id(0) % blocks_per_seq == 0)
        def _():
            carry_sc[...] = jnp.zeros_like(carry_sc)
    else:
        x_ref, hist_ref, g_ref, wu_ref, cw_ref, cb_ref, wd_ref, fg_ref, y_ref, s_ref = refs

    x = x_ref[...]
    hn = _rms(x, g_ref[...]).astype(BF16)
    nsub = D_FF // FF_SUB
    r8 = _iota2((8, 1), 0)
    tpos = _iota2((tm, 1), 0) & (Lb - 1)

    def conv(up, cols):
        cw = cw_ref[:, cols]
        cb = cb_ref[:, cols]
        if prompt:
            c8 = carry_sc[:, cols]
            carry_sc[:, cols] = up[tm - 8:tm]
            s_ref[0, :, cols] = up[tm - 2:tm]
            top = up[0:8]
            p1 = jnp.where(r8 >= 1, pltpu.roll(top, 1, 0), c8[7:8])
            p2 = jnp.where(r8 >= 2, pltpu.roll(top, 2, 0), jnp.where(r8 == 0, c8[6:7], c8[7:8]))
            head = (cw[0:1] * p2 + cw[1:2] * p1 + cw[2:3] * top) + cb
            body = (cw[0:1] * pltpu.roll(up, 2, 0) + cw[1:2] * pltpu.roll(up, 1, 0) + cw[2:3] * up) + cb
            return jnp.concatenate([head, body[8:]], axis=0)
        for b in range(tm // Lb):
            s_ref[b, :, cols] = up[b * Lb + Lb - 2:b * Lb + Lb]
        return (cw[0:1] * _shifted(up, 2, tpos, hist_ref[1, :, cols])
                + cw[1:2] * _shifted(up, 1, tpos, hist_ref[0, :, cols]) + cw[2:3] * up) + cb

    def up_proj(k):
        cg = slice(k * FF_SUB, (k + 1) * FF_SUB)
        cv = slice(D_FF + k * FF_SUB, D_FF + (k + 1) * FF_SUB)
        return (jnp.dot(hn, wu_ref[:, cg], preferred_element_type=F32),
                jnp.dot(hn, wu_ref[:, cv], preferred_element_type=F32))

    def gate_mul(k, ug, uv):
        cg = slice(k * FF_SUB, (k + 1) * FF_SUB)
        cv = slice(D_FF + k * FF_SUB, D_FF + (k + 1) * FF_SUB)
        gate = conv(ug, cg)
        return (gate * _sigmoid(gate) * conv(uv, cv)).astype(BF16)

    acc = x
    ups = {}
    acts = []
    for k in range(nsub + 1):
        if k < nsub:
            ups[k] = up_proj(k)
        if k >= 1:
            acts.append(gate_mul(k - 1, *ups.pop(k - 1)))
            if len(acts) == FF_GROUP or k == nsub:
                lo = (k - len(acts)) * FF_SUB
                acc = acc + jnp.dot(jnp.concatenate(acts, axis=1), wd_ref[lo:k * FF_SUB, :],
                                    preferred_element_type=F32)
                acts = []
    y_ref[...] = _rms(acc, fg_ref[...])


def _resident_spec(a):
    nd = a.ndim
    return pl.BlockSpec(a.shape, lambda *_: (0,) * nd, pipeline_mode=pl.Buffered(1))


def _ffn(x, hist, p, final_g, tm, Lb, prompt, seq_len):
    n, d = x.shape
    w_up = p["ffn_w_up"].astype(BF16)
    w_down = p["ffn_w_down"].astype(BF16)
    cw, cb = p["ffn_conv_w"], p["ffn_conv_b"].reshape(1, 2 * D_FF)
    row_spec = pl.BlockSpec((tm, d), lambda i: (i, 0))
    if prompt:
        bps = seq_len // tm
        nstate = n // tm
        s_spec = pl.BlockSpec((1, 2, 2 * D_FF), lambda i: (i, 0, 0))
        extra_in, extra_specs = [], []
        scratch = [pltpu.VMEM((8, 2 * D_FF), F32)]
    else:
        bps = 1
        nstate = n // seq_len
        s_spec = pl.BlockSpec((tm // Lb, 2, 2 * D_FF), lambda i: (i, 0, 0))
        extra_in = [hist]
        extra_specs = [pl.BlockSpec((2, tm, 2 * D_FF), lambda i: (0, i, 0))]
        scratch = []
    consts = [p["norm_ffn_w"].reshape(1, d), w_up, cw, cb, w_down, final_g.reshape(1, d)]
    y, st = pl.pallas_call(
        functools.partial(_ffn_kernel, tm=tm, Lb=Lb, prompt=prompt, blocks_per_seq=bps),
        grid=(n // tm,),
        in_specs=[row_spec] + extra_specs + [_resident_spec(a) for a in consts],
        out_specs=[row_spec, s_spec],
        out_shape=[jax.ShapeDtypeStruct((n, d), F32), jax.ShapeDtypeStruct((nstate, 2, 2 * D_FF), F32)],
        scratch_shapes=scratch,
        compiler_params=_cparams("arbitrary"),
        name="convffn",
    )(x, *extra_in, *consts)
    if prompt:
        st = st[bps - 1::bps]
    return y, st


def _pair_blocks(sbd):
    b = sbd.shape[0]
    s0 = sbd[:, :, 0:64, 0:64]
    s1 = sbd[:, :, 64:128, 64:128]
    return jnp.stack([s0, s1], axis=2).reshape(b, 8, 64, 64)


def _hist_rows(state, seq_len, k):
    b, w, c = state.shape
    cols = [state[:, w + t - k] if t < k else jnp.zeros((b, c), state.dtype) for t in range(seq_len)]
    return jnp.stack(cols, axis=1).reshape(b * seq_len, c)


def kernel(x_prompt, x_sample, mem_prompt, state_ssm_conv, state_ssm, state_shift, state_wkv, state_ffn_conv, cache_mem_k, cache_mem_v, norm_mix_w, w_in, ssd_conv_w, ssd_conv_b, ssd_dt_bias, ssd_a_log, ssd_d, ssd_norm_w, rwkv_mu, rwkv_w0, rwkv_w2, rwkv_a0, rwkv_a2, rwkv_g2, rwkv_k_k, rwkv_k_a, rwkv_r_k, rwkv_ln_w, rwkv_ln_b, w_out, norm_xa_w, mem_norm_w, xa_w_q, xa_w_k, xa_w_v, xa_w_o, norm_ffn_w, ffn_w_up, ffn_conv_w, ffn_conv_b, ffn_w_down, final_norm_w):
    depth = w_in.shape[0]
    assert depth == 1, "final rmsnorm is fused into the (single) layer's ConvFFN kernel"
    bp, lp, d = x_prompt.shape
    bs, ls, _ = x_sample.shape
    i = 0
    p = dict(ssd_conv_w=ssd_conv_w[i], ssd_conv_b=ssd_conv_b[i], ssd_dt_bias=ssd_dt_bias[i],
             ssd_a_log=ssd_a_log[i], ssd_d=ssd_d[i], ssd_norm_w=ssd_norm_w[i], rwkv_mu=rwkv_mu[i],
             rwkv_w0=rwkv_w0[i], rwkv_w2=rwkv_w2[i], rwkv_a0=rwkv_a0[i], rwkv_a2=rwkv_a2[i],
             rwkv_g2=rwkv_g2[i], rwkv_k_k=rwkv_k_k[i], rwkv_k_a=rwkv_k_a[i],
             rwkv_r_k=rwkv_r_k[i].reshape(-1), rwkv_ln_w=rwkv_ln_w[i], rwkv_ln_b=rwkv_ln_b[i],
             norm_ffn_w=norm_ffn_w[i], ffn_w_up=ffn_w_up[i], ffn_conv_w=ffn_conv_w[i],
             ffn_conv_b=ffn_conv_b[i], ffn_w_down=ffn_w_down[i])

    w_in_p = jnp.concatenate([w_in[i][:, :SSD_PROJ], jnp.zeros((d, SEG - SSD_PROJ), F32),
                              w_in[i][:, SSD_PROJ:]], axis=1).astype(BF16)
    w_out_b = w_out[i].astype(BF16)
    w_q_b = (xa_w_q[i]).astype(BF16)
    w_o_b = xa_w_o[i].astype(BF16)
    w_kv_b = jnp.concatenate([xa_w_k[i], xa_w_v[i]], axis=1).astype(BF16)

    xp = x_prompt.reshape(bp * lp, d)
    xs = x_sample.reshape(bs * ls, d)

    mk, mv = _mem_kv(mem_prompt.reshape(bp * N_MEM, d), mem_norm_w[i], w_kv_b, 512)
    mk = mk.reshape(bp, N_MEM, d)
    mv = mv.reshape(bp, N_MEM, d)
    u_p = _norm_proj(xp, norm_mix_w[i], w_in_p, 1024, SEG)
    y_ssd_p, ssm_p = _ssd_prompt(u_p, p, bp, lp)
    y_rw_p, wkv_bd_p = _wkv_prompt(u_p, p, bp, lp)
    x1_p, q_p = _outproj(xp, y_ssd_p, y_rw_p, w_out_b, norm_xa_w[i], w_q_b, 512)
    x2_p = _xattn(q_p, x1_p, mk, mv, w_o_b, 512, lp)
    y_p, ffn_conv_p = _ffn(x2_p, None, p, final_norm_w, 1024, lp, True, lp)
    u3 = u_p.reshape(bp, lp, 2 * SEG)
    ssm_conv_p = u3[:, lp - 3:, SSD_WIDTH:SSD_WIDTH + SSD_XBC]
    shift_p = u3[:, lp - 1, SEG:]

    u_s = _norm_proj(xs, norm_mix_w[i], w_in_p, 512, SEG)
    u3s = u_s.reshape(bs, ls, 2 * SEG)
    xbc_s = u3s[:, :, SSD_WIDTH:SSD_WIDTH + SSD_XBC]
    conv_state = state_ssm_conv[i]
    hist_conv = jnp.stack([_hist_rows(conv_state, ls, k) for k in (1, 2, 3)], axis=0)
    y_ssd_s, ssm_s = _ssd_sample(u_s, hist_conv, state_ssm[i].reshape(bs, 4, LANES, LANES), p, ls)
    hist_shift = _hist_rows(state_shift[i][:, None, :], ls, 1)
    y_rw_s, wkv_s = _wkv_sample(u_s, hist_shift, state_wkv[i], p, ls)
    x1_s, q_s = _outproj(xs, y_ssd_s, y_rw_s, w_out_b, norm_xa_w[i], w_q_b, 512)
    x2_s = _xattn_cache(q_s, x1_s, cache_mem_k[i], cache_mem_v[i], w_o_b, 8, ls)
    fstate = state_ffn_conv[i]
    hist_ffn = jnp.stack([_hist_rows(fstate, ls, k) for k in (1, 2)], axis=0)
    y_s, ffn_conv_s = _ffn(x2_s, hist_ffn, p, final_norm_w, 256, ls, False, ls)
    ssm_conv_s = jnp.concatenate([conv_state, xbc_s], axis=1)[:, -3:]
    shift_s = u3s[:, ls - 1, SEG:]

    e = lambda a: a[None]
    return (y_p.reshape(bp, lp, d), y_s.reshape(bs, ls, d),
            e(ssm_conv_p), e(ssm_conv_s),
            e(ssm_p.reshape(bp, SSD_HEADS, 64, SSD_STATE)), e(ssm_s.reshape(bs, SSD_HEADS, 64, SSD_STATE)),
            e(shift_p), e(shift_s),
            e(_pair_blocks(wkv_bd_p)), e(wkv_s),
            e(ffn_conv_p), e(ffn_conv_s),
            e(mk.reshape(bp, N_MEM, XA_HEADS, XA_HEAD_DIM)), e(mv.reshape(bp, N_MEM, XA_HEADS, XA_HEAD_DIM)))
```

```python
import functools
import math

import jax
import jax.numpy as jnp
from jax import lax
from jax.experimental import pallas as pl
from jax.experimental.pallas import tpu as pltpu

F32 = jnp.float32
BF16 = jnp.bfloat16

D_MODEL = 1024
N_MEM = 256
XA_HEADS = 4
XA_HEAD_DIM = D_MODEL // XA_HEADS
SSD_WIDTH = 512
SSD_HEADS = 8
SSD_STATE = 128
SSD_XBC = 1024
SSD_PROJ = SSD_WIDTH + SSD_XBC + SSD_HEADS
RWKV_WIDTH = 512
RWKV_PROJ = 1792
D_FF = 2816
EPS = 1e-6
GN_EPS = 64e-5

LANES = 128
SEG = 1792
DT_OFF = SSD_WIDTH + SSD_XBC
VMEM_LIMIT_BYTES = 56 * 1024 * 1024


def _cparams(*sem):
    return pltpu.CompilerParams(dimension_semantics=sem, vmem_limit_bytes=VMEM_LIMIT_BYTES)


_NN = (((1,), (0,)), ((), ()))
_NT = (((1,), (1,)), ((), ()))
_TN = (((0,), (0,)), ((), ()))


def _bdot(a, b, dims=_NN):
    return lax.dot_general(a.astype(BF16), b.astype(BF16), dims, preferred_element_type=F32)


def _split2(x):
    hi = x.astype(BF16)
    lo = (x - hi.astype(F32)).astype(BF16)
    return hi, lo


def _maskdot(m_bf16, x):
    hi = x.astype(BF16)
    r1 = x - hi.astype(F32)
    mid = r1.astype(BF16)
    lo = (r1 - mid.astype(F32)).astype(BF16)
    return jnp.dot(jnp.concatenate([m_bf16, m_bf16, m_bf16], axis=1),
                   jnp.concatenate([hi, mid, lo], axis=0), preferred_element_type=F32)


def _xmask(x, m2_bf16):
    hi, lo = _split2(x)
    return jnp.dot(jnp.concatenate([hi, lo], axis=1), m2_bf16, preferred_element_type=F32)


def _sigmoid(x):
    return 0.5 + 0.5 * jnp.tanh(0.5 * x)


def _softplus(x):
    return jnp.maximum(x, 0.0) + jnp.log1p(jnp.exp(-jnp.abs(x)))


def _rms(x, g):
    return x * lax.rsqrt(jnp.mean(x * x, axis=-1, keepdims=True) + EPS) * g


def _iota2(shape, dim):
    return lax.broadcasted_iota(jnp.int32, shape, dim)


def _shifted(x, k, tpos, hist):
    return jnp.where(tpos >= k, pltpu.roll(x, k, 0), hist)


def _norm_proj_kernel(x_ref, g_ref, w_ref, o_ref, hn_sc):
    @pl.when(pl.program_id(1) == 0)
    def _():
        hn_sc[...] = _rms(x_ref[...], g_ref[...]).astype(BF16)

    o_ref[...] = jnp.dot(hn_sc[...], w_ref[...], preferred_element_type=F32)


def _norm_proj(x, g, w_bf16, tm, tn):
    n, d = x.shape
    f = w_bf16.shape[1]
    return pl.pallas_call(
        _norm_proj_kernel,
        grid=(n // tm, f // tn),
        in_specs=[pl.BlockSpec((tm, d), lambda i, j: (i, 0)),
                  pl.BlockSpec((1, d), lambda i, j: (0, 0)),
                  pl.BlockSpec((d, tn), lambda i, j: (0, j))],
        out_specs=pl.BlockSpec((tm, tn), lambda i, j: (i, j)),
        out_shape=jax.ShapeDtypeStruct((n, f), F32),
        scratch_shapes=[pltpu.VMEM((tm, d), BF16)],
        compiler_params=_cparams("parallel", "arbitrary"),
        name="norm_proj",
    )(x, g.reshape(1, d), w_bf16)


def _mem_kv_kernel(x_ref, g_ref, w_ref, k_ref, v_ref):
    hn = _rms(x_ref[...], g_ref[...]).astype(BF16)
    d = k_ref.shape[1]
    k_ref[...] = jnp.dot(hn, w_ref[:, 0:d], preferred_element_type=F32)
    v_ref[...] = jnp.dot(hn, w_ref[:, d:2 * d], preferred_element_type=F32)


def _mem_kv(x, g, w_kv_bf16, tm):
    n, d = x.shape
    row_spec = pl.BlockSpec((tm, d), lambda i: (i, 0))
    return pl.pallas_call(
        _mem_kv_kernel,
        grid=(n // tm,),
        in_specs=[row_spec, pl.BlockSpec((1, d), lambda i: (0, 0)), _const_spec(w_kv_bf16)],
        out_specs=[row_spec, row_spec],
        out_shape=[jax.ShapeDtypeStruct((n, d), F32)] * 2,
        compiler_params=_cparams("parallel"),
        name="mem_kv",
    )(x, g.reshape(1, d), w_kv_bf16)


def _ssd_kernel(*refs, Q, Lb, prompt, G=1):
    if not prompt:
        return _ssd_block(*refs, Q=Q, Lb=Lb, prompt=False)
    (u_ref, cw_ref, cb_ref, dtb_ref, an_ref, dsk_ref, nw_ref, y_ref, hout_ref, carry_sc, h_sc) = refs

    @pl.when(pl.program_id(1) == 0)
    def _():
        carry_sc[...] = jnp.zeros_like(carry_sc)
        h_sc[...] = jnp.zeros_like(h_sc)

    for gi in range(G):
        _ssd_block(u_ref.at[gi], cw_ref, cb_ref, dtb_ref, an_ref, dsk_ref, nw_ref, y_ref.at[gi],
                   hout_ref.at[pl.ds(gi, 1)], carry_sc.at[gi], h_sc.at[pl.ds(4 * gi, 4)],
                   Q=Q, Lb=Lb, prompt=True)


def _ssd_block(*refs, Q, Lb, prompt):
    ns = Q // Lb
    lb = int(math.log2(Lb))
    if prompt:
        (u_ref, cw_ref, cb_ref, dtb_ref, an_ref, dsk_ref, nw_ref,
         y_ref, hout_ref, carry_sc, h_sc) = refs
    else:
        (u_ref, hist_ref, hin_ref, cw_ref, cb_ref, dtb_ref, an_ref, dsk_ref, nw_ref,
         y_ref, hout_ref) = refs

    u = u_ref[...]
    z = u[:, 0:SSD_WIDTH]
    x = u[:, SSD_WIDTH:SSD_WIDTH + SSD_XBC]
    dtr = u[:, DT_OFF:DT_OFF + LANES]

    rows = _iota2((Q, 1), 0)
    tpos = rows & (Lb - 1)
    if prompt:
        c8 = carry_sc[...]
        h1 = c8[7:8]
        h2 = jnp.where(rows == 0, c8[6:7], c8[7:8])
        h3 = jnp.where(rows == 0, c8[5:6], jnp.where(rows == 1, c8[6:7], c8[7:8]))
        carry_sc[...] = x[Q - 8:Q]
    else:
        h1, h2, h3 = hist_ref[0], hist_ref[1], hist_ref[2]
    cw = cw_ref[...]
    xc = (cw[0:1] * _shifted(x, 3, tpos, h3) + cw[1:2] * _shifted(x, 2, tpos, h2)
          + cw[2:3] * _shifted(x, 1, tpos, h1) + cw[3:4] * x) + cb_ref[...]
    xc = xc * _sigmoid(xc)
    xs = xc[:, 0:SSD_WIDTH]
    bm = xc[:, SSD_WIDTH:SSD_WIDTH + 2 * SSD_STATE]
    cm = xc[:, SSD_WIDTH + 2 * SSD_STATE:]

    dt = _softplus(dtr + dtb_ref[...])
    da = dt * an_ref[...]

    ri = _iota2((Q, Q), 0)
    ci = _iota2((Q, Q), 1)
    same = (ri >> lb) == (ci >> lb)
    tril = same & (ci <= ri)
    sel = ci == (((ri >> lb) << lb) + (Lb - 1))
    acs = _maskdot(tril.astype(BF16), da)
    acs_t = acs.T
    acs_last = _maskdot(sel.astype(BF16), acs)
    dec_end = jnp.exp(acs_last - acs)
    eacs = jnp.exp(acs)
    seqid = rows >> lb

    lane = _iota2((Q, LANES), 1)
    lo_half = lane < 64
    prow = _iota2((LANES, LANES), 0)
    dsk = dsk_ref[...]

    ys = []
    for q in range(4):
        g = q // 2
        h0, h1i = 2 * q, 2 * q + 1
        if q % 2 == 0:
            cg = cm[:, g * SSD_STATE:(g + 1) * SSD_STATE]
            bg = bm[:, g * SSD_STATE:(g + 1) * SSD_STATE]
            cb_g = _bdot(cg, bg, _NT)
            if ns == 1:
                cexp, bexp = cg.astype(BF16), bg.astype(BF16)
            else:
                cexp = jnp.concatenate([jnp.where(seqid == b, cg, 0.0).astype(BF16) for b in range(ns)], axis=1)
                bexp = jnp.concatenate([jnp.where(seqid == b, bg, 0.0).astype(BF16) for b in range(ns)], axis=1)
        m0 = jnp.where(tril, cb_g * jnp.exp(acs[:, h0:h0 + 1] - acs_t[h0:h0 + 1, :]), 0.0)
        m1 = jnp.where(tril, cb_g * jnp.exp(acs[:, h1i:h1i + 1] - acs_t[h1i:h1i + 1, :]), 0.0)
        xp = xs[:, q * LANES:(q + 1) * LANES]
        xdt = xp * jnp.where(lo_half, dt[:, h0:h0 + 1], dt[:, h1i:h1i + 1])
        xdt0 = jnp.where(lo_half, xdt, 0.0)
        xdt1 = jnp.where(lo_half, 0.0, xdt)
        ydiag = _bdot(jnp.concatenate([m0, m1], axis=1), jnp.concatenate([xdt0, xdt1], axis=0))
        if prompt:
            hst = h_sc[q]
        else:
            hst = jnp.concatenate([hin_ref[b, q] for b in range(ns)], axis=1)
        ecs = jnp.where(lo_half, eacs[:, h0:h0 + 1], eacs[:, h1i:h1i + 1])
        yoff = _bdot(cexp, hst, _NT) * ecs
        xd = xdt * jnp.where(lo_half, dec_end[:, h0:h0 + 1], dec_end[:, h1i:h1i + 1])
        incr = _bdot(xd, bexp, _TN)
        scales = []
        for b in range(ns):
            r = b * Lb + Lb - 1
            e = eacs[r:r + 1, :]
            scales.append(jnp.where(prow < 64, e[:, h0:h0 + 1], e[:, h1i:h1i + 1]))
        scale = scales[0] if ns == 1 else jnp.concatenate(scales, axis=1)
        hnew = hst * scale + incr
        if prompt:
            h_sc[q] = hnew
            hout_ref[0, q] = hnew
        else:
            for b in range(ns):
                hout_ref[b, q] = hnew[:, b * LANES:(b + 1) * LANES]
        ys.append(ydiag + yoff + dsk[:, q * LANES:(q + 1) * LANES] * xp)

    y = jnp.concatenate(ys, axis=1)
    y = y * (z * _sigmoid(z))
    half = SSD_WIDTH // 2
    outs = []
    for g in range(2):
        yg = y[:, g * half:(g + 1) * half]
        outs.append(yg * lax.rsqrt(jnp.mean(yg * yg, axis=-1, keepdims=True) + EPS))
    y_ref[...] = (jnp.concatenate(outs, axis=1) * nw_ref[...]).astype(y_ref.dtype)


def _ssd_params(p):
    an = jnp.zeros((1, LANES), F32).at[0, :SSD_HEADS].set(-jnp.exp(p["ssd_a_log"]))
    dtb = jnp.zeros((1, LANES), F32).at[0, :SSD_HEADS].set(p["ssd_dt_bias"])
    dsk = jnp.repeat(p["ssd_d"], SSD_WIDTH // SSD_HEADS).reshape(1, SSD_WIDTH)
    return [p["ssd_conv_w"], p["ssd_conv_b"].reshape(1, SSD_XBC), dtb, an, dsk,
            p["ssd_norm_w"].reshape(1, SSD_WIDTH)]


def _const_spec(a):
    nd = a.ndim
    return pl.BlockSpec(a.shape, lambda *_: (0,) * nd)


def _ssd_prompt(u, p, batch, seq, Q=128, G=4):
    nc = seq // Q
    params = _ssd_params(p)
    y, hout = pl.pallas_call(
        functools.partial(_ssd_kernel, Q=Q, Lb=Q, prompt=True, G=G),
        grid=(batch // G, nc),
        in_specs=[pl.BlockSpec((G, Q, SEG), lambda b, c: (b, c, 0))] + [_const_spec(a) for a in params],
        out_specs=[pl.BlockSpec((G, Q, SSD_WIDTH), lambda b, c: (b, c, 0)),
                   pl.BlockSpec((G, 4, LANES, LANES), lambda b, c: (b, 0, 0, 0))],
        out_shape=[jax.ShapeDtypeStruct((batch, seq, SSD_WIDTH), BF16),
                   jax.ShapeDtypeStruct((batch, 4, LANES, LANES), F32)],
        scratch_shapes=[pltpu.VMEM((G, 8, SSD_XBC), F32), pltpu.VMEM((4 * G, LANES, LANES), F32)],
        compiler_params=_cparams("parallel", "arbitrary"),
        name="ssd_prompt",
    )(u.reshape(batch, seq, 2 * SEG), *params)
    return y.reshape(batch * seq, SSD_WIDTH), hout


def _ssd_sample(u, hist, hin, p, Lb, Q=64):
    n = u.shape[0]
    ns = Q // Lb
    params = _ssd_params(p)
    y, hout = pl.pallas_call(
        functools.partial(_ssd_kernel, Q=Q, Lb=Lb, prompt=False),
        grid=(n // Q,),
        in_specs=[pl.BlockSpec((Q, SEG), lambda i: (i, 0)),
                  pl.BlockSpec((3, Q, SSD_XBC), lambda i: (0, i, 0)),
                  pl.BlockSpec((ns, 4, LANES, LANES), lambda i: (i, 0, 0, 0))]
                 + [_const_spec(a) for a in params],
        out_specs=[pl.BlockSpec((Q, SSD_WIDTH), lambda i: (i, 0)),
                   pl.BlockSpec((ns, 4, LANES, LANES), lambda i: (i, 0, 0, 0))],
        out_shape=[jax.ShapeDtypeStruct((n, SSD_WIDTH), BF16),
                   jax.ShapeDtypeStruct(hin.shape, F32)],
        compiler_params=_cparams("parallel"),
        name="ssd_sample",
    )(u, hist, hin, *params)
    return y, hout


def _dot3s(a, b, dims=_NN):
    ka = dims[0][0][0]
    kb = dims[0][1][0]
    lhs = jnp.concatenate([a[0], a[0], a[1]], axis=ka)
    rhs = jnp.concatenate([b[0], b[1], b[0]], axis=kb)
    return lax.dot_general(lhs, rhs, dims, preferred_element_type=F32)


def _cat2(parts, axis):
    return (jnp.concatenate([p[0] for p in parts], axis=axis), jnp.concatenate([p[1] for p in parts], axis=axis))


def _tri_inverse(mats, lb):
    n = mats[0].shape[0]
    ri = _iota2((n, n), 0)
    ci = _iota2((n, n), 1)
    off1 = ((ri >> 1) == (ci >> 1)) & ((ri & 1) == 1) & ((ci & 1) == 0)
    eye = jnp.where(ri == ci, 1.0, 0.0)
    ts = [eye + jnp.where(off1, a, 0.0) for a in mats]
    for lvl in range(1, lb):
        m = 1 << lvl
        off = ((ri >> (lvl + 1)) == (ci >> (lvl + 1))) & ((ri & (2 * m - 1)) >= m) & ((ci & (2 * m - 1)) < m)
        tsb = [t.astype(BF16) for t in ts]
        ws = [jnp.dot(jnp.where(off, a, 0.0).astype(BF16), tb, preferred_element_type=F32)
              for a, tb in zip(mats, tsb)]
        ts = [t + jnp.dot(tb, w.astype(BF16), preferred_element_type=F32) for t, tb, w in zip(ts, tsb, ws)]
    return [t.astype(BF16) for t in ts]


def _refined_solve(tinvs, mats, rhss):
    d = functools.partial(jnp.dot, preferred_element_type=F32)
    n = range(len(mats))
    rs = [_split2(r) for r in rhss]
    u0 = [d(jnp.concatenate([tinvs[i], tinvs[i]], axis=1), jnp.concatenate(rs[i], axis=0)) for i in n]
    au = [_dot3s(_split2(mats[i]), _split2(u0[i])) for i in n]
    res = [((rhss[i] - u0[i]) + au[i]).astype(BF16) for i in n]
    return [u0[i] + d(tinvs[i], res[i]) for i in n]


def _wkv_kernel(*refs, C, Lb, prompt, G):
    nb = C // Lb
    lb = int(math.log2(Lb))
    R = 2 * C
    T = G * C
    if prompt:
        (u_ref, mu_ref, w0_ref, w2_ref, a0_ref, a2_ref, g2_ref, kk_ref, ka_ref, rk_ref, lnw_ref, lnb_ref,
         y_ref, sout_ref, carry_sc, s_sc) = refs
        c = pl.program_id(1)

        @pl.when(c == 0)
        def _():
            carry_sc[...] = jnp.zeros_like(carry_sc)
            s_sc[...] = jnp.zeros_like(s_sc)
    else:
        (u_ref, hist_ref, sin_ref, mu_ref, w0_ref, w2_ref, a0_ref, a2_ref, g2_ref, kk_ref, ka_ref, rk_ref,
         lnw_ref, lnb_ref, y_ref, sout_ref) = refs

    rows = _iota2((T, 1), 0)
    tpos = rows & (Lb - 1)
    if prompt:
        u = u_ref[...].reshape(T, SEG)
        hist = jnp.concatenate([jnp.broadcast_to(carry_sc[g, 7:8], (C, SEG)) for g in range(G)], axis=0)
        for g in range(G):
            carry_sc[g] = u[g * C + C - 8:(g + 1) * C]
    else:
        u = u_ref[...]
        hist = hist_ref[...]
    um = u + (_shifted(u, 1, tpos, hist) - u) * mu_ref[...]

    W = RWKV_WIDTH
    r = um[:, 0:W]
    k = um[:, W:2 * W]
    v = um[:, 2 * W:3 * W]
    t12 = um[:, 3 * W:3 * W + LANES]
    lg = um[:, 3 * W + LANES:3 * W + 2 * LANES]

    wl = w0_ref[...] + _bdot(jnp.tanh(t12), w2_ref[...])
    logw = -jnp.exp(-_softplus(-wl) - 0.5)
    a = _sigmoid(a0_ref[...] + _bdot(t12, a2_ref[...]))
    out_gate = _bdot(_sigmoid(lg), g2_ref[...])

    hi = _iota2((2 * W, W), 0) & (W - 1)
    hj = _iota2((2 * W, W), 1)
    headsum = ((hi >> 6) == (hj >> 6)).astype(BF16)

    kk = k * kk_ref[...]
    kk = kk / jnp.maximum(jnp.sqrt(_xmask(kk * kk, headsum)), 1e-12)
    kmod = k * (1.0 + (a - 1.0) * ka_ref[...])
    beta = kk * a

    ri = _iota2((T, T), 0)
    ci = _iota2((T, T), 1)
    same_c = (ri >> lb) == (ci >> lb)
    lc = _maskdot((same_c & (ci <= ri)).astype(BF16), logw)
    lc_last = _maskdot((ci == (((ri >> lb) << lb) + (Lb - 1))).astype(BF16), lc)
    e_neg = jnp.exp(-lc)
    e_end = jnp.exp(lc_last - lc)
    at = -kk * jnp.exp(lc - logw)
    rt = r * jnp.exp(lc)
    bt = beta * e_neg
    kt = kmod * e_neg
    bh = beta * e_end
    kh = kmod * e_end
    p_last = jnp.exp(lc_last)

    si = _iota2((R, R), 0)
    sj = _iota2((R, R), 1)
    same_s = (si >> lb) == (sj >> lb)
    strict = same_s & (sj < si)
    incl = same_s & (sj <= si)
    incl2 = jnp.concatenate([incl, incl], axis=1)
    lane_row = _iota2((1, LANES), 1)
    m_lo = jnp.where(lane_row < 64, 1.0, 0.0).astype(BF16)
    m_hi = jnp.where(lane_row < 64, 0.0, 1.0).astype(BF16)
    if nb > 1:
        seq_s = (_iota2((R, LANES), 0) & (C - 1)) >> lb
        seq_masks = [jnp.where(seq_s == b, 1.0, 0.0).astype(BF16) for b in range(nb)]

    def stack(xp):
        return jnp.concatenate([xp * m_lo, xp * m_hi], axis=0)

    def expand(xs_):
        if nb == 1:
            return xs_
        return jnp.concatenate([xs_ * seq_masks[b] for b in range(nb)], axis=1)

    probs = [(g, p) for g in range(G) for p in range(4)]
    NP = range(len(probs))

    def tile(x, i):
        g, p = probs[i]
        return x[g * C:(g + 1) * C, p * LANES:(p + 1) * LANES]

    def stacked(pair, i):
        return (stack(tile(pair[0], i)), stack(tile(pair[1], i)))

    at2, v2, bt2, kt2, bh2, kh2 = (_split2(x) for x in (at, v, bt, kt, bh, kh))
    rt_b = rt.astype(BF16)
    a_s = [stacked(at2, i) for i in NP]
    r_s = [stack(tile(rt_b, i)) for i in NP]
    v_s = [stacked(v2, i) for i in NP]
    bk_s = [_cat2([stacked(bt2, i), stacked(kt2, i)], 0) for i in NP]
    gm_a = [_dot3s(a_s[i], bk_s[i], _NT) for i in NP]
    gm_r = [lax.dot_general(r_s[i], bk_s[i][0], _NT, preferred_element_type=F32) for i in NP]
    a_ab = [jnp.where(strict, gm_a[i][:, 0:R], 0.0) for i in NP]
    a_ak = [_split2(jnp.where(strict, gm_a[i][:, R:2 * R], 0.0)) for i in NP]
    a_r = [jnp.where(incl2, gm_r[i], 0.0).astype(BF16) for i in NP]
    akv = [_dot3s(a_ak[i], v_s[i]) for i in NP]
    tinv = _tri_inverse(a_ab, lb)

    if prompt:
        sst = [s_sc[i] for i in NP]
    else:
        z64 = jnp.zeros((64, 64), F32)

        def pair_blockdiag(b, p):
            top = jnp.concatenate([sin_ref[b, 2 * p], z64], axis=1)
            bot = jnp.concatenate([z64, sin_ref[b, 2 * p + 1]], axis=1)
            return jnp.concatenate([top, bot], axis=0)

        sst = [jnp.concatenate([pair_blockdiag(g * nb + b, p) for b in range(nb)], axis=1) for g, p in probs]
    ss = [_split2(s) for s in sst]
    ar0_a = [_dot3s((expand(a_s[i][0]), expand(a_s[i][1])), ss[i], _NT) for i in NP]
    ar0_r = [lax.dot_general(expand(r_s[i]), ss[i][0], _NT, preferred_element_type=F32) for i in NP]
    us = [_split2(x) for x in _refined_solve(tinv, a_ab, [ar0_a[i] + akv[i] for i in NP])]
    uv = [_cat2([us[i], v_s[i]], 0) for i in NP]
    yst = [ar0_r[i] + jnp.dot(a_r[i], uv[i][0], preferred_element_type=F32) for i in NP]
    ys = [yst[i][0:C] + yst[i][C:R] for i in NP]
    for i in NP:
        g, p = probs[i]
        pl_lanes = tile(p_last, i)
        if nb == 1:
            plast = pl_lanes[0:1]
        else:
            plast = jnp.concatenate([pl_lanes[b * Lb:b * Lb + 1] for b in range(nb)], axis=1)
        bh_s, kh_s = stacked(bh2, i), stacked(kh2, i)
        bkh = (jnp.concatenate([expand(bh_s[0]), expand(kh_s[0])], axis=0),
               jnp.concatenate([expand(bh_s[1]), expand(kh_s[1])], axis=0))
        snew = sst[i] * plast + _dot3s(uv[i], bkh, _TN)
        if prompt:
            s_sc[i] = snew
            sout_ref[g, p] = snew
        else:
            for b in range(nb):
                sout_ref[g * nb + b, 2 * p] = snew[0:64, b * LANES:b * LANES + 64]
                sout_ref[g * nb + b, 2 * p + 1] = snew[64:128, b * LANES + 64:(b + 1) * LANES]

    y = jnp.concatenate([jnp.concatenate(ys[4 * g:4 * g + 4], axis=1) for g in range(G)], axis=0)
    inv_d = 1.0 / 64.0
    mean = _xmask(y, headsum) * inv_d
    yc = y - mean
    var = _xmask(yc * yc, headsum) * inv_d
    yn = yc * lax.rsqrt(var + GN_EPS) * lnw_ref[...] + lnb_ref[...]
    yn = yn + _xmask(r * kmod * rk_ref[...], headsum) * v
    y_ref[...] = (yn * out_gate).astype(y_ref.dtype).reshape(y_ref.shape)


def _wkv_params(p):
    z64 = jnp.zeros((64, RWKV_WIDTH), F32)
    w2 = jnp.concatenate([p["rwkv_w2"], z64], axis=0).astype(BF16)
    a2 = jnp.concatenate([z64, p["rwkv_a2"]], axis=0).astype(BF16)
    row = lambda a: a.reshape(1, -1)
    return [row(p["rwkv_mu"]), row(p["rwkv_w0"]), w2, row(p["rwkv_a0"]), a2, p["rwkv_g2"].astype(BF16),
            row(p["rwkv_k_k"]), row(p["rwkv_k_a"]), row(p["rwkv_r_k"]), row(p["rwkv_ln_w"]), row(p["rwkv_ln_b"])]


def _wkv_prompt(u, p, batch, seq, C=64, G=4):
    nc = seq // C
    params = _wkv_params(p)
    y, sout = pl.pallas_call(
        functools.partial(_wkv_kernel, C=C, Lb=C, prompt=True, G=G),
        grid=(batch // G, nc),
        in_specs=[pl.BlockSpec((G, C, SEG), lambda b, c: (b, c, 1))] + [_const_spec(a) for a in params],
        out_specs=[pl.BlockSpec((G, C, RWKV_WIDTH), lambda b, c: (b, c, 0)),
                   pl.BlockSpec((G, 4, LANES, LANES), lambda b, c: (b, 0, 0, 0))],
        out_shape=[jax.ShapeDtypeStruct((batch, seq, RWKV_WIDTH), BF16),
                   jax.ShapeDtypeStruct((batch, 4, LANES, LANES), F32)],
        scratch_shapes=[pltpu.VMEM((G, 8, SEG), F32), pltpu.VMEM((4 * G, LANES, LANES), F32)],
        compiler_params=_cparams("parallel", "arbitrary"),
        name="wkv_prompt",
    )(u.reshape(batch, seq, 2 * SEG), *params)
    return y.reshape(batch * seq, RWKV_WIDTH), sout


def _wkv_sample(u, hist, sin, p, Lb, C=64, G=2):
    n = u.shape[0]
    nb = C // Lb
    T = G * C
    params = _wkv_params(p)
    return pl.pallas_call(
        functools.partial(_wkv_kernel, C=C, Lb=Lb, prompt=False, G=G),
        grid=(n // T,),
        in_specs=[pl.BlockSpec((T, SEG), lambda i: (i, 1)),
                  pl.BlockSpec((T, SEG), lambda i: (i, 0)),
                  pl.BlockSpec((G * nb, 8, 64, 64), lambda i: (i, 0, 0, 0))]
                 + [_const_spec(a) for a in params],
        out_specs=[pl.BlockSpec((T, RWKV_WIDTH), lambda i: (i, 0)),
                   pl.BlockSpec((G * nb, 8, 64, 64), lambda i: (i, 0, 0, 0))],
        out_shape=[jax.ShapeDtypeStruct((n, RWKV_WIDTH), BF16),
                   jax.ShapeDtypeStruct(sin.shape, F32)],
        compiler_params=_cparams("parallel"),
        name="wkv_sample",
    )(u, hist, sin, *params)


def _outproj_kernel(x_ref, ys_ref, yr_ref, wo_ref, g_ref, wq_ref, x1_ref, q_ref):
    wo = wo_ref[...]
    x1 = x_ref[...] + (_bdot(ys_ref[...], wo[0:SSD_WIDTH]) + _bdot(yr_ref[...], wo[SSD_WIDTH:]))
    x1_ref[...] = x1
    q_ref[...] = _bdot(_rms(x1, g_ref[...]), wq_ref[...])


def _outproj(x, y_ssd, y_rw, w_out, g, w_q, tm):
    n, d = x.shape
    return pl.pallas_call(
        _outproj_kernel,
        grid=(n // tm,),
        in_specs=[pl.BlockSpec((tm, d), lambda i: (i, 0)),
                  pl.BlockSpec((tm, SSD_WIDTH), lambda i: (i, 0)),
                  pl.BlockSpec((tm, RWKV_WIDTH), lambda i: (i, 0)),
                  _const_spec(w_out), pl.BlockSpec((1, d), lambda i: (0, 0)), _const_spec(w_q)],
        out_specs=[pl.BlockSpec((tm, d), lambda i: (i, 0)), pl.BlockSpec((tm, d), lambda i: (i, 0))],
        out_shape=[jax.ShapeDtypeStruct((n, d), F32), jax.ShapeDtypeStruct((n, d), F32)],
        compiler_params=_cparams("parallel"),
        name="outproj_q",
    )(x, y_ssd, y_rw, w_out, g.reshape(1, d), w_q)


def _xattn_kernel(q_ref, x1_ref, k_ref, v_ref, wo_ref, o_ref):
    q = q_ref[...]
    scale = XA_HEAD_DIM ** -0.5
    outs = []
    for h in range(XA_HEADS):
        sl = slice(h * XA_HEAD_DIM, (h + 1) * XA_HEAD_DIM)
        s = _bdot(q[:, sl], k_ref[0, :, sl], _NT) * scale
        s = s - jnp.max(s, axis=-1, keepdims=True)
        e = jnp.exp(s)
        pr = e / jnp.sum(e, axis=-1, keepdims=True)
        outs.append(_bdot(pr, v_ref[0, :, sl]))
    o_ref[...] = x1_ref[...] + _bdot(jnp.concatenate(outs, axis=1), wo_ref[...])


def _xattn(q, x1, mk, mv, w_o, tq, seq_len):
    n, d = q.shape
    per = seq_len // tq
    row_spec = pl.BlockSpec((tq, d), lambda i: (i, 0))
    kv_spec = pl.BlockSpec((1, N_MEM, d), lambda i: (i // per, 0, 0))
    return pl.pallas_call(
        _xattn_kernel,
        grid=(n // tq,),
        in_specs=[row_spec, row_spec, kv_spec, kv_spec, _const_spec(w_o)],
        out_specs=row_spec,
        out_shape=jax.ShapeDtypeStruct((n, d), F32),
        compiler_params=_cparams("parallel"),
        name="xattn",
    )(q, x1, mk, mv, w_o)


def _xattn_cache_kernel(q_ref, x1_ref, k_ref, v_ref, wo_ref, o_ref, *, seq_len):
    tq = q_ref.shape[0]
    nmem = tq // seq_len
    rows = N_MEM * XA_HEADS
    cols = XA_HEADS * tq
    q = q_ref[...] * (XA_HEAD_DIM ** -0.5)
    qblk = jnp.concatenate([q[:, h * XA_HEAD_DIM:(h + 1) * XA_HEAD_DIM] for h in range(XA_HEADS)], axis=0)
    key_head = _iota2((cols, rows), 1) & (XA_HEADS - 1)
    qrow = _iota2((cols, 1), 0)
    head_ok = key_head == (qrow >> int(math.log2(tq)))
    out_mem = _iota2((tq, 1), 0) >> int(math.log2(seq_len))
    o = None
    for m in range(nmem):
        kall = k_ref[m].reshape(rows, XA_HEAD_DIM)
        vall = v_ref[m].reshape(rows, XA_HEAD_DIM)
        s = jnp.where(head_ok, _bdot(qblk, kall, _NT), -jnp.inf)
        e = jnp.exp(s - jnp.max(s, axis=-1, keepdims=True))
        pr = e / jnp.sum(e, axis=-1, keepdims=True)
        om = _bdot(pr, vall)
        om = jnp.concatenate([om[h * tq:(h + 1) * tq] for h in range(XA_HEADS)], axis=1)
        o = om if o is None else jnp.where(out_mem == m, om, o)
    o_ref[...] = x1_ref[...] + _bdot(o, wo_ref[...])


def _xattn_cache(q, x1, ck, cv, w_o, tq, seq_len):
    n, d = q.shape
    nmem = tq // seq_len
    row_spec = pl.BlockSpec((tq, d), lambda i: (i, 0))
    kv_spec = pl.BlockSpec((nmem, N_MEM, XA_HEADS, XA_HEAD_DIM), lambda i: (i, 0, 0, 0))
    return pl.pallas_call(
        functools.partial(_xattn_cache_kernel, seq_len=seq_len),
        grid=(n // tq,),
        in_specs=[row_spec, row_spec, kv_spec, kv_spec, _const_spec(w_o)],
        out_specs=row_spec,
        out_shape=jax.ShapeDtypeStruct((n, d), F32),
        compiler_params=_cparams("parallel"),
        name="xattn_cache",
    )(q, x1, ck, cv, w_o)


FF_SUB = 256
FF_GROUP = 4


def _ffn_kernel(*refs, tm, Lb, prompt, blocks_per_seq):
    if prompt:
        x_ref, g_ref, wu_ref, cw_ref, cb_ref, wd_ref, fg_ref, y_ref, s_ref, carry_sc = refs
        hist_ref = None
    else:
        x_ref, hist_ref, g_ref, wu_ref, cw_ref, cb_ref, wd_ref, fg_ref, y_ref, s_ref = refs
        carry_sc = None
    _ffn_body(x_ref[...], hist_ref, g_ref, wu_ref, cw_ref, cb_ref, wd_ref, fg_ref, y_ref, s_ref, carry_sc,
              tm=tm, Lb=Lb, prompt=prompt, blocks_per_seq=blocks_per_seq)


def _ffn_body(x, hist_ref, g_ref, wu_ref, cw_ref, cb_ref, wd_ref, fg_ref, y_ref, s_ref, carry_sc,
              *, tm, Lb, prompt, blocks_per_seq):
    if prompt:
        @pl.when(pl.program_id(0) % blocks_per_seq == 0)
        def _():
            carry_sc[...] = jnp.zeros_like(carry_sc)

    hn = _rms(x, g_ref[...]).astype(BF16)
    nsub = D_FF // FF_SUB
    r8 = _iota2((8, 1), 0)
    tpos = _iota2((tm, 1), 0) & (Lb - 1)
    if not prompt:
        trow = _iota2((tm, tm // Lb), 0)
        seq0 = _iota2((tm, tm // Lb), 1) * Lb
        at_row0 = (trow == seq0).astype(BF16)
        at_row1 = (trow == seq0 + 1).astype(BF16)

    def conv(up, cols):
        cw = cw_ref[:, cols]
        cb = cb_ref[:, cols]
        if prompt:
            c8 = carry_sc[:, cols]
            carry_sc[:, cols] = up[tm - 8:tm]
            s_ref[0, :, cols] = up[tm - 2:tm]
            top = up[0:8]
            p1 = jnp.where(r8 >= 1, pltpu.roll(top, 1, 0), c8[7:8])
            p2 = jnp.where(r8 >= 2, pltpu.roll(top, 2, 0), jnp.where(r8 == 0, c8[6:7], c8[7:8]))
            head = (cw[0:1] * p2 + cw[1:2] * p1 + cw[2:3] * top) + cb
            body = (cw[0:1] * pltpu.roll(up, 2, 0) + cw[1:2] * pltpu.roll(up, 1, 0) + cw[2:3] * up) + cb
            return jnp.concatenate([head, body[8:]], axis=0)
        for b in range(tm // Lb):
            s_ref[b, :, cols] = up[b * Lb + Lb - 2:b * Lb + Lb]
        st0, st1 = hist_ref[0, :, cols], hist_ref[1, :, cols]
        h1 = _maskdot(at_row0, st1)
        h2 = _maskdot(at_row0, st0) + _maskdot(at_row1, st1)
        return (cw[0:1] * _shifted(up, 2, tpos, h2) + cw[1:2] * _shifted(up, 1, tpos, h1) + cw[2:3] * up) + cb

    def up_proj(k):
        cg = slice(k * FF_SUB, (k + 1) * FF_SUB)
        cv = slice(D_FF + k * FF_SUB, D_FF + (k + 1) * FF_SUB)
        return (jnp.dot(hn, wu_ref[:, cg], preferred_element_type=F32),
                jnp.dot(hn, wu_ref[:, cv], preferred_element_type=F32))

    def gate_mul(k, ug, uv):
        cg = slice(k * FF_SUB, (k + 1) * FF_SUB)
        cv = slice(D_FF + k * FF_SUB, D_FF + (k + 1) * FF_SUB)
        gate = conv(ug, cg)
        return (gate * _sigmoid(gate) * conv(uv, cv)).astype(BF16)

    acc = x
    ups = {}
    acts = []
    for k in range(nsub + 1):
        if k < nsub:
            ups[k] = up_proj(k)
        if k >= 1:
            acts.append(gate_mul(k - 1, *ups.pop(k - 1)))
            if len(acts) == FF_GROUP or k == nsub:
                lo = (k - len(acts)) * FF_SUB
                acc = acc + jnp.dot(jnp.concatenate(acts, axis=1), wd_ref[lo:k * FF_SUB, :],
                                    preferred_element_type=F32)
                acts = []
    y_ref[...] = _rms(acc, fg_ref[...])


def _resident_spec(a):
    nd = a.ndim
    return pl.BlockSpec(a.shape, lambda *_: (0,) * nd, pipeline_mode=pl.Buffered(1))


def _ffn(x, hist, p, final_g, tm, Lb, prompt, seq_len):
    n, d = x.shape
    w_up = p["ffn_w_up"].astype(BF16)
    w_down = p["ffn_w_down"].astype(BF16)
    cw, cb = p["ffn_conv_w"], p["ffn_conv_b"].reshape(1, 2 * D_FF)
    row_spec = pl.BlockSpec((tm, d), lambda i: (i, 0))
    if prompt:
        bps = seq_len // tm
        nstate = n // tm
        s_spec = pl.BlockSpec((1, 2, 2 * D_FF), lambda i: (i, 0, 0))
        extra_in, extra_specs = [], []
        scratch = [pltpu.VMEM((8, 2 * D_FF), F32)]
    else:
        bps = 1
        nstate = n // seq_len
        s_spec = pl.BlockSpec((tm // Lb, 2, 2 * D_FF), lambda i: (i, 0, 0))
        extra_in = [hist]
        extra_specs = [pl.BlockSpec((2, tm // Lb, 2 * D_FF), lambda i: (0, i, 0))]
        scratch = []
    consts = [p["norm_ffn_w"].reshape(1, d), w_up, cw, cb, w_down, final_g.reshape(1, d)]
    y, st = pl.pallas_call(
        functools.partial(_ffn_kernel, tm=tm, Lb=Lb, prompt=prompt, blocks_per_seq=bps),
        grid=(n // tm,),
        in_specs=[row_spec] + extra_specs + [_resident_spec(a) for a in consts],
        out_specs=[row_spec, s_spec],
        out_shape=[jax.ShapeDtypeStruct((n, d), F32), jax.ShapeDtypeStruct((nstate, 2, 2 * D_FF), F32)],
        scratch_shapes=scratch,
        compiler_params=_cparams("arbitrary"),
        name="convffn",
    )(x, *extra_in, *consts)
    if prompt:
        st = st[bps - 1::bps]
    return y, st


def _post_mix_kernel(x_ref, ys_ref, yr_ref, k_ref, v_ref, wo_ref, gxa_ref, wq_ref, wao_ref,
                     g_ref, wu_ref, cw_ref, cb_ref, wd_ref, fg_ref, y_ref, s_ref, carry_sc,
                     *, tm, seq_len):
    wo = wo_ref[...]
    x1 = x_ref[...] + (jnp.dot(ys_ref[...], wo[0:SSD_WIDTH], preferred_element_type=F32)
                       + jnp.dot(yr_ref[...], wo[SSD_WIDTH:], preferred_element_type=F32))
    q = (_bdot(_rms(x1, gxa_ref[...]), wq_ref[...]) * (XA_HEAD_DIM ** -0.5)).astype(BF16)
    outs = []
    for h in range(XA_HEADS):
        sl = slice(h * XA_HEAD_DIM, (h + 1) * XA_HEAD_DIM)
        s = lax.dot_general(q[:, sl], k_ref[0, :, sl].astype(BF16), _NT, preferred_element_type=F32)
        s = s - jnp.max(s, axis=-1, keepdims=True)
        e = jnp.exp(s)
        pr = e / jnp.sum(e, axis=-1, keepdims=True)
        outs.append(_bdot(pr, v_ref[0, :, sl]))
    x2 = x1 + _bdot(jnp.concatenate(outs, axis=1), wao_ref[...])
    _ffn_body(x2, None, g_ref, wu_ref, cw_ref, cb_ref, wd_ref, fg_ref, y_ref, s_ref, carry_sc,
              tm=tm, Lb=seq_len, prompt=True, blocks_per_seq=seq_len // tm)


def _post_mix(x, y_ssd, y_rw, mk, mv, w_out, g_xa, w_q, w_o, p, final_g, tm, seq_len):
    n, d = x.shape
    bps = seq_len // tm
    row = lambda w: pl.BlockSpec((tm, w), lambda i: (i, 0))
    kv_spec = pl.BlockSpec((1, N_MEM, d), lambda i: (i // bps, 0, 0))
    consts = [w_out, g_xa.reshape(1, d), w_q, w_o, p["norm_ffn_w"].reshape(1, d), p["ffn_w_up"].astype(BF16),
              p["ffn_conv_w"], p["ffn_conv_b"].reshape(1, 2 * D_FF), p["ffn_w_down"].astype(BF16),
              final_g.reshape(1, d)]
    y, st = pl.pallas_call(
        functools.partial(_post_mix_kernel, tm=tm, seq_len=seq_len),
        grid=(n // tm,),
        in_specs=[row(d), row(SSD_WIDTH), row(RWKV_WIDTH), kv_spec, kv_spec] + [_resident_spec(a) for a in consts],
        out_specs=[row(d), pl.BlockSpec((1, 2, 2 * D_FF), lambda i: (i, 0, 0))],
        out_shape=[jax.ShapeDtypeStruct((n, d), F32), jax.ShapeDtypeStruct((n // tm, 2, 2 * D_FF), F32)],
        scratch_shapes=[pltpu.VMEM((8, 2 * D_FF), F32)],
        compiler_params=_cparams("arbitrary"),
        name="post_mix",
    )(x, y_ssd, y_rw, mk, mv, *consts)
    return y, st[bps - 1::bps]


def _pair_blocks(sbd):
    b = sbd.shape[0]
    s0 = sbd[:, :, 0:64, 0:64]
    s1 = sbd[:, :, 64:128, 64:128]
    return jnp.stack([s0, s1], axis=2).reshape(b, 8, 64, 64)


def _hist_rows(state, seq_len, k):
    b, w, c = state.shape
    cols = [state[:, w + t - k] if t < k else jnp.zeros((b, c), state.dtype) for t in range(seq_len)]
    return jnp.stack(cols, axis=1).reshape(b * seq_len, c)


def kernel(x_prompt, x_sample, mem_prompt, state_ssm_conv, state_ssm, state_shift, state_wkv, state_ffn_conv, cache_mem_k, cache_mem_v, norm_mix_w, w_in, ssd_conv_w, ssd_conv_b, ssd_dt_bias, ssd_a_log, ssd_d, ssd_norm_w, rwkv_mu, rwkv_w0, rwkv_w2, rwkv_a0, rwkv_a2, rwkv_g2, rwkv_k_k, rwkv_k_a, rwkv_r_k, rwkv_ln_w, rwkv_ln_b, w_out, norm_xa_w, mem_norm_w, xa_w_q, xa_w_k, xa_w_v, xa_w_o, norm_ffn_w, ffn_w_up, ffn_conv_w, ffn_conv_b, ffn_w_down, final_norm_w):
    depth = w_in.shape[0]
    assert depth == 1, "final rmsnorm is fused into the (single) layer's ConvFFN kernel"
    bp, lp, d = x_prompt.shape
    bs, ls, _ = x_sample.shape
    i = 0
    p = dict(ssd_conv_w=ssd_conv_w[i], ssd_conv_b=ssd_conv_b[i], ssd_dt_bias=ssd_dt_bias[i],
             ssd_a_log=ssd_a_log[i], ssd_d=ssd_d[i], ssd_norm_w=ssd_norm_w[i], rwkv_mu=rwkv_mu[i],
             rwkv_w0=rwkv_w0[i], rwkv_w2=rwkv_w2[i], rwkv_a0=rwkv_a0[i], rwkv_a2=rwkv_a2[i],
             rwkv_g2=rwkv_g2[i], rwkv_k_k=rwkv_k_k[i], rwkv_k_a=rwkv_k_a[i],
             rwkv_r_k=rwkv_r_k[i].reshape(-1), rwkv_ln_w=rwkv_ln_w[i], rwkv_ln_b=rwkv_ln_b[i],
             norm_ffn_w=norm_ffn_w[i], ffn_w_up=ffn_w_up[i], ffn_conv_w=ffn_conv_w[i],
             ffn_conv_b=ffn_conv_b[i], ffn_w_down=ffn_w_down[i])

    w_in_p = jnp.concatenate([w_in[i][:, :SSD_PROJ], jnp.zeros((d, SEG - SSD_PROJ), F32),
                              w_in[i][:, SSD_PROJ:]], axis=1).astype(BF16)
    w_out_b = w_out[i].astype(BF16)
    w_q_b = (xa_w_q[i]).astype(BF16)
    w_o_b = xa_w_o[i].astype(BF16)
    w_kv_b = jnp.concatenate([xa_w_k[i], xa_w_v[i]], axis=1).astype(BF16)

    xp = x_prompt.reshape(bp * lp, d)
    xs = x_sample.reshape(bs * ls, d)

    mk, mv = _mem_kv(mem_prompt.reshape(bp * N_MEM, d), mem_norm_w[i], w_kv_b, 512)
    mk = mk.reshape(bp, N_MEM, d)
    mv = mv.reshape(bp, N_MEM, d)
    u_p = _norm_proj(xp, norm_mix_w[i], w_in_p, 1024, SEG)
    y_ssd_p, ssm_p = _ssd_prompt(u_p, p, bp, lp)
    y_rw_p, wkv_bd_p = _wkv_prompt(u_p, p, bp, lp)
    y_p, ffn_conv_p = _post_mix(xp, y_ssd_p, y_rw_p, mk, mv, w_out_b, norm_xa_w[i], w_q_b, w_o_b, p,
                                final_norm_w, 512, lp)
    u3 = u_p.reshape(bp, lp, 2 * SEG)
    ssm_conv_p = u3[:, lp - 3:, SSD_WIDTH:SSD_WIDTH + SSD_XBC]
    shift_p = u3[:, lp - 1, SEG:]

    u_s = _norm_proj(xs, norm_mix_w[i], w_in_p, 512, SEG)
    conv_state = state_ssm_conv[i]
    hist_conv = jnp.stack([_hist_rows(conv_state, ls, k) for k in (1, 2, 3)], axis=0)
    y_ssd_s, ssm_s = _ssd_sample(u_s, hist_conv, state_ssm[i].reshape(bs, 4, LANES, LANES), p, ls)
    hist_shift = _hist_rows(state_shift[i][:, None, :], ls, 1)
    y_rw_s, wkv_s = _wkv_sample(u_s, hist_shift, state_wkv[i], p, ls)
    x1_s, q_s = _outproj(xs, y_ssd_s, y_rw_s, w_out_b, norm_xa_w[i], w_q_b, 512)
    x2_s = _xattn_cache(q_s, x1_s, cache_mem_k[i], cache_mem_v[i], w_o_b, 16, ls)
    y_s, ffn_conv_s = _ffn(x2_s, jnp.swapaxes(state_ffn_conv[i], 0, 1), p, final_norm_w, 256, ls, False, ls)
    xbc_rows = [u_s[t::ls, SSD_WIDTH:SSD_WIDTH + SSD_XBC] for t in range(ls)]
    ssm_conv_s = jnp.stack(([conv_state[:, j] for j in range(conv_state.shape[1])] + xbc_rows)[-3:], axis=1)
    shift_s = u_s[ls - 1::ls, SEG:]

    e = lambda a: a[None]
    return (y_p.reshape(bp, lp, d), y_s.reshape(bs, ls, d),
            e(ssm_conv_p), e(ssm_conv_s),
            e(ssm_p.reshape(bp, SSD_HEADS, 64, SSD_STATE)), e(ssm_s.reshape(bs, SSD_HEADS, 64, SSD_STATE)),
            e(shift_p), e(shift_s),
            e(_pair_blocks(wkv_bd_p)), e(wkv_s),
            e(ffn_conv_p), e(ffn_conv_s),
            e(mk.reshape(bp, N_MEM, XA_HEADS, XA_HEAD_DIM)), e(mv.reshape(bp, N_MEM, XA_HEADS, XA_HEAD_DIM)))
```

```python
import functools
import math

import jax
import jax.numpy as jnp
from jax import lax
from jax.experimental import pallas as pl
from jax.experimental.pallas import tpu as pltpu

F32 = jnp.float32
BF16 = jnp.bfloat16

D_MODEL = 1024
N_MEM = 256
XA_HEADS = 4
XA_HEAD_DIM = D_MODEL // XA_HEADS
SSD_WIDTH = 512
SSD_HEADS = 8
SSD_STATE = 128
SSD_XBC = 1024
SSD_PROJ = SSD_WIDTH + SSD_XBC + SSD_HEADS
RWKV_WIDTH = 512
RWKV_PROJ = 1792
D_FF = 2816
EPS = 1e-6
GN_EPS = 64e-5

LANES = 128
SEG = 1792
DT_OFF = SSD_WIDTH + SSD_XBC
VMEM_LIMIT_BYTES = 56 * 1024 * 1024


def _cparams(*sem):
    return pltpu.CompilerParams(dimension_semantics=sem, vmem_limit_bytes=VMEM_LIMIT_BYTES)


_NN = (((1,), (0,)), ((), ()))
_NT = (((1,), (1,)), ((), ()))
_TN = (((0,), (0,)), ((), ()))


def _bdot(a, b, dims=_NN):
    return lax.dot_general(a.astype(BF16), b.astype(BF16), dims, preferred_element_type=F32)


def _split2(x):
    hi = x.astype(BF16)
    lo = (x - hi.astype(F32)).astype(BF16)
    return hi, lo


def _maskdot(m_bf16, x):
    hi = x.astype(BF16)
    r1 = x - hi.astype(F32)
    mid = r1.astype(BF16)
    lo = (r1 - mid.astype(F32)).astype(BF16)
    return jnp.dot(jnp.concatenate([m_bf16, m_bf16, m_bf16], axis=1),
                   jnp.concatenate([hi, mid, lo], axis=0), preferred_element_type=F32)


def _sigmoid(x):
    return 0.5 + 0.5 * jnp.tanh(0.5 * x)


def _softplus(x):
    return jnp.maximum(x, 0.0) + jnp.log1p(jnp.exp(-jnp.abs(x)))


def _rms(x, g):
    return x * lax.rsqrt(jnp.mean(x * x, axis=-1, keepdims=True) + EPS) * g


def _iota2(shape, dim):
    return lax.broadcasted_iota(jnp.int32, shape, dim)


def _shifted(x, k, tpos, hist):
    return jnp.where(tpos >= k, pltpu.roll(x, k, 0), hist)


def _norm_proj_kernel(x_ref, g_ref, w_ref, o_ref, dt_ref, tail_ref, hn_sc):
    j = pl.program_id(1)

    @pl.when(j == 0)
    def _():
        hn_sc[...] = _rms(x_ref[...], g_ref[...]).astype(BF16)

    acc = jnp.dot(hn_sc[...], w_ref[...], preferred_element_type=F32)
    o_ref[...] = acc.astype(o_ref.dtype)
    tail_ref[0] = acc[acc.shape[0] - 8:]

    @pl.when(j == 0)
    def _():
        dt_ref[...] = acc[:, DT_OFF:DT_OFF + LANES]


def _norm_proj(x, g, w_bf16, tm, out_dtype):
    n, d = x.shape
    return pl.pallas_call(
        _norm_proj_kernel,
        grid=(n // tm, 2),
        in_specs=[pl.BlockSpec((tm, d), lambda i, j: (i, 0)),
                  pl.BlockSpec((1, d), lambda i, j: (0, 0)),
                  pl.BlockSpec((d, SEG), lambda i, j: (0, j))],
        out_specs=[pl.BlockSpec((tm, SEG), lambda i, j: (i, j)),
                   pl.BlockSpec((tm, LANES), lambda i, j: (i, 0)),
                   pl.BlockSpec((1, 8, SEG), lambda i, j: (i, 0, j))],
        out_shape=[jax.ShapeDtypeStruct((n, 2 * SEG), out_dtype),
                   jax.ShapeDtypeStruct((n, LANES), F32),
                   jax.ShapeDtypeStruct((n // tm, 8, 2 * SEG), F32)],
        scratch_shapes=[pltpu.VMEM((tm, d), BF16)],
        compiler_params=_cparams("parallel", "arbitrary"),
        name="norm_proj",
    )(x, g.reshape(1, d), w_bf16)


def _mem_kv_kernel(x_ref, g_ref, w_ref, k_ref, v_ref):
    hn = _rms(x_ref[...], g_ref[...]).astype(BF16)
    d = k_ref.shape[1]
    k_ref[...] = jnp.dot(hn, w_ref[:, 0:d], preferred_element_type=F32)
    v_ref[...] = jnp.dot(hn, w_ref[:, d:2 * d], preferred_element_type=F32)


def _mem_kv(x, g, w_kv_bf16, tm):
    n, d = x.shape
    row_spec = pl.BlockSpec((tm, d), lambda i: (i, 0))
    return pl.pallas_call(
        _mem_kv_kernel,
        grid=(n // tm,),
        in_specs=[row_spec, pl.BlockSpec((1, d), lambda i: (0, 0)), _const_spec(w_kv_bf16)],
        out_specs=[row_spec, row_spec],
        out_shape=[jax.ShapeDtypeStruct((n, d), F32)] * 2,
        compiler_params=_cparams("parallel"),
        name="mem_kv",
    )(x, g.reshape(1, d), w_kv_bf16)


def _ssd_kernel(*refs, Q, Lb, prompt, G=1):
    if not prompt:
        return _ssd_block(*refs, Q=Q, Lb=Lb, prompt=False)
    (u_ref, dt_ref, cw_ref, cb_ref, dtb_ref, an_ref, dsk_ref, nw_ref, y_ref, hout_ref, carry_sc, h_sc) = refs

    @pl.when(pl.program_id(1) == 0)
    def _():
        carry_sc[...] = jnp.zeros_like(carry_sc)
        h_sc[...] = jnp.zeros_like(h_sc)

    for gi in range(G):
        _ssd_block(u_ref.at[gi], dt_ref.at[gi], cw_ref, cb_ref, dtb_ref, an_ref, dsk_ref, nw_ref, y_ref.at[gi],
                   hout_ref.at[pl.ds(gi, 1)], carry_sc.at[gi], h_sc.at[pl.ds(4 * gi, 4)],
                   Q=Q, Lb=Lb, prompt=True)


def _ssd_block(*refs, Q, Lb, prompt):
    ns = Q // Lb
    lb = int(math.log2(Lb))
    if prompt:
        (u_ref, dt_ref, cw_ref, cb_ref, dtb_ref, an_ref, dsk_ref, nw_ref,
         y_ref, hout_ref, carry_sc, h_sc) = refs
    else:
        (u_ref, dt_ref, hist_ref, hin_ref, cw_ref, cb_ref, dtb_ref, an_ref, dsk_ref, nw_ref,
         y_ref, hout_ref) = refs

    u = u_ref[...].astype(F32)
    z = u[:, 0:SSD_WIDTH]
    x = u[:, SSD_WIDTH:SSD_WIDTH + SSD_XBC]
    dtr = dt_ref[...]

    rows = _iota2((Q, 1), 0)
    tpos = rows & (Lb - 1)
    if prompt:
        c8 = carry_sc[...]
        h1 = c8[7:8]
        h2 = jnp.where(rows == 0, c8[6:7], c8[7:8])
        h3 = jnp.where(rows == 0, c8[5:6], jnp.where(rows == 1, c8[6:7], c8[7:8]))
        carry_sc[...] = x[Q - 8:Q]
    else:
        h1, h2, h3 = hist_ref[0], hist_ref[1], hist_ref[2]
    cw = cw_ref[...]
    xc = (cw[0:1] * _shifted(x, 3, tpos, h3) + cw[1:2] * _shifted(x, 2, tpos, h2)
          + cw[2:3] * _shifted(x, 1, tpos, h1) + cw[3:4] * x) + cb_ref[...]
    xc = xc * _sigmoid(xc)
    xs = xc[:, 0:SSD_WIDTH]
    bm = xc[:, SSD_WIDTH:SSD_WIDTH + 2 * SSD_STATE]
    cm = xc[:, SSD_WIDTH + 2 * SSD_STATE:]

    dt = _softplus(dtr + dtb_ref[...])
    da = dt * an_ref[...]

    ri = _iota2((Q, Q), 0)
    ci = _iota2((Q, Q), 1)
    same = (ri >> lb) == (ci >> lb)
    tril = same & (ci <= ri)
    sel = ci == (((ri >> lb) << lb) + (Lb - 1))
    acs = _maskdot(tril.astype(BF16), da)
    acs_t = acs.T
    acs_last = _maskdot(sel.astype(BF16), acs)
    dec_end = jnp.exp(acs_last - acs)
    eacs = jnp.exp(acs)
    seqid = rows >> lb

    lane = _iota2((Q, LANES), 1)
    lo_half = lane < 64
    prow = _iota2((LANES, LANES), 0)
    dsk = dsk_ref[...]

    ys = []
    for q in range(4):
        g = q // 2
        h0, h1i = 2 * q, 2 * q + 1
        if q % 2 == 0:
            cg = cm[:, g * SSD_STATE:(g + 1) * SSD_STATE]
            bg = bm[:, g * SSD_STATE:(g + 1) * SSD_STATE]
            cb_g = _bdot(cg, bg, _NT)
            if ns == 1:
                cexp, bexp = cg.astype(BF16), bg.astype(BF16)
            else:
                cexp = jnp.concatenate([jnp.where(seqid == b, cg, 0.0).astype(BF16) for b in range(ns)], axis=1)
                bexp = jnp.concatenate([jnp.where(seqid == b, bg, 0.0).astype(BF16) for b in range(ns)], axis=1)
        m0 = jnp.where(tril, cb_g * jnp.exp(acs[:, h0:h0 + 1] - acs_t[h0:h0 + 1, :]), 0.0)
        m1 = jnp.where(tril, cb_g * jnp.exp(acs[:, h1i:h1i + 1] - acs_t[h1i:h1i + 1, :]), 0.0)
        xp = xs[:, q * LANES:(q + 1) * LANES]
        xdt = xp * jnp.where(lo_half, dt[:, h0:h0 + 1], dt[:, h1i:h1i + 1])
        xdt0 = jnp.where(lo_half, xdt, 0.0)
        xdt1 = jnp.where(lo_half, 0.0, xdt)
        ydiag = _bdot(jnp.concatenate([m0, m1], axis=1), jnp.concatenate([xdt0, xdt1], axis=0))
        if prompt:
            hst = h_sc[q]
        else:
            hst = jnp.concatenate([hin_ref[b, q] for b in range(ns)], axis=1)
        ecs = jnp.where(lo_half, eacs[:, h0:h0 + 1], eacs[:, h1i:h1i + 1])
        yoff = _bdot(cexp, hst, _NT) * ecs
        xd = xdt * jnp.where(lo_half, dec_end[:, h0:h0 + 1], dec_end[:, h1i:h1i + 1])
        incr = _bdot(xd, bexp, _TN)
        scales = []
        for b in range(ns):
            r = b * Lb + Lb - 1
            e = eacs[r:r + 1, :]
            scales.append(jnp.where(prow < 64, e[:, h0:h0 + 1], e[:, h1i:h1i + 1]))
        scale = scales[0] if ns == 1 else jnp.concatenate(scales, axis=1)
        hnew = hst * scale + incr
        if prompt:
            h_sc[q] = hnew
            hout_ref[0, q] = hnew
        else:
            for b in range(ns):
                hout_ref[b, q] = hnew[:, b * LANES:(b + 1) * LANES]
        ys.append(ydiag + yoff + dsk[:, q * LANES:(q + 1) * LANES] * xp)

    y = jnp.concatenate(ys, axis=1)
    y = y * (z * _sigmoid(z))
    half = SSD_WIDTH // 2
    outs = []
    for g in range(2):
        yg = y[:, g * half:(g + 1) * half]
        outs.append(yg * lax.rsqrt(jnp.mean(yg * yg, axis=-1, keepdims=True) + EPS))
    y_ref[...] = (jnp.concatenate(outs, axis=1) * nw_ref[...]).astype(y_ref.dtype)


def _ssd_params(p):
    an = jnp.zeros((1, LANES), F32).at[0, :SSD_HEADS].set(-jnp.exp(p["ssd_a_log"]))
    dtb = jnp.zeros((1, LANES), F32).at[0, :SSD_HEADS].set(p["ssd_dt_bias"])
    dsk = jnp.repeat(p["ssd_d"], SSD_WIDTH // SSD_HEADS).reshape(1, SSD_WIDTH)
    return [p["ssd_conv_w"], p["ssd_conv_b"].reshape(1, SSD_XBC), dtb, an, dsk,
            p["ssd_norm_w"].reshape(1, SSD_WIDTH)]


def _const_spec(a):
    nd = a.ndim
    return pl.BlockSpec(a.shape, lambda *_: (0,) * nd)


def _ssd_prompt(u, dt, p, batch, seq, Q=128, G=4):
    nc = seq // Q
    params = _ssd_params(p)
    y, hout = pl.pallas_call(
        functools.partial(_ssd_kernel, Q=Q, Lb=Q, prompt=True, G=G),
        grid=(batch // G, nc),
        in_specs=[pl.BlockSpec((G, Q, SEG), lambda b, c: (b, c, 0)),
                  pl.BlockSpec((G, Q, LANES), lambda b, c: (b, c, 0))] + [_const_spec(a) for a in params],
        out_specs=[pl.BlockSpec((G, Q, SSD_WIDTH), lambda b, c: (b, c, 0)),
                   pl.BlockSpec((G, 4, LANES, LANES), lambda b, c: (b, 0, 0, 0))],
        out_shape=[jax.ShapeDtypeStruct((batch, seq, SSD_WIDTH), BF16),
                   jax.ShapeDtypeStruct((batch, 4, LANES, LANES), F32)],
        scratch_shapes=[pltpu.VMEM((G, 8, SSD_XBC), F32), pltpu.VMEM((4 * G, LANES, LANES), F32)],
        compiler_params=_cparams("parallel", "arbitrary"),
        name="ssd_prompt",
    )(u.reshape(batch, seq, 2 * SEG), dt.reshape(batch, seq, LANES), *params)
    return y.reshape(batch * seq, SSD_WIDTH), hout


def _ssd_sample(u, dt, hist, hin, p, Lb, Q=64):
    n = u.shape[0]
    ns = Q // Lb
    params = _ssd_params(p)
    y, hout = pl.pallas_call(
        functools.partial(_ssd_kernel, Q=Q, Lb=Lb, prompt=False),
        grid=(n // Q,),
        in_specs=[pl.BlockSpec((Q, SEG), lambda i: (i, 0)),
                  pl.BlockSpec((Q, LANES), lambda i: (i, 0)),
                  pl.BlockSpec((3, Q, SSD_XBC), lambda i: (0, i, 0)),
                  pl.BlockSpec((ns, 4, LANES, LANES), lambda i: (i, 0, 0, 0))]
                 + [_const_spec(a) for a in params],
        out_specs=[pl.BlockSpec((Q, SSD_WIDTH), lambda i: (i, 0)),
                   pl.BlockSpec((ns, 4, LANES, LANES), lambda i: (i, 0, 0, 0))],
        out_shape=[jax.ShapeDtypeStruct((n, SSD_WIDTH), BF16),
                   jax.ShapeDtypeStruct(hin.shape, F32)],
        compiler_params=_cparams("parallel"),
        name="ssd_sample",
    )(u, dt, hist, hin, *params)
    return y, hout


def _dot3s(a, b, dims=_NN):
    ka = dims[0][0][0]
    kb = dims[0][1][0]
    lhs = jnp.concatenate([a[0], a[0], a[1]], axis=ka)
    rhs = jnp.concatenate([b[0], b[1], b[0]], axis=kb)
    return lax.dot_general(lhs, rhs, dims, preferred_element_type=F32)


def _cat2(parts, axis):
    return (jnp.concatenate([p[0] for p in parts], axis=axis), jnp.concatenate([p[1] for p in parts], axis=axis))


def _tri_inverse(mats, lb):
    n = mats[0].shape[0]
    ri = _iota2((n, n), 0)
    ci = _iota2((n, n), 1)
    off1 = ((ri >> 1) == (ci >> 1)) & ((ri & 1) == 1) & ((ci & 1) == 0)
    eye = jnp.where(ri == ci, 1.0, 0.0)
    ts = [eye + jnp.where(off1, a, 0.0) for a in mats]
    for lvl in range(1, lb):
        m = 1 << lvl
        off = ((ri >> (lvl + 1)) == (ci >> (lvl + 1))) & ((ri & (2 * m - 1)) >= m) & ((ci & (2 * m - 1)) < m)
        tsb = [t.astype(BF16) for t in ts]
        ws = [jnp.dot(jnp.where(off, a, 0.0).astype(BF16), tb, preferred_element_type=F32)
              for a, tb in zip(mats, tsb)]
        ts = [t + jnp.dot(tb, w.astype(BF16), preferred_element_type=F32) for t, tb, w in zip(ts, tsb, ws)]
    return [t.astype(BF16) for t in ts]


def _refined_solve(tinvs, mats, rhss):
    d = functools.partial(jnp.dot, preferred_element_type=F32)
    n = range(len(mats))
    rs = [_split2(r) for r in rhss]
    u0 = [d(jnp.concatenate([tinvs[i], tinvs[i]], axis=1), jnp.concatenate(rs[i], axis=0)) for i in n]
    au = [_dot3s(_split2(mats[i]), _split2(u0[i])) for i in n]
    res = [((rhss[i] - u0[i]) + au[i]).astype(BF16) for i in n]
    return [u0[i] + d(tinvs[i], res[i]) for i in n]


def _wkv_kernel(*refs, C, Lb, prompt, G):
    nb = C // Lb
    lb = int(math.log2(Lb))
    R = 2 * C
    T = G * C
    if prompt:
        (u_ref, mu_ref, w0_ref, w2_ref, a0_ref, a2_ref, g2_ref, kk_ref, ka_ref, rk_ref, lnw_ref, lnb_ref,
         y_ref, sout_ref, carry_sc, s_sc) = refs
        c = pl.program_id(1)

        @pl.when(c == 0)
        def _():
            carry_sc[...] = jnp.zeros_like(carry_sc)
            s_sc[...] = jnp.zeros_like(s_sc)
    else:
        (u_ref, hist_ref, sin_ref, mu_ref, w0_ref, w2_ref, a0_ref, a2_ref, g2_ref, kk_ref, ka_ref, rk_ref,
         lnw_ref, lnb_ref, y_ref, sout_ref) = refs

    rows = _iota2((T, 1), 0)
    tpos = rows & (Lb - 1)
    if prompt:
        u = u_ref[...].astype(F32).reshape(T, SEG)
        hist = jnp.concatenate([jnp.broadcast_to(carry_sc[g, 7:8], (C, SEG)) for g in range(G)], axis=0)
        for g in range(G):
            carry_sc[g] = u[g * C + C - 8:(g + 1) * C]
    else:
        u = u_ref[...].astype(F32)
        hist = hist_ref[...]
    um = u + (_shifted(u, 1, tpos, hist) - u) * mu_ref[...]

    W = RWKV_WIDTH
    r = um[:, 0:W]
    k = um[:, W:2 * W]
    v = um[:, 2 * W:3 * W]
    t12 = um[:, 3 * W:3 * W + LANES]
    lg = um[:, 3 * W + LANES:3 * W + 2 * LANES]

    wl = w0_ref[...] + _bdot(jnp.tanh(t12), w2_ref[...])
    logw = -math.exp(-0.5) * _sigmoid(wl)
    a = _sigmoid(a0_ref[...] + _bdot(t12, a2_ref[...]))
    out_gate = _bdot(_sigmoid(lg), g2_ref[...])

    hi = _iota2((W, W), 0)
    hj = _iota2((W, W), 1)
    headsum = ((hi >> 6) == (hj >> 6)).astype(BF16)

    kk = k * kk_ref[...]
    kk = kk * lax.rsqrt(jnp.maximum(_bdot(kk * kk, headsum), 1e-24))
    kmod = k * (1.0 + (a - 1.0) * ka_ref[...])
    beta = kk * a

    ri = _iota2((T, T), 0)
    ci = _iota2((T, T), 1)
    same_c = (ri >> lb) == (ci >> lb)
    lc = _maskdot((same_c & (ci <= ri)).astype(BF16), logw)
    e_pos = jnp.exp(lc)
    e_neg = jnp.exp(-lc)
    at = -kk * jnp.exp(lc - logw)
    rt = r * e_pos
    bt = beta * e_neg
    kt = kmod * e_neg

    si = _iota2((R, R), 0)
    sj = _iota2((R, R), 1)
    same_s = (si >> lb) == (sj >> lb)
    strict = same_s & (sj < si)
    incl = same_s & (sj <= si)
    incl2 = jnp.concatenate([incl, incl], axis=1)
    lane_row = _iota2((1, LANES), 1)
    m_lo = jnp.where(lane_row < 64, 1.0, 0.0).astype(BF16)
    m_hi = jnp.where(lane_row < 64, 0.0, 1.0).astype(BF16)
    if nb > 1:
        seq_s = (_iota2((R, LANES), 0) & (C - 1)) >> lb
        seq_masks = [jnp.where(seq_s == b, 1.0, 0.0).astype(BF16) for b in range(nb)]

    def stack(xp):
        return jnp.concatenate([xp * m_lo, xp * m_hi], axis=0)

    def expand(xs_):
        if nb == 1:
            return xs_
        return jnp.concatenate([xs_ * seq_masks[b] for b in range(nb)], axis=1)

    probs = [(g, p) for g in range(G) for p in range(4)]
    NP = range(len(probs))

    def tile(x, i):
        g, p = probs[i]
        return x[g * C:(g + 1) * C, p * LANES:(p + 1) * LANES]

    def stacked(pair, i):
        return (stack(tile(pair[0], i)), stack(tile(pair[1], i)))

    at2, v2, bt2, kt2 = (_split2(x) for x in (at, v, bt, kt))
    rt_b = rt.astype(BF16)
    a_s = [stacked(at2, i) for i in NP]
    r_s = [stack(tile(rt_b, i)) for i in NP]
    v_s = [stacked(v2, i) for i in NP]
    bk_s = [_cat2([stacked(bt2, i), stacked(kt2, i)], 0) for i in NP]
    gm_a = [_dot3s(a_s[i], bk_s[i], _NT) for i in NP]
    gm_r = [lax.dot_general(r_s[i], bk_s[i][0], _NT, preferred_element_type=F32) for i in NP]
    a_ab = [jnp.where(strict, gm_a[i][:, 0:R], 0.0) for i in NP]
    a_ak = [_split2(jnp.where(strict, gm_a[i][:, R:2 * R], 0.0)) for i in NP]
    a_r = [jnp.where(incl2, gm_r[i], 0.0).astype(BF16) for i in NP]
    akv = [_dot3s(a_ak[i], v_s[i]) for i in NP]
    tinv = _tri_inverse(a_ab, lb)

    if prompt:
        sst = [s_sc[i] for i in NP]
    else:
        z64 = jnp.zeros((64, 64), F32)

        def pair_blockdiag(b, p):
            top = jnp.concatenate([sin_ref[b, 2 * p], z64], axis=1)
            bot = jnp.concatenate([z64, sin_ref[b, 2 * p + 1]], axis=1)
            return jnp.concatenate([top, bot], axis=0)

        sst = [jnp.concatenate([pair_blockdiag(g * nb + b, p) for b in range(nb)], axis=1) for g, p in probs]
    ss = [_split2(s) for s in sst]
    ar0_a = [_dot3s((expand(a_s[i][0]), expand(a_s[i][1])), ss[i], _NT) for i in NP]
    ar0_r = [lax.dot_general(expand(r_s[i]), ss[i][0], _NT, preferred_element_type=F32) for i in NP]
    us = [_split2(x) for x in _refined_solve(tinv, a_ab, [ar0_a[i] + akv[i] for i in NP])]
    uv = [_cat2([us[i], v_s[i]], 0) for i in NP]
    yst = [ar0_r[i] + jnp.dot(a_r[i], uv[i][0], preferred_element_type=F32) for i in NP]
    ys = [yst[i][0:C] + yst[i][C:R] for i in NP]
    for i in NP:
        g, p = probs[i]
        pl_lanes = tile(e_pos, i)
        plast = jnp.concatenate([pl_lanes[b * Lb + Lb - 1:b * Lb + Lb] for b in range(nb)], axis=1)
        bk_x = tuple(jnp.concatenate([expand(x[0:R]), expand(x[R:2 * R])], axis=0) for x in bk_s[i])
        snew = (sst[i] + _dot3s(uv[i], bk_x, _TN)) * plast
        if prompt:
            s_sc[i] = snew
            sout_ref[g, p] = snew
        else:
            for b in range(nb):
                sout_ref[g * nb + b, 2 * p] = snew[0:64, b * LANES:b * LANES + 64]
                sout_ref[g * nb + b, 2 * p + 1] = snew[64:128, b * LANES + 64:(b + 1) * LANES]

    y = jnp.concatenate([jnp.concatenate(ys[4 * g:4 * g + 4], axis=1) for g in range(G)], axis=0)
    inv_d = 1.0 / 64.0
    mean = _bdot(y, headsum) * inv_d
    yc = y - mean
    var = _bdot(yc * yc, headsum) * inv_d
    yn = yc * lax.rsqrt(var + GN_EPS) * lnw_ref[...] + lnb_ref[...]
    yn = yn + _bdot(r * kmod * rk_ref[...], headsum) * v
    y_ref[...] = (yn * out_gate).astype(y_ref.dtype).reshape(y_ref.shape)


def _wkv_params(p):
    z64 = jnp.zeros((64, RWKV_WIDTH), F32)
    w2 = jnp.concatenate([p["rwkv_w2"], z64], axis=0).astype(BF16)
    a2 = jnp.concatenate([z64, p["rwkv_a2"]], axis=0).astype(BF16)
    row = lambda a: a.reshape(1, -1)
    return [row(p["rwkv_mu"]), row(p["rwkv_w0"]), w2, row(p["rwkv_a0"]), a2, p["rwkv_g2"].astype(BF16),
            row(p["rwkv_k_k"]), row(p["rwkv_k_a"]), row(p["rwkv_r_k"]), row(p["rwkv_ln_w"]), row(p["rwkv_ln_b"])]


def _wkv_prompt(u, p, batch, seq, C=64, G=4):
    nc = seq // C
    params = _wkv_params(p)
    y, sout = pl.pallas_call(
        functools.partial(_wkv_kernel, C=C, Lb=C, prompt=True, G=G),
        grid=(batch // G, nc),
        in_specs=[pl.BlockSpec((G, C, SEG), lambda b, c: (b, c, 1))] + [_const_spec(a) for a in params],
        out_specs=[pl.BlockSpec((G, C, RWKV_WIDTH), lambda b, c: (b, c, 0)),
                   pl.BlockSpec((G, 4, LANES, LANES), lambda b, c: (b, 0, 0, 0))],
        out_shape=[jax.ShapeDtypeStruct((batch, seq, RWKV_WIDTH), BF16),
                   jax.ShapeDtypeStruct((batch, 4, LANES, LANES), F32)],
        scratch_shapes=[pltpu.VMEM((G, 8, SEG), F32), pltpu.VMEM((4 * G, LANES, LANES), F32)],
        compiler_params=_cparams("parallel", "arbitrary"),
        name="wkv_prompt",
    )(u.reshape(batch, seq, 2 * SEG), *params)
    return y.reshape(batch * seq, RWKV_WIDTH), sout


def _wkv_sample(u, hist, sin, p, Lb, C=64, G=2):
    n = u.shape[0]
    nb = C // Lb
    T = G * C
    params = _wkv_params(p)
    return pl.pallas_call(
        functools.partial(_wkv_kernel, C=C, Lb=Lb, prompt=False, G=G),
        grid=(n // T,),
        in_specs=[pl.BlockSpec((T, SEG), lambda i: (i, 1)),
                  pl.BlockSpec((T, SEG), lambda i: (i, 0)),
                  pl.BlockSpec((G * nb, 8, 64, 64), lambda i: (i, 0, 0, 0))]
                 + [_const_spec(a) for a in params],
        out_specs=[pl.BlockSpec((T, RWKV_WIDTH), lambda i: (i, 0)),
                   pl.BlockSpec((G * nb, 8, 64, 64), lambda i: (i, 0, 0, 0))],
        out_shape=[jax.ShapeDtypeStruct((n, RWKV_WIDTH), BF16),
                   jax.ShapeDtypeStruct(sin.shape, F32)],
        compiler_params=_cparams("parallel"),
        name="wkv_sample",
    )(u, hist, sin, *params)


def _outproj_kernel(x_ref, ys_ref, yr_ref, wo_ref, g_ref, wq_ref, x1_ref, q_ref):
    wo = wo_ref[...]
    x1 = x_ref[...] + (_bdot(ys_ref[...], wo[0:SSD_WIDTH]) + _bdot(yr_ref[...], wo[SSD_WIDTH:]))
    x1_ref[...] = x1
    q_ref[...] = _bdot(_rms(x1, g_ref[...]), wq_ref[...])


def _outproj(x, y_ssd, y_rw, w_out, g, w_q, tm):
    n, d = x.shape
    return pl.pallas_call(
        _outproj_kernel,
        grid=(n // tm,),
        in_specs=[pl.BlockSpec((tm, d), lambda i: (i, 0)),
                  pl.BlockSpec((tm, SSD_WIDTH), lambda i: (i, 0)),
                  pl.BlockSpec((tm, RWKV_WIDTH), lambda i: (i, 0)),
                  _const_spec(w_out), pl.BlockSpec((1, d), lambda i: (0, 0)), _const_spec(w_q)],
        out_specs=[pl.BlockSpec((tm, d), lambda i: (i, 0)), pl.BlockSpec((tm, d), lambda i: (i, 0))],
        out_shape=[jax.ShapeDtypeStruct((n, d), F32), jax.ShapeDtypeStruct((n, d), F32)],
        compiler_params=_cparams("parallel"),
        name="outproj_q",
    )(x, y_ssd, y_rw, w_out, g.reshape(1, d), w_q)


def _xattn_kernel(q_ref, x1_ref, k_ref, v_ref, wo_ref, o_ref):
    q = q_ref[...]
    scale = XA_HEAD_DIM ** -0.5
    outs = []
    for h in range(XA_HEADS):
        sl = slice(h * XA_HEAD_DIM, (h + 1) * XA_HEAD_DIM)
        s = _bdot(q[:, sl], k_ref[0, :, sl], _NT) * scale
        s = s - jnp.max(s, axis=-1, keepdims=True)
        e = jnp.exp(s)
        pr = e / jnp.sum(e, axis=-1, keepdims=True)
        outs.append(_bdot(pr, v_ref[0, :, sl]))
    o_ref[...] = x1_ref[...] + _bdot(jnp.concatenate(outs, axis=1), wo_ref[...])


def _xattn(q, x1, mk, mv, w_o, tq, seq_len):
    n, d = q.shape
    per = seq_len // tq
    row_spec = pl.BlockSpec((tq, d), lambda i: (i, 0))
    kv_spec = pl.BlockSpec((1, N_MEM, d), lambda i: (i // per, 0, 0))
    return pl.pallas_call(
        _xattn_kernel,
        grid=(n // tq,),
        in_specs=[row_spec, row_spec, kv_spec, kv_spec, _const_spec(w_o)],
        out_specs=row_spec,
        out_shape=jax.ShapeDtypeStruct((n, d), F32),
        compiler_params=_cparams("parallel"),
        name="xattn",
    )(q, x1, mk, mv, w_o)


def _xattn_cache_kernel(q_ref, x1_ref, k_ref, v_ref, wo_ref, o_ref, *, seq_len):
    tq = q_ref.shape[0]
    nmem = tq // seq_len
    rows = N_MEM * XA_HEADS
    cols = XA_HEADS * tq
    q = q_ref[...] * (XA_HEAD_DIM ** -0.5)
    qblk = jnp.concatenate([q[:, h * XA_HEAD_DIM:(h + 1) * XA_HEAD_DIM] for h in range(XA_HEADS)], axis=0)
    key_head = _iota2((cols, rows), 1) & (XA_HEADS - 1)
    qrow = _iota2((cols, 1), 0)
    head_ok = key_head == (qrow >> int(math.log2(tq)))
    out_mem = _iota2((tq, 1), 0) >> int(math.log2(seq_len))
    o = None
    for m in range(nmem):
        kall = k_ref[m].reshape(rows, XA_HEAD_DIM)
        vall = v_ref[m].reshape(rows, XA_HEAD_DIM)
        s = jnp.where(head_ok, _bdot(qblk, kall, _NT), -jnp.inf)
        e = jnp.exp(s - jnp.max(s, axis=-1, keepdims=True))
        pr = e / jnp.sum(e, axis=-1, keepdims=True)
        om = _bdot(pr, vall)
        om = jnp.concatenate([om[h * tq:(h + 1) * tq] for h in range(XA_HEADS)], axis=1)
        o = om if o is None else jnp.where(out_mem == m, om, o)
    o_ref[...] = x1_ref[...] + _bdot(o, wo_ref[...])


def _xattn_cache(q, x1, ck, cv, w_o, tq, seq_len):
    n, d = q.shape
    nmem = tq // seq_len
    row_spec = pl.BlockSpec((tq, d), lambda i: (i, 0))
    kv_spec = pl.BlockSpec((nmem, N_MEM, XA_HEADS, XA_HEAD_DIM), lambda i: (i, 0, 0, 0))
    return pl.pallas_call(
        functools.partial(_xattn_cache_kernel, seq_len=seq_len),
        grid=(n // tq,),
        in_specs=[row_spec, row_spec, kv_spec, kv_spec, _const_spec(w_o)],
        out_specs=row_spec,
        out_shape=jax.ShapeDtypeStruct((n, d), F32),
        compiler_params=_cparams("parallel"),
        name="xattn_cache",
    )(q, x1, ck, cv, w_o)


FF_SUB = 256
FF_GROUP = 4


def _ffn_kernel(*refs, tm, Lb, prompt, blocks_per_seq):
    if prompt:
        x_ref, g_ref, wu_ref, cw_ref, cb_ref, wd_ref, fg_ref, y_ref, s_ref, carry_sc = refs
        hist_ref = None
    else:
        x_ref, hist_ref, g_ref, wu_ref, cw_ref, cb_ref, wd_ref, fg_ref, y_ref, s_ref = refs
        carry_sc = None
    _ffn_body(x_ref[...], hist_ref, g_ref, wu_ref, cw_ref, cb_ref, wd_ref, fg_ref, y_ref, s_ref, carry_sc,
              tm=tm, Lb=Lb, prompt=prompt, blocks_per_seq=blocks_per_seq)


def _ffn_body(x, hist_ref, g_ref, wu_ref, cw_ref, cb_ref, wd_ref, fg_ref, y_ref, s_ref, carry_sc,
              *, tm, Lb, prompt, blocks_per_seq):
    if prompt:
        @pl.when(pl.program_id(0) % blocks_per_seq == 0)
        def _():
            carry_sc[...] = jnp.zeros_like(carry_sc)

    hn = _rms(x, g_ref[...]).astype(BF16)
    nsub = D_FF // FF_SUB
    r8 = _iota2((8, 1), 0)
    tpos = _iota2((tm, 1), 0) & (Lb - 1)
    if not prompt:
        trow = _iota2((tm, tm // Lb), 0)
        seq0 = _iota2((tm, tm // Lb), 1) * Lb
        at_row0 = (trow == seq0).astype(BF16)
        at_row1 = (trow == seq0 + 1).astype(BF16)

    def conv(up, cols):
        cw = cw_ref[:, cols]
        cb = cb_ref[:, cols]
        if prompt:
            c8 = carry_sc[:, cols]
            carry_sc[:, cols] = up[tm - 8:tm]
            s_ref[0, :, cols] = up[tm - 2:tm]
            top = up[0:8]
            p1 = jnp.where(r8 >= 1, pltpu.roll(top, 1, 0), c8[7:8])
            p2 = jnp.where(r8 >= 2, pltpu.roll(top, 2, 0), jnp.where(r8 == 0, c8[6:7], c8[7:8]))
            head = (cw[0:1] * p2 + cw[1:2] * p1 + cw[2:3] * top) + cb
            body = (cw[0:1] * pltpu.roll(up, 2, 0) + cw[1:2] * pltpu.roll(up, 1, 0) + cw[2:3] * up) + cb
            return jnp.concatenate([head, body[8:]], axis=0)
        for b in range(tm // Lb):
            s_ref[b, :, cols] = up[b * Lb + Lb - 2:b * Lb + Lb]
        st0, st1 = hist_ref[0, :, cols], hist_ref[1, :, cols]
        h1 = _maskdot(at_row0, st1)
        h2 = _maskdot(at_row0, st0) + _maskdot(at_row1, st1)
        return (cw[0:1] * _shifted(up, 2, tpos, h2) + cw[1:2] * _shifted(up, 1, tpos, h1) + cw[2:3] * up) + cb

    def up_proj(k):
        cg = slice(k * FF_SUB, (k + 1) * FF_SUB)
        cv = slice(D_FF + k * FF_SUB, D_FF + (k + 1) * FF_SUB)
        return (jnp.dot(hn, wu_ref[:, cg], preferred_element_type=F32),
                jnp.dot(hn, wu_ref[:, cv], preferred_element_type=F32))

    def gate_mul(k, ug, uv):
        cg = slice(k * FF_SUB, (k + 1) * FF_SUB)
        cv = slice(D_FF + k * FF_SUB, D_FF + (k + 1) * FF_SUB)
        gate = conv(ug, cg)
        return (gate * _sigmoid(gate) * conv(uv, cv)).astype(BF16)

    acc = x
    ups = {}
    acts = []
    for k in range(nsub + 1):
        if k < nsub:
            ups[k] = up_proj(k)
        if k >= 1:
            acts.append(gate_mul(k - 1, *ups.pop(k - 1)))
            if len(acts) == FF_GROUP or k == nsub:
                lo = (k - len(acts)) * FF_SUB
                acc = acc + jnp.dot(jnp.concatenate(acts, axis=1), wd_ref[lo:k * FF_SUB, :],
                                    preferred_element_type=F32)
                acts = []
    y_ref[...] = _rms(acc, fg_ref[...])


def _resident_spec(a):
    nd = a.ndim
    return pl.BlockSpec(a.shape, lambda *_: (0,) * nd, pipeline_mode=pl.Buffered(1))


def _ffn(x, hist, p, final_g, tm, Lb, prompt, seq_len):
    n, d = x.shape
    w_up = p["ffn_w_up"].astype(BF16)
    w_down = p["ffn_w_down"].astype(BF16)
    cw, cb = p["ffn_conv_w"], p["ffn_conv_b"].reshape(1, 2 * D_FF)
    row_spec = pl.BlockSpec((tm, d), lambda i: (i, 0))
    if prompt:
        bps = seq_len // tm
        nstate = n // tm
        s_spec = pl.BlockSpec((1, 2, 2 * D_FF), lambda i: (i, 0, 0))
        extra_in, extra_specs = [], []
        scratch = [pltpu.VMEM((8, 2 * D_FF), F32)]
    else:
        bps = 1
        nstate = n // seq_len
        s_spec = pl.BlockSpec((tm // Lb, 2, 2 * D_FF), lambda i: (i, 0, 0))
        extra_in = [hist]
        extra_specs = [pl.BlockSpec((2, tm // Lb, 2 * D_FF), lambda i: (0, i, 0))]
        scratch = []
    consts = [p["norm_ffn_w"].reshape(1, d), w_up, cw, cb, w_down, final_g.reshape(1, d)]
    y, st = pl.pallas_call(
        functools.partial(_ffn_kernel, tm=tm, Lb=Lb, prompt=prompt, blocks_per_seq=bps),
        grid=(n // tm,),
        in_specs=[row_spec] + extra_specs + [_resident_spec(a) for a in consts],
        out_specs=[row_spec, s_spec],
        out_shape=[jax.ShapeDtypeStruct((n, d), F32), jax.ShapeDtypeStruct((nstate, 2, 2 * D_FF), F32)],
        scratch_shapes=scratch,
        compiler_params=_cparams("arbitrary"),
        name="convffn",
    )(x, *extra_in, *consts)
    if prompt:
        st = st[bps - 1::bps]
    return y, st


def _post_mix_kernel(x_ref, ys_ref, yr_ref, k_ref, v_ref, wo_ref, gxa_ref, wq_ref, wao_ref,
                     g_ref, wu_ref, cw_ref, cb_ref, wd_ref, fg_ref, y_ref, s_ref, carry_sc,
                     *, tm, seq_len):
    wo = wo_ref[...]
    x1 = x_ref[...] + (jnp.dot(ys_ref[...], wo[0:SSD_WIDTH], preferred_element_type=F32)
                       + jnp.dot(yr_ref[...], wo[SSD_WIDTH:], preferred_element_type=F32))
    q = (_bdot(_rms(x1, gxa_ref[...]), wq_ref[...]) * (XA_HEAD_DIM ** -0.5)).astype(BF16)
    outs = []
    for h in range(XA_HEADS):
        sl = slice(h * XA_HEAD_DIM, (h + 1) * XA_HEAD_DIM)
        s = lax.dot_general(q[:, sl], k_ref[0, :, sl].astype(BF16), _NT, preferred_element_type=F32)
        s = s - jnp.max(s, axis=-1, keepdims=True)
        e = jnp.exp(s)
        pr = e / jnp.sum(e, axis=-1, keepdims=True)
        outs.append(_bdot(pr, v_ref[0, :, sl]))
    x2 = x1 + _bdot(jnp.concatenate(outs, axis=1), wao_ref[...])
    _ffn_body(x2, None, g_ref, wu_ref, cw_ref, cb_ref, wd_ref, fg_ref, y_ref, s_ref, carry_sc,
              tm=tm, Lb=seq_len, prompt=True, blocks_per_seq=seq_len // tm)


def _post_mix(x, y_ssd, y_rw, mk, mv, w_out, g_xa, w_q, w_o, p, final_g, tm, seq_len):
    n, d = x.shape
    bps = seq_len // tm
    row = lambda w: pl.BlockSpec((tm, w), lambda i: (i, 0))
    kv_spec = pl.BlockSpec((1, N_MEM, d), lambda i: (i // bps, 0, 0))
    consts = [w_out, g_xa.reshape(1, d), w_q, w_o, p["norm_ffn_w"].reshape(1, d), p["ffn_w_up"].astype(BF16),
              p["ffn_conv_w"], p["ffn_conv_b"].reshape(1, 2 * D_FF), p["ffn_w_down"].astype(BF16),
              final_g.reshape(1, d)]
    y, st = pl.pallas_call(
        functools.partial(_post_mix_kernel, tm=tm, seq_len=seq_len),
        grid=(n // tm,),
        in_specs=[row(d), row(SSD_WIDTH), row(RWKV_WIDTH), kv_spec, kv_spec] + [_resident_spec(a) for a in consts],
        out_specs=[row(d), pl.BlockSpec((1, 2, 2 * D_FF), lambda i: (i, 0, 0))],
        out_shape=[jax.ShapeDtypeStruct((n, d), F32), jax.ShapeDtypeStruct((n // tm, 2, 2 * D_FF), F32)],
        scratch_shapes=[pltpu.VMEM((8, 2 * D_FF), F32)],
        compiler_params=_cparams("arbitrary"),
        name="post_mix",
    )(x, y_ssd, y_rw, mk, mv, *consts)
    return y, st[bps - 1::bps]


def _pair_blocks(sbd):
    b = sbd.shape[0]
    s0 = sbd[:, :, 0:64, 0:64]
    s1 = sbd[:, :, 64:128, 64:128]
    return jnp.stack([s0, s1], axis=2).reshape(b, 8, 64, 64)


def _hist_rows(state, seq_len, k):
    b, w, c = state.shape
    cols = [state[:, w + t - k] if t < k else jnp.zeros((b, c), state.dtype) for t in range(seq_len)]
    return jnp.stack(cols, axis=1).reshape(b * seq_len, c)


def kernel(x_prompt, x_sample, mem_prompt, state_ssm_conv, state_ssm, state_shift, state_wkv, state_ffn_conv, cache_mem_k, cache_mem_v, norm_mix_w, w_in, ssd_conv_w, ssd_conv_b, ssd_dt_bias, ssd_a_log, ssd_d, ssd_norm_w, rwkv_mu, rwkv_w0, rwkv_w2, rwkv_a0, rwkv_a2, rwkv_g2, rwkv_k_k, rwkv_k_a, rwkv_r_k, rwkv_ln_w, rwkv_ln_b, w_out, norm_xa_w, mem_norm_w, xa_w_q, xa_w_k, xa_w_v, xa_w_o, norm_ffn_w, ffn_w_up, ffn_conv_w, ffn_conv_b, ffn_w_down, final_norm_w):
    depth = w_in.shape[0]
    assert depth == 1, "final rmsnorm is fused into the (single) layer's ConvFFN kernel"
    bp, lp, d = x_prompt.shape
    bs, ls, _ = x_sample.shape
    i = 0
    p = dict(ssd_conv_w=ssd_conv_w[i], ssd_conv_b=ssd_conv_b[i], ssd_dt_bias=ssd_dt_bias[i],
             ssd_a_log=ssd_a_log[i], ssd_d=ssd_d[i], ssd_norm_w=ssd_norm_w[i], rwkv_mu=rwkv_mu[i],
             rwkv_w0=rwkv_w0[i], rwkv_w2=rwkv_w2[i], rwkv_a0=rwkv_a0[i], rwkv_a2=rwkv_a2[i],
             rwkv_g2=rwkv_g2[i], rwkv_k_k=rwkv_k_k[i], rwkv_k_a=rwkv_k_a[i],
             rwkv_r_k=rwkv_r_k[i].reshape(-1), rwkv_ln_w=rwkv_ln_w[i], rwkv_ln_b=rwkv_ln_b[i],
             norm_ffn_w=norm_ffn_w[i], ffn_w_up=ffn_w_up[i], ffn_conv_w=ffn_conv_w[i],
             ffn_conv_b=ffn_conv_b[i], ffn_w_down=ffn_w_down[i])

    w_in_p = jnp.concatenate([w_in[i][:, :SSD_PROJ], jnp.zeros((d, SEG - SSD_PROJ), F32),
                              w_in[i][:, SSD_PROJ:]], axis=1).astype(BF16)
    w_out_b = w_out[i].astype(BF16)
    w_q_b = (xa_w_q[i]).astype(BF16)
    w_o_b = xa_w_o[i].astype(BF16)
    w_kv_b = jnp.concatenate([xa_w_k[i], xa_w_v[i]], axis=1).astype(BF16)

    xp = x_prompt.reshape(bp * lp, d)
    xs = x_sample.reshape(bs * ls, d)

    mk, mv = _mem_kv(mem_prompt.reshape(bp * N_MEM, d), mem_norm_w[i], w_kv_b, 512)
    mk = mk.reshape(bp, N_MEM, d)
    mv = mv.reshape(bp, N_MEM, d)
    tm_p = 1024
    u_p, dt_p, tail_p = _norm_proj(xp, norm_mix_w[i], w_in_p, tm_p, BF16)
    y_ssd_p, ssm_p = _ssd_prompt(u_p, dt_p, p, bp, lp)
    y_rw_p, wkv_bd_p = _wkv_prompt(u_p, p, bp, lp)
    y_p, ffn_conv_p = _post_mix(xp, y_ssd_p, y_rw_p, mk, mv, w_out_b, norm_xa_w[i], w_q_b, w_o_b, p,
                                final_norm_w, 512, lp)
    seq_tail = tail_p[lp // tm_p - 1::lp // tm_p]
    ssm_conv_p = seq_tail[:, 5:, SSD_WIDTH:SSD_WIDTH + SSD_XBC]
    shift_p = seq_tail[:, 7, SEG:]

    u_s, dt_s, _ = _norm_proj(xs, norm_mix_w[i], w_in_p, 512, F32)
    conv_state = state_ssm_conv[i]
    hist_conv = jnp.stack([_hist_rows(conv_state, ls, k) for k in (1, 2, 3)], axis=0)
    y_ssd_s, ssm_s = _ssd_sample(u_s, dt_s, hist_conv, state_ssm[i].reshape(bs, 4, LANES, LANES), p, ls)
    hist_shift = _hist_rows(state_shift[i][:, None, :], ls, 1)
    y_rw_s, wkv_s = _wkv_sample(u_s, hist_shift, state_wkv[i], p, ls)
    x1_s, q_s = _outproj(xs, y_ssd_s, y_rw_s, w_out_b, norm_xa_w[i], w_q_b, 512)
    x2_s = _xattn_cache(q_s, x1_s, cache_mem_k[i], cache_mem_v[i], w_o_b, 16, ls)
    y_s, ffn_conv_s = _ffn(x2_s, jnp.swapaxes(state_ffn_conv[i], 0, 1), p, final_norm_w, 256, ls, False, ls)
    xbc_rows = [u_s[t::ls, SSD_WIDTH:SSD_WIDTH + SSD_XBC] for t in range(ls)]
    ssm_conv_s = jnp.stack(([conv_state[:, j] for j in range(conv_state.shape[1])] + xbc_rows)[-3:], axis=1)
    shift_s = u_s[ls - 1::ls, SEG:]

    e = lambda a: a[None]
    return (y_p.reshape(bp, lp, d), y_s.reshape(bs, ls, d),
            e(ssm_conv_p), e(ssm_conv_s),
            e(ssm_p.reshape(bp, SSD_HEADS, 64, SSD_STATE)), e(ssm_s.reshape(bs, SSD_HEADS, 64, SSD_STATE)),
            e(shift_p), e(shift_s),
            e(_pair_blocks(wkv_bd_p)), e(wkv_s),
            e(ffn_conv_p), e(ffn_conv_s),
            e(mk.reshape(bp, N_MEM, XA_HEADS, XA_HEAD_DIM)), e(mv.reshape(bp, N_MEM, XA_HEADS, XA_HEAD_DIM)))
```

```python
import functools
import math

import jax
import jax.numpy as jnp
from jax import lax
from jax.experimental import pallas as pl
from jax.experimental.pallas import tpu as pltpu

F32 = jnp.float32
BF16 = jnp.bfloat16

D_MODEL = 1024
N_MEM = 256
XA_HEADS = 4
XA_HEAD_DIM = D_MODEL // XA_HEADS
SSD_WIDTH = 512
SSD_HEADS = 8
SSD_STATE = 128
SSD_XBC = 1024
SSD_PROJ = SSD_WIDTH + SSD_XBC + SSD_HEADS
RWKV_WIDTH = 512
RWKV_PROJ = 1792
D_FF = 2816
EPS = 1e-6
GN_EPS = 64e-5

LANES = 128
SEG = 1792
DT_OFF = SSD_WIDTH + SSD_XBC
VMEM_LIMIT_BYTES = 56 * 1024 * 1024


def _cparams(*sem):
    return pltpu.CompilerParams(dimension_semantics=sem, vmem_limit_bytes=VMEM_LIMIT_BYTES)


_NN = (((1,), (0,)), ((), ()))
_NT = (((1,), (1,)), ((), ()))
_TN = (((0,), (0,)), ((), ()))


def _bdot(a, b, dims=_NN):
    return lax.dot_general(a.astype(BF16), b.astype(BF16), dims, preferred_element_type=F32)


def _split2(x):
    hi = x.astype(BF16)
    lo = (x - hi.astype(F32)).astype(BF16)
    return hi, lo


def _maskdot(m_bf16, x):
    hi = x.astype(BF16)
    r1 = x - hi.astype(F32)
    mid = r1.astype(BF16)
    lo = (r1 - mid.astype(F32)).astype(BF16)
    return jnp.dot(jnp.concatenate([m_bf16, m_bf16, m_bf16], axis=1),
                   jnp.concatenate([hi, mid, lo], axis=0), preferred_element_type=F32)


def _sigmoid(x):
    return 0.5 + 0.5 * jnp.tanh(0.5 * x)


def _softplus(x):
    return jnp.maximum(x, 0.0) + jnp.log1p(jnp.exp(-jnp.abs(x)))


def _rms(x, g):
    return x * lax.rsqrt(jnp.mean(x * x, axis=-1, keepdims=True) + EPS) * g


def _iota2(shape, dim):
    return lax.broadcasted_iota(jnp.int32, shape, dim)


def _shifted(x, k, tpos, hist):
    return jnp.where(tpos >= k, pltpu.roll(x, k, 0), hist)


def _norm_proj_kernel(x_ref, g_ref, w_ref, o_ref, dt_ref, tail_ref, hn_sc):
    j = pl.program_id(1)

    @pl.when(j == 0)
    def _():
        hn_sc[...] = _rms(x_ref[...], g_ref[...]).astype(BF16)

    acc = jnp.dot(hn_sc[...], w_ref[...], preferred_element_type=F32)
    o_ref[...] = acc.astype(o_ref.dtype)
    tail_ref[0] = acc[acc.shape[0] - 8:]

    @pl.when(j == 0)
    def _():
        dt_ref[...] = acc[:, DT_OFF:DT_OFF + LANES]


def _norm_proj(x, g, w_bf16, tm, out_dtype):
    n, d = x.shape
    return pl.pallas_call(
        _norm_proj_kernel,
        grid=(n // tm, 2),
        in_specs=[pl.BlockSpec((tm, d), lambda i, j: (i, 0)),
                  pl.BlockSpec((1, d), lambda i, j: (0, 0)),
                  pl.BlockSpec((d, SEG), lambda i, j: (0, j))],
        out_specs=[pl.BlockSpec((tm, SEG), lambda i, j: (i, j)),
                   pl.BlockSpec((tm, LANES), lambda i, j: (i, 0)),
                   pl.BlockSpec((1, 8, SEG), lambda i, j: (i, 0, j))],
        out_shape=[jax.ShapeDtypeStruct((n, 2 * SEG), out_dtype),
                   jax.ShapeDtypeStruct((n, LANES), F32),
                   jax.ShapeDtypeStruct((n // tm, 8, 2 * SEG), F32)],
        scratch_shapes=[pltpu.VMEM((tm, d), BF16)],
        compiler_params=_cparams("parallel", "arbitrary"),
        name="norm_proj",
    )(x, g.reshape(1, d), w_bf16)


def _mem_kv_kernel(x_ref, g_ref, w_ref, k_ref, v_ref, k4_ref, v4_ref):
    hn = _rms(x_ref[...], g_ref[...]).astype(BF16)
    d = k_ref.shape[1]
    k = jnp.dot(hn, w_ref[:, 0:d], preferred_element_type=F32)
    v = jnp.dot(hn, w_ref[:, d:2 * d], preferred_element_type=F32)
    k_ref[...] = k
    v_ref[...] = v
    for h in range(XA_HEADS):
        sl = slice(h * XA_HEAD_DIM, (h + 1) * XA_HEAD_DIM)
        k4_ref[:, h, :] = k[:, sl]
        v4_ref[:, h, :] = v[:, sl]


def _mem_kv(x, g, w_kv_bf16, tm):
    n, d = x.shape
    row_spec = pl.BlockSpec((tm, d), lambda i: (i, 0))
    head_spec = pl.BlockSpec((tm, XA_HEADS, XA_HEAD_DIM), lambda i: (i, 0, 0))
    return pl.pallas_call(
        _mem_kv_kernel,
        grid=(n // tm,),
        in_specs=[row_spec, pl.BlockSpec((1, d), lambda i: (0, 0)), _const_spec(w_kv_bf16)],
        out_specs=[row_spec, row_spec, head_spec, head_spec],
        out_shape=[jax.ShapeDtypeStruct((n, d), F32)] * 2
                  + [jax.ShapeDtypeStruct((n, XA_HEADS, XA_HEAD_DIM), F32)] * 2,
        compiler_params=_cparams("parallel"),
        name="mem_kv",
    )(x, g.reshape(1, d), w_kv_bf16)


def _ssd_kernel(*refs, Q, Lb, prompt, G=1):
    if not prompt:
        return _ssd_block(*refs, Q=Q, Lb=Lb, prompt=False)
    (u_ref, dt_ref, cw_ref, cb_ref, dtb_ref, an_ref, dsk_ref, nw_ref, y_ref, hout_ref, carry_sc, h_sc) = refs

    @pl.when(pl.program_id(1) == 0)
    def _():
        carry_sc[...] = jnp.zeros_like(carry_sc)
        h_sc[...] = jnp.zeros_like(h_sc)

    for gi in range(G):
        _ssd_block(u_ref.at[gi], dt_ref.at[gi], cw_ref, cb_ref, dtb_ref, an_ref, dsk_ref, nw_ref, y_ref.at[gi],
                   hout_ref.at[pl.ds(gi, 1)], carry_sc.at[gi], h_sc.at[pl.ds(4 * gi, 4)],
                   Q=Q, Lb=Lb, prompt=True)


def _ssd_block(*refs, Q, Lb, prompt):
    ns = Q // Lb
    lb = int(math.log2(Lb))
    if prompt:
        (u_ref, dt_ref, cw_ref, cb_ref, dtb_ref, an_ref, dsk_ref, nw_ref,
         y_ref, hout_ref, carry_sc, h_sc) = refs
    else:
        (u_ref, dt_ref, hist_ref, hin_ref, sel_ref, cw_ref, cb_ref, dtb_ref, an_ref, dsk_ref, nw_ref,
         y_ref, hout_ref, cs_ref) = refs

    u = u_ref[...].astype(F32)
    z = u[:, 0:SSD_WIDTH]
    x = u[:, SSD_WIDTH:SSD_WIDTH + SSD_XBC]
    dtr = dt_ref[...]
    if not prompt:
        cs_ref[...] = _maskdot(sel_ref[...], x)

    rows = _iota2((Q, 1), 0)
    tpos = rows & (Lb - 1)
    if prompt:
        c8 = carry_sc[...]
        h1 = c8[7:8]
        h2 = jnp.where(rows == 0, c8[6:7], c8[7:8])
        h3 = jnp.where(rows == 0, c8[5:6], jnp.where(rows == 1, c8[6:7], c8[7:8]))
        carry_sc[...] = x[Q - 8:Q]
    else:
        h1, h2, h3 = hist_ref[0], hist_ref[1], hist_ref[2]
    cw = cw_ref[...]
    xc = (cw[0:1] * _shifted(x, 3, tpos, h3) + cw[1:2] * _shifted(x, 2, tpos, h2)
          + cw[2:3] * _shifted(x, 1, tpos, h1) + cw[3:4] * x) + cb_ref[...]
    xc = xc * _sigmoid(xc)
    xs = xc[:, 0:SSD_WIDTH]
    bm = xc[:, SSD_WIDTH:SSD_WIDTH + 2 * SSD_STATE]
    cm = xc[:, SSD_WIDTH + 2 * SSD_STATE:]

    dt = _softplus(dtr + dtb_ref[...])
    da = dt * an_ref[...]

    ri = _iota2((Q, Q), 0)
    ci = _iota2((Q, Q), 1)
    same = (ri >> lb) == (ci >> lb)
    tril = same & (ci <= ri)
    sel = ci == (((ri >> lb) << lb) + (Lb - 1))
    acs = _maskdot(tril.astype(BF16), da)
    acs_t = acs.T
    acs_last = _maskdot(sel.astype(BF16), acs)
    dec_end = jnp.exp(acs_last - acs)
    eacs = jnp.exp(acs)
    seqid = rows >> lb

    lane = _iota2((Q, LANES), 1)
    lo_half = lane < 64
    prow = _iota2((LANES, LANES), 0)
    dsk = dsk_ref[...]

    ys = []
    for q in range(4):
        g = q // 2
        h0, h1i = 2 * q, 2 * q + 1
        if q % 2 == 0:
            cg = cm[:, g * SSD_STATE:(g + 1) * SSD_STATE]
            bg = bm[:, g * SSD_STATE:(g + 1) * SSD_STATE]
            cb_g = _bdot(cg, bg, _NT)
            if ns == 1:
                cexp, bexp = cg.astype(BF16), bg.astype(BF16)
            else:
                cexp = jnp.concatenate([jnp.where(seqid == b, cg, 0.0).astype(BF16) for b in range(ns)], axis=1)
                bexp = jnp.concatenate([jnp.where(seqid == b, bg, 0.0).astype(BF16) for b in range(ns)], axis=1)
        m0 = jnp.where(tril, cb_g * jnp.exp(acs[:, h0:h0 + 1] - acs_t[h0:h0 + 1, :]), 0.0)
        m1 = jnp.where(tril, cb_g * jnp.exp(acs[:, h1i:h1i + 1] - acs_t[h1i:h1i + 1, :]), 0.0)
        xp = xs[:, q * LANES:(q + 1) * LANES]
        xdt = xp * jnp.where(lo_half, dt[:, h0:h0 + 1], dt[:, h1i:h1i + 1])
        xdt0 = jnp.where(lo_half, xdt, 0.0)
        xdt1 = jnp.where(lo_half, 0.0, xdt)
        ydiag = _bdot(jnp.concatenate([m0, m1], axis=1), jnp.concatenate([xdt0, xdt1], axis=0))
        if prompt:
            hst = h_sc[q]
        else:
            hst = jnp.concatenate([hin_ref[b, q] for b in range(ns)], axis=1)
        ecs = jnp.where(lo_half, eacs[:, h0:h0 + 1], eacs[:, h1i:h1i + 1])
        yoff = _bdot(cexp, hst, _NT) * ecs
        xd = xdt * jnp.where(lo_half, dec_end[:, h0:h0 + 1], dec_end[:, h1i:h1i + 1])
        incr = _bdot(xd, bexp, _TN)
        scales = []
        for b in range(ns):
            r = b * Lb + Lb - 1
            e = eacs[r:r + 1, :]
            scales.append(jnp.where(prow < 64, e[:, h0:h0 + 1], e[:, h1i:h1i + 1]))
        scale = scales[0] if ns == 1 else jnp.concatenate(scales, axis=1)
        hnew = hst * scale + incr
        if prompt:
            h_sc[q] = hnew
            hout_ref[0, q] = hnew
        else:
            for b in range(ns):
                hout_ref[b, q] = hnew[:, b * LANES:(b + 1) * LANES]
        ys.append(ydiag + yoff + dsk[:, q * LANES:(q + 1) * LANES] * xp)

    y = jnp.concatenate(ys, axis=1)
    y = y * (z * _sigmoid(z))
    half = SSD_WIDTH // 2
    outs = []
    for g in range(2):
        yg = y[:, g * half:(g + 1) * half]
        outs.append(yg * lax.rsqrt(jnp.mean(yg * yg, axis=-1, keepdims=True) + EPS))
    y_ref[...] = (jnp.concatenate(outs, axis=1) * nw_ref[...]).astype(y_ref.dtype)


def _ssd_params(p):
    an = jnp.zeros((1, LANES), F32).at[0, :SSD_HEADS].set(-jnp.exp(p["ssd_a_log"]))
    dtb = jnp.zeros((1, LANES), F32).at[0, :SSD_HEADS].set(p["ssd_dt_bias"])
    dsk = jnp.repeat(p["ssd_d"], SSD_WIDTH // SSD_HEADS).reshape(1, SSD_WIDTH)
    return [p["ssd_conv_w"], p["ssd_conv_b"].reshape(1, SSD_XBC), dtb, an, dsk,
            p["ssd_norm_w"].reshape(1, SSD_WIDTH)]


def _const_spec(a):
    nd = a.ndim
    return pl.BlockSpec(a.shape, lambda *_: (0,) * nd)


def _ssd_prompt(u, dt, p, batch, seq, Q=128, G=4):
    nc = seq // Q
    params = _ssd_params(p)
    y, hout = pl.pallas_call(
        functools.partial(_ssd_kernel, Q=Q, Lb=Q, prompt=True, G=G),
        grid=(batch // G, nc),
        in_specs=[pl.BlockSpec((G, Q, SEG), lambda b, c: (b, c, 0)),
                  pl.BlockSpec((G, Q, LANES), lambda b, c: (b, c, 0))] + [_const_spec(a) for a in params],
        out_specs=[pl.BlockSpec((G, Q, SSD_WIDTH), lambda b, c: (b, c, 0)),
                   pl.BlockSpec((G, 4, LANES, LANES), lambda b, c: (b, 0, 0, 0))],
        out_shape=[jax.ShapeDtypeStruct((batch, seq, SSD_WIDTH), BF16),
                   jax.ShapeDtypeStruct((batch, 4, LANES, LANES), F32)],
        scratch_shapes=[pltpu.VMEM((G, 8, SSD_XBC), F32), pltpu.VMEM((4 * G, LANES, LANES), F32)],
        compiler_params=_cparams("parallel", "arbitrary"),
        name="ssd_prompt",
    )(u.reshape(batch, seq, 2 * SEG), dt.reshape(batch, seq, LANES), *params)
    return y.reshape(batch * seq, SSD_WIDTH), hout


def _row_selector(nseq, Lb, offsets):
    k = len(offsets)
    rows = jnp.arange(nseq * k)
    target = (rows // k) * Lb + jnp.asarray(offsets)[rows % k]
    return (target[:, None] == jnp.arange(nseq * Lb)[None, :]).astype(BF16)


def _ssd_sample(u, dt, hist, hin, p, Lb, Q=64):
    n = u.shape[0]
    ns = Q // Lb
    params = _ssd_params(p)
    sel = _row_selector(ns, Lb, (Lb - 3, Lb - 2, Lb - 1))
    return pl.pallas_call(
        functools.partial(_ssd_kernel, Q=Q, Lb=Lb, prompt=False),
        grid=(n // Q,),
        in_specs=[pl.BlockSpec((Q, SEG), lambda i: (i, 0)),
                  pl.BlockSpec((Q, LANES), lambda i: (i, 0)),
                  pl.BlockSpec((3, Q, SSD_XBC), lambda i: (0, i, 0)),
                  pl.BlockSpec((ns, 4, LANES, LANES), lambda i: (i, 0, 0, 0)),
                  _const_spec(sel)]
                 + [_const_spec(a) for a in params],
        out_specs=[pl.BlockSpec((Q, SSD_WIDTH), lambda i: (i, 0)),
                   pl.BlockSpec((ns, 4, LANES, LANES), lambda i: (i, 0, 0, 0)),
                   pl.BlockSpec((ns * 3, SSD_XBC), lambda i: (i, 0))],
        out_shape=[jax.ShapeDtypeStruct((n, SSD_WIDTH), BF16),
                   jax.ShapeDtypeStruct(hin.shape, F32),
                   jax.ShapeDtypeStruct((n // Lb * 3, SSD_XBC), F32)],
        compiler_params=_cparams("parallel"),
        name="ssd_sample",
    )(u, dt, hist, hin, sel, *params)


def _dot3s(a, b, dims=_NN):
    ka = dims[0][0][0]
    kb = dims[0][1][0]
    lhs = jnp.concatenate([a[0], a[0], a[1]], axis=ka)
    rhs = jnp.concatenate([b[0], b[1], b[0]], axis=kb)
    return lax.dot_general(lhs, rhs, dims, preferred_element_type=F32)


def _cat2(parts, axis):
    return (jnp.concatenate([p[0] for p in parts], axis=axis), jnp.concatenate([p[1] for p in parts], axis=axis))


def _tri_inverse(mats, lb):
    n = mats[0].shape[0]
    ri = _iota2((n, n), 0)
    ci = _iota2((n, n), 1)
    off1 = ((ri >> 1) == (ci >> 1)) & ((ri & 1) == 1) & ((ci & 1) == 0)
    eye = jnp.where(ri == ci, 1.0, 0.0)
    ts = [eye + jnp.where(off1, a, 0.0) for a in mats]
    for lvl in range(1, lb):
        m = 1 << lvl
        off = ((ri >> (lvl + 1)) == (ci >> (lvl + 1))) & ((ri & (2 * m - 1)) >= m) & ((ci & (2 * m - 1)) < m)
        tsb = [t.astype(BF16) for t in ts]
        ws = [jnp.dot(jnp.where(off, a, 0.0).astype(BF16), tb, preferred_element_type=F32)
              for a, tb in zip(mats, tsb)]
        ts = [t + jnp.dot(tb, w.astype(BF16), preferred_element_type=F32) for t, tb, w in zip(ts, tsb, ws)]
    return [t.astype(BF16) for t in ts]


def _refined_solve(tinvs, mats, rhss):
    d = functools.partial(jnp.dot, preferred_element_type=F32)
    n = range(len(mats))
    rs = [_split2(r) for r in rhss]
    u0 = [d(jnp.concatenate([tinvs[i], tinvs[i]], axis=1), jnp.concatenate(rs[i], axis=0)) for i in n]
    au = [_dot3s(_split2(mats[i]), _split2(u0[i])) for i in n]
    res = [((rhss[i] - u0[i]) + au[i]).astype(BF16) for i in n]
    return [u0[i] + d(tinvs[i], res[i]) for i in n]


def _wkv_kernel(*refs, C, Lb, prompt, G):
    nb = C // Lb
    lb = int(math.log2(Lb))
    R = 2 * C
    T = G * C
    if prompt:
        (u_ref, mu_ref, w0_ref, w2_ref, a0_ref, a2_ref, g2_ref, kk_ref, ka_ref, rk_ref, lnw_ref, lnb_ref,
         y_ref, sout_ref, carry_sc, s_sc) = refs
        c = pl.program_id(1)

        @pl.when(c == 0)
        def _():
            carry_sc[...] = jnp.zeros_like(carry_sc)
            s_sc[...] = jnp.zeros_like(s_sc)
    else:
        (u_ref, hist_ref, sin_ref, sel_ref, mu_ref, w0_ref, w2_ref, a0_ref, a2_ref, g2_ref, kk_ref, ka_ref,
         rk_ref, lnw_ref, lnb_ref, y_ref, sout_ref, shift_ref) = refs

    rows = _iota2((T, 1), 0)
    tpos = rows & (Lb - 1)
    if prompt:
        u = u_ref[...].astype(F32).reshape(T, SEG)
        hist = jnp.concatenate([jnp.broadcast_to(carry_sc[g, 7:8], (C, SEG)) for g in range(G)], axis=0)
        for g in range(G):
            carry_sc[g] = u[g * C + C - 8:(g + 1) * C]
    else:
        u = u_ref[...].astype(F32)
        hist = hist_ref[...]
        shift_ref[...] = _maskdot(sel_ref[...], u)
    um = u + (_shifted(u, 1, tpos, hist) - u) * mu_ref[...]

    W = RWKV_WIDTH
    r = um[:, 0:W]
    k = um[:, W:2 * W]
    v = um[:, 2 * W:3 * W]
    t12 = um[:, 3 * W:3 * W + LANES]
    lg = um[:, 3 * W + LANES:3 * W + 2 * LANES]

    wl = w0_ref[...] + _bdot(jnp.tanh(t12), w2_ref[...])
    logw = -math.exp(-0.5) * _sigmoid(wl)
    a = _sigmoid(a0_ref[...] + _bdot(t12, a2_ref[...]))
    out_gate = _bdot(_sigmoid(lg), g2_ref[...])

    hi = _iota2((W, W), 0)
    hj = _iota2((W, W), 1)
    headsum = ((hi >> 6) == (hj >> 6)).astype(BF16)

    kk = k * kk_ref[...]
    kk = kk * lax.rsqrt(jnp.maximum(_bdot(kk * kk, headsum), 1e-24))
    kmod = k * (1.0 + (a - 1.0) * ka_ref[...])
    beta = kk * a

    ri = _iota2((T, T), 0)
    ci = _iota2((T, T), 1)
    same_c = (ri >> lb) == (ci >> lb)
    lc = _maskdot((same_c & (ci <= ri)).astype(BF16), logw)
    e_pos = jnp.exp(lc)
    e_neg = jnp.exp(-lc)
    at = -kk * jnp.exp(lc - logw)
    rt = r * e_pos
    bt = beta * e_neg
    kt = kmod * e_neg

    si = _iota2((R, R), 0)
    sj = _iota2((R, R), 1)
    same_s = (si >> lb) == (sj >> lb)
    strict = same_s & (sj < si)
    incl = same_s & (sj <= si)
    incl2 = jnp.concatenate([incl, incl], axis=1)
    lane_row = _iota2((1, LANES), 1)
    m_lo = jnp.where(lane_row < 64, 1.0, 0.0).astype(BF16)
    m_hi = jnp.where(lane_row < 64, 0.0, 1.0).astype(BF16)
    if nb > 1:
        seq_s = (_iota2((R, LANES), 0) & (C - 1)) >> lb
        seq_masks = [jnp.where(seq_s == b, 1.0, 0.0).astype(BF16) for b in range(nb)]

    def stack(xp):
        return jnp.concatenate([xp * m_lo, xp * m_hi], axis=0)

    def expand(xs_):
        if nb == 1:
            return xs_
        return jnp.concatenate([xs_ * seq_masks[b] for b in range(nb)], axis=1)

    probs = [(g, p) for g in range(G) for p in range(4)]
    NP = range(len(probs))

    def tile(x, i):
        g, p = probs[i]
        return x[g * C:(g + 1) * C, p * LANES:(p + 1) * LANES]

    def stacked(pair, i):
        return (stack(tile(pair[0], i)), stack(tile(pair[1], i)))

    at2, v2, bt2, kt2 = (_split2(x) for x in (at, v, bt, kt))
    rt_b = rt.astype(BF16)
    a_s = [stacked(at2, i) for i in NP]
    r_s = [stack(tile(rt_b, i)) for i in NP]
    v_s = [stacked(v2, i) for i in NP]
    bk_s = [_cat2([stacked(bt2, i), stacked(kt2, i)], 0) for i in NP]
    gm_a = [_dot3s(a_s[i], bk_s[i], _NT) for i in NP]
    gm_r = [lax.dot_general(r_s[i], bk_s[i][0], _NT, preferred_element_type=F32) for i in NP]
    a_ab = [jnp.where(strict, gm_a[i][:, 0:R], 0.0) for i in NP]
    a_ak = [_split2(jnp.where(strict, gm_a[i][:, R:2 * R], 0.0)) for i in NP]
    a_r = [jnp.where(incl2, gm_r[i], 0.0).astype(BF16) for i in NP]
    akv = [_dot3s(a_ak[i], v_s[i]) for i in NP]
    tinv = _tri_inverse(a_ab, lb)

    if prompt:
        sst = [s_sc[i] for i in NP]
    else:
        z64 = jnp.zeros((64, 64), F32)

        def pair_blockdiag(b, p):
            top = jnp.concatenate([sin_ref[b, 2 * p], z64], axis=1)
            bot = jnp.concatenate([z64, sin_ref[b, 2 * p + 1]], axis=1)
            return jnp.concatenate([top, bot], axis=0)

        sst = [jnp.concatenate([pair_blockdiag(g * nb + b, p) for b in range(nb)], axis=1) for g, p in probs]
    ss = [_split2(s) for s in sst]
    ar0_a = [_dot3s((expand(a_s[i][0]), expand(a_s[i][1])), ss[i], _NT) for i in NP]
    ar0_r = [lax.dot_general(expand(r_s[i]), ss[i][0], _NT, preferred_element_type=F32) for i in NP]
    us = [_split2(x) for x in _refined_solve(tinv, a_ab, [ar0_a[i] + akv[i] for i in NP])]
    uv = [_cat2([us[i], v_s[i]], 0) for i in NP]
    yst = [ar0_r[i] + jnp.dot(a_r[i], uv[i][0], preferred_element_type=F32) for i in NP]
    ys = [yst[i][0:C] + yst[i][C:R] for i in NP]
    for i in NP:
        g, p = probs[i]
        pl_lanes = tile(e_pos, i)
        plast = jnp.concatenate([pl_lanes[b * Lb + Lb - 1:b * Lb + Lb] for b in range(nb)], axis=1)
        bk_x = tuple(jnp.concatenate([expand(x[0:R]), expand(x[R:2 * R])], axis=0) for x in bk_s[i])
        snew = (sst[i] + _dot3s(uv[i], bk_x, _TN)) * plast
        if prompt:
            s_sc[i] = snew
            sout_ref[g, p] = snew
        else:
            for b in range(nb):
                sout_ref[g * nb + b, 2 * p] = snew[0:64, b * LANES:b * LANES + 64]
                sout_ref[g * nb + b, 2 * p + 1] = snew[64:128, b * LANES + 64:(b + 1) * LANES]

    y = jnp.concatenate([jnp.concatenate(ys[4 * g:4 * g + 4], axis=1) for g in range(G)], axis=0)
    inv_d = 1.0 / 64.0
    mean = _bdot(y, headsum) * inv_d
    yc = y - mean
    var = _bdot(yc * yc, headsum) * inv_d
    yn = yc * lax.rsqrt(var + GN_EPS) * lnw_ref[...] + lnb_ref[...]
    yn = yn + _bdot(r * kmod * rk_ref[...], headsum) * v
    y_ref[...] = (yn * out_gate).astype(y_ref.dtype).reshape(y_ref.shape)


def _wkv_params(p):
    z64 = jnp.zeros((64, RWKV_WIDTH), F32)
    w2 = jnp.concatenate([p["rwkv_w2"], z64], axis=0).astype(BF16)
    a2 = jnp.concatenate([z64, p["rwkv_a2"]], axis=0).astype(BF16)
    row = lambda a: a.reshape(1, -1)
    return [row(p["rwkv_mu"]), row(p["rwkv_w0"]), w2, row(p["rwkv_a0"]), a2, p["rwkv_g2"].astype(BF16),
            row(p["rwkv_k_k"]), row(p["rwkv_k_a"]), row(p["rwkv_r_k"]), row(p["rwkv_ln_w"]), row(p["rwkv_ln_b"])]


def _wkv_prompt(u, p, batch, seq, C=64, G=4):
    nc = seq // C
    params = _wkv_params(p)
    y, sout = pl.pallas_call(
        functools.partial(_wkv_kernel, C=C, Lb=C, prompt=True, G=G),
        grid=(batch // G, nc),
        in_specs=[pl.BlockSpec((G, C, SEG), lambda b, c: (b, c, 1))] + [_const_spec(a) for a in params],
        out_specs=[pl.BlockSpec((G, C, RWKV_WIDTH), lambda b, c: (b, c, 0)),
                   pl.BlockSpec((G, 4, LANES, LANES), lambda b, c: (b, 0, 0, 0))],
        out_shape=[jax.ShapeDtypeStruct((batch, seq, RWKV_WIDTH), BF16),
                   jax.ShapeDtypeStruct((batch, 4, LANES, LANES), F32)],
        scratch_shapes=[pltpu.VMEM((G, 8, SEG), F32), pltpu.VMEM((4 * G, LANES, LANES), F32)],
        compiler_params=_cparams("parallel", "arbitrary"),
        name="wkv_prompt",
    )(u.reshape(batch, seq, 2 * SEG), *params)
    return y.reshape(batch * seq, RWKV_WIDTH), sout


def _wkv_sample(u, hist, sin, p, Lb, C=64, G=2):
    n = u.shape[0]
    nb = C // Lb
    T = G * C
    params = _wkv_params(p)
    sel = _row_selector(G * nb, Lb, (Lb - 1,))
    return pl.pallas_call(
        functools.partial(_wkv_kernel, C=C, Lb=Lb, prompt=False, G=G),
        grid=(n // T,),
        in_specs=[pl.BlockSpec((T, SEG), lambda i: (i, 1)),
                  pl.BlockSpec((T, SEG), lambda i: (i, 0)),
                  pl.BlockSpec((G * nb, 8, 64, 64), lambda i: (i, 0, 0, 0)),
                  _const_spec(sel)]
                 + [_const_spec(a) for a in params],
        out_specs=[pl.BlockSpec((T, RWKV_WIDTH), lambda i: (i, 0)),
                   pl.BlockSpec((G * nb, 8, 64, 64), lambda i: (i, 0, 0, 0)),
                   pl.BlockSpec((G * nb, SEG), lambda i: (i, 0))],
        out_shape=[jax.ShapeDtypeStruct((n, RWKV_WIDTH), BF16),
                   jax.ShapeDtypeStruct(sin.shape, F32),
                   jax.ShapeDtypeStruct((n // Lb, SEG), F32)],
        compiler_params=_cparams("parallel"),
        name="wkv_sample",
    )(u, hist, sin, sel, *params)


def _outproj_kernel(x_ref, ys_ref, yr_ref, wo_ref, g_ref, wq_ref, x1_ref, q_ref):
    wo = wo_ref[...]
    x1 = x_ref[...] + (_bdot(ys_ref[...], wo[0:SSD_WIDTH]) + _bdot(yr_ref[...], wo[SSD_WIDTH:]))
    x1_ref[...] = x1
    q_ref[...] = _bdot(_rms(x1, g_ref[...]), wq_ref[...])


def _outproj(x, y_ssd, y_rw, w_out, g, w_q, tm):
    n, d = x.shape
    return pl.pallas_call(
        _outproj_kernel,
        grid=(n // tm,),
        in_specs=[pl.BlockSpec((tm, d), lambda i: (i, 0)),
                  pl.BlockSpec((tm, SSD_WIDTH), lambda i: (i, 0)),
                  pl.BlockSpec((tm, RWKV_WIDTH), lambda i: (i, 0)),
                  _const_spec(w_out), pl.BlockSpec((1, d), lambda i: (0, 0)), _const_spec(w_q)],
        out_specs=[pl.BlockSpec((tm, d), lambda i: (i, 0)), pl.BlockSpec((tm, d), lambda i: (i, 0))],
        out_shape=[jax.ShapeDtypeStruct((n, d), F32), jax.ShapeDtypeStruct((n, d), F32)],
        compiler_params=_cparams("parallel"),
        name="outproj_q",
    )(x, y_ssd, y_rw, w_out, g.reshape(1, d), w_q)


def _xattn_cache_kernel(q_ref, k_ref, v_ref, o_ref, *, seq_len):
    tq = q_ref.shape[0]
    nmem = tq // seq_len
    rows = N_MEM * XA_HEADS
    cols = XA_HEADS * tq
    q = q_ref[...] * (XA_HEAD_DIM ** -0.5)
    qblk = jnp.concatenate([q[:, h * XA_HEAD_DIM:(h + 1) * XA_HEAD_DIM] for h in range(XA_HEADS)], axis=0)
    key_head = _iota2((cols, rows), 1) & (XA_HEADS - 1)
    qrow = _iota2((cols, 1), 0)
    head_ok = key_head == (qrow >> int(math.log2(tq)))
    out_mem = _iota2((tq, 1), 0) >> int(math.log2(seq_len))
    o = None
    for m in range(nmem):
        kall = k_ref[m].reshape(rows, XA_HEAD_DIM)
        vall = v_ref[m].reshape(rows, XA_HEAD_DIM)
        s = jnp.where(head_ok, _bdot(qblk, kall, _NT), -jnp.inf)
        e = jnp.exp(s - jnp.max(s, axis=-1, keepdims=True))
        pr = e / jnp.sum(e, axis=-1, keepdims=True)
        om = _bdot(pr, vall)
        om = jnp.concatenate([om[h * tq:(h + 1) * tq] for h in range(XA_HEADS)], axis=1)
        o = om if o is None else jnp.where(out_mem == m, om, o)
    o_ref[...] = o.astype(o_ref.dtype)


def _xattn_cache(q, ck, cv, tq, seq_len):
    n, d = q.shape
    nmem = tq // seq_len
    row_spec = pl.BlockSpec((tq, d), lambda i: (i, 0))
    kv_spec = pl.BlockSpec((nmem, N_MEM, XA_HEADS, XA_HEAD_DIM), lambda i: (i, 0, 0, 0))
    return pl.pallas_call(
        functools.partial(_xattn_cache_kernel, seq_len=seq_len),
        grid=(n // tq,),
        in_specs=[row_spec, kv_spec, kv_spec],
        out_specs=row_spec,
        out_shape=jax.ShapeDtypeStruct((n, d), BF16),
        compiler_params=_cparams("parallel"),
        name="xattn_cache",
    )(q, ck, cv)


FF_SUB = 256
FF_GROUP = 4


def _ffn_kernel(*refs, tm, Lb, prompt, blocks_per_seq):
    assert not prompt
    x1_ref, o_ref, wo_ref, hist_ref, g_ref, wu_ref, cw_ref, cb_ref, wd_ref, fg_ref, y_ref, s_ref = refs
    x2 = x1_ref[...] + jnp.dot(o_ref[...], wo_ref[...], preferred_element_type=F32)
    _ffn_body(x2, hist_ref, g_ref, wu_ref, cw_ref, cb_ref, wd_ref, fg_ref, y_ref, s_ref, None,
              tm=tm, Lb=Lb, prompt=False, blocks_per_seq=blocks_per_seq)


def _ffn_body(x, hist_ref, g_ref, wu_ref, cw_ref, cb_ref, wd_ref, fg_ref, y_ref, s_ref, carry_sc,
              *, tm, Lb, prompt, blocks_per_seq):
    if prompt:
        @pl.when(pl.program_id(0) % blocks_per_seq == 0)
        def _():
            carry_sc[...] = jnp.zeros_like(carry_sc)

    hn = _rms(x, g_ref[...]).astype(BF16)
    nsub = D_FF // FF_SUB
    r8 = _iota2((8, 1), 0)
    tpos = _iota2((tm, 1), 0) & (Lb - 1)
    if not prompt:
        trow = _iota2((tm, tm // Lb), 0)
        seq0 = _iota2((tm, tm // Lb), 1) * Lb
        at_row0 = (trow == seq0).astype(BF16)
        at_row1 = (trow == seq0 + 1).astype(BF16)

    def conv(up, cols):
        cw = cw_ref[:, cols]
        cb = cb_ref[:, cols]
        if prompt:
            c8 = carry_sc[:, cols]
            carry_sc[:, cols] = up[tm - 8:tm]
            s_ref[0, :, cols] = up[tm - 2:tm]
            top = up[0:8]
            p1 = jnp.where(r8 >= 1, pltpu.roll(top, 1, 0), c8[7:8])
            p2 = jnp.where(r8 >= 2, pltpu.roll(top, 2, 0), jnp.where(r8 == 0, c8[6:7], c8[7:8]))
            head = (cw[0:1] * p2 + cw[1:2] * p1 + cw[2:3] * top) + cb
            body = (cw[0:1] * pltpu.roll(up, 2, 0) + cw[1:2] * pltpu.roll(up, 1, 0) + cw[2:3] * up) + cb
            return jnp.concatenate([head, body[8:]], axis=0)
        for b in range(tm // Lb):
            s_ref[b, :, cols] = up[b * Lb + Lb - 2:b * Lb + Lb]
        st0, st1 = hist_ref[0, :, cols], hist_ref[1, :, cols]
        h1 = _maskdot(at_row0, st1)
        h2 = _maskdot(at_row0, st0) + _maskdot(at_row1, st1)
        return (cw[0:1] * _shifted(up, 2, tpos, h2) + cw[1:2] * _shifted(up, 1, tpos, h1) + cw[2:3] * up) + cb

    def up_proj(k):
        cg = slice(k * FF_SUB, (k + 1) * FF_SUB)
        cv = slice(D_FF + k * FF_SUB, D_FF + (k + 1) * FF_SUB)
        return (jnp.dot(hn, wu_ref[:, cg], preferred_element_type=F32),
                jnp.dot(hn, wu_ref[:, cv], preferred_element_type=F32))

    def gate_mul(k, ug, uv):
        cg = slice(k * FF_SUB, (k + 1) * FF_SUB)
        cv = slice(D_FF + k * FF_SUB, D_FF + (k + 1) * FF_SUB)
        gate = conv(ug, cg)
        return (gate * _sigmoid(gate) * conv(uv, cv)).astype(BF16)

    acc = x
    ups = {}
    acts = []
    for k in range(nsub + 1):
        if k < nsub:
            ups[k] = up_proj(k)
        if k >= 1:
            acts.append(gate_mul(k - 1, *ups.pop(k - 1)))
            if len(acts) == FF_GROUP or k == nsub:
                lo = (k - len(acts)) * FF_SUB
                acc = acc + jnp.dot(jnp.concatenate(acts, axis=1), wd_ref[lo:k * FF_SUB, :],
                                    preferred_element_type=F32)
                acts = []
    y_ref[...] = _rms(acc, fg_ref[...])


def _resident_spec(a):
    nd = a.ndim
    return pl.BlockSpec(a.shape, lambda *_: (0,) * nd, pipeline_mode=pl.Buffered(1))


def _ffn_short(x1, o, w_o, hist, p, final_g, tm, Lb):
    n, d = x1.shape
    w_up = p["ffn_w_up"].astype(BF16)
    w_down = p["ffn_w_down"].astype(BF16)
    cw, cb = p["ffn_conv_w"], p["ffn_conv_b"].reshape(1, 2 * D_FF)
    row_spec = pl.BlockSpec((tm, d), lambda i: (i, 0))
    seq_spec = lambda lead: pl.BlockSpec((lead, tm // Lb, 2 * D_FF), lambda i: (0, i, 0))
    consts = [p["norm_ffn_w"].reshape(1, d), w_up, cw, cb, w_down, final_g.reshape(1, d)]
    return pl.pallas_call(
        functools.partial(_ffn_kernel, tm=tm, Lb=Lb, prompt=False, blocks_per_seq=1),
        grid=(n // tm,),
        in_specs=[row_spec, row_spec, _resident_spec(w_o), seq_spec(2)] + [_resident_spec(a) for a in consts],
        out_specs=[row_spec, pl.BlockSpec((tm // Lb, 2, 2 * D_FF), lambda i: (i, 0, 0))],
        out_shape=[jax.ShapeDtypeStruct((n, d), F32), jax.ShapeDtypeStruct((n // Lb, 2, 2 * D_FF), F32)],
        compiler_params=_cparams("arbitrary"),
        name="convffn",
    )(x1, o, w_o, hist, *consts)


def _post_mix_kernel(x_ref, ys_ref, yr_ref, k_ref, v_ref, wo_ref, gxa_ref, wq_ref, wao_ref,
                     g_ref, wu_ref, cw_ref, cb_ref, wd_ref, fg_ref, y_ref, s_ref, carry_sc,
                     *, tm, seq_len):
    wo = wo_ref[...]
    x1 = x_ref[...] + (jnp.dot(ys_ref[...], wo[0:SSD_WIDTH], preferred_element_type=F32)
                       + jnp.dot(yr_ref[...], wo[SSD_WIDTH:], preferred_element_type=F32))
    q = (_bdot(_rms(x1, gxa_ref[...]), wq_ref[...]) * (XA_HEAD_DIM ** -0.5)).astype(BF16)
    outs = []
    for h in range(XA_HEADS):
        sl = slice(h * XA_HEAD_DIM, (h + 1) * XA_HEAD_DIM)
        s = lax.dot_general(q[:, sl], k_ref[0, :, sl].astype(BF16), _NT, preferred_element_type=F32)
        s = s - jnp.max(s, axis=-1, keepdims=True)
        e = jnp.exp(s)
        pr = e / jnp.sum(e, axis=-1, keepdims=True)
        outs.append(_bdot(pr, v_ref[0, :, sl]))
    x2 = x1 + _bdot(jnp.concatenate(outs, axis=1), wao_ref[...])
    _ffn_body(x2, None, g_ref, wu_ref, cw_ref, cb_ref, wd_ref, fg_ref, y_ref, s_ref, carry_sc,
              tm=tm, Lb=seq_len, prompt=True, blocks_per_seq=seq_len // tm)


def _post_mix(x, y_ssd, y_rw, mk, mv, w_out, g_xa, w_q, w_o, p, final_g, tm, seq_len):
    n, d = x.shape
    bps = seq_len // tm
    row = lambda w: pl.BlockSpec((tm, w), lambda i: (i, 0))
    kv_spec = pl.BlockSpec((1, N_MEM, d), lambda i: (i // bps, 0, 0))
    consts = [w_out, g_xa.reshape(1, d), w_q, w_o, p["norm_ffn_w"].reshape(1, d), p["ffn_w_up"].astype(BF16),
              p["ffn_conv_w"], p["ffn_conv_b"].reshape(1, 2 * D_FF), p["ffn_w_down"].astype(BF16),
              final_g.reshape(1, d)]
    y, st = pl.pallas_call(
        functools.partial(_post_mix_kernel, tm=tm, seq_len=seq_len),
        grid=(n // tm,),
        in_specs=[row(d), row(SSD_WIDTH), row(RWKV_WIDTH), kv_spec, kv_spec] + [_resident_spec(a) for a in consts],
        out_specs=[row(d), pl.BlockSpec((1, 2, 2 * D_FF), lambda i: (i, 0, 0))],
        out_shape=[jax.ShapeDtypeStruct((n, d), F32), jax.ShapeDtypeStruct((n // tm, 2, 2 * D_FF), F32)],
        scratch_shapes=[pltpu.VMEM((8, 2 * D_FF), F32)],
        compiler_params=_cparams("arbitrary"),
        name="post_mix",
    )(x, y_ssd, y_rw, mk, mv, *consts)
    return y, st[bps - 1::bps]


def _pair_blocks(sbd):
    b = sbd.shape[0]
    s0 = sbd[:, :, 0:64, 0:64]
    s1 = sbd[:, :, 64:128, 64:128]
    return jnp.stack([s0, s1], axis=2).reshape(b, 8, 64, 64)


def _hist_rows(state, seq_len, k):
    b, w, c = state.shape
    cols = [state[:, w + t - k] if t < k else jnp.zeros((b, c), state.dtype) for t in range(seq_len)]
    return jnp.stack(cols, axis=1).reshape(b * seq_len, c)


def kernel(x_prompt, x_sample, mem_prompt, state_ssm_conv, state_ssm, state_shift, state_wkv, state_ffn_conv, cache_mem_k, cache_mem_v, norm_mix_w, w_in, ssd_conv_w, ssd_conv_b, ssd_dt_bias, ssd_a_log, ssd_d, ssd_norm_w, rwkv_mu, rwkv_w0, rwkv_w2, rwkv_a0, rwkv_a2, rwkv_g2, rwkv_k_k, rwkv_k_a, rwkv_r_k, rwkv_ln_w, rwkv_ln_b, w_out, norm_xa_w, mem_norm_w, xa_w_q, xa_w_k, xa_w_v, xa_w_o, norm_ffn_w, ffn_w_up, ffn_conv_w, ffn_conv_b, ffn_w_down, final_norm_w):
    depth = w_in.shape[0]
    assert depth == 1, "final rmsnorm is fused into the (single) layer's ConvFFN kernel"
    bp, lp, d = x_prompt.shape
    bs, ls, _ = x_sample.shape
    i = 0
    p = dict(ssd_conv_w=ssd_conv_w[i], ssd_conv_b=ssd_conv_b[i], ssd_dt_bias=ssd_dt_bias[i],
             ssd_a_log=ssd_a_log[i], ssd_d=ssd_d[i], ssd_norm_w=ssd_norm_w[i], rwkv_mu=rwkv_mu[i],
             rwkv_w0=rwkv_w0[i], rwkv_w2=rwkv_w2[i], rwkv_a0=rwkv_a0[i], rwkv_a2=rwkv_a2[i],
             rwkv_g2=rwkv_g2[i], rwkv_k_k=rwkv_k_k[i], rwkv_k_a=rwkv_k_a[i],
             rwkv_r_k=rwkv_r_k[i].reshape(-1), rwkv_ln_w=rwkv_ln_w[i], rwkv_ln_b=rwkv_ln_b[i],
             norm_ffn_w=norm_ffn_w[i], ffn_w_up=ffn_w_up[i], ffn_conv_w=ffn_conv_w[i],
             ffn_conv_b=ffn_conv_b[i], ffn_w_down=ffn_w_down[i])

    w_in_p = jnp.concatenate([w_in[i][:, :SSD_PROJ], jnp.zeros((d, SEG - SSD_PROJ), F32),
                              w_in[i][:, SSD_PROJ:]], axis=1).astype(BF16)
    w_out_b = w_out[i].astype(BF16)
    w_q_b = (xa_w_q[i]).astype(BF16)
    w_o_b = xa_w_o[i].astype(BF16)
    w_kv_b = jnp.concatenate([xa_w_k[i], xa_w_v[i]], axis=1).astype(BF16)

    xp = x_prompt.reshape(bp * lp, d)
    xs = x_sample.reshape(bs * ls, d)

    mk, mv, mk4, mv4 = _mem_kv(mem_prompt.reshape(bp * N_MEM, d), mem_norm_w[i], w_kv_b, 512)
    mk = mk.reshape(bp, N_MEM, d)
    mv = mv.reshape(bp, N_MEM, d)
    tm_p = 1024
    u_p, dt_p, tail_p = _norm_proj(xp, norm_mix_w[i], w_in_p, tm_p, BF16)
    y_ssd_p, ssm_p = _ssd_prompt(u_p, dt_p, p, bp, lp)
    y_rw_p, wkv_bd_p = _wkv_prompt(u_p, p, bp, lp)
    y_p, ffn_conv_p = _post_mix(xp, y_ssd_p, y_rw_p, mk, mv, w_out_b, norm_xa_w[i], w_q_b, w_o_b, p,
                                final_norm_w, 512, lp)
    seq_tail = tail_p[lp // tm_p - 1::lp // tm_p]
    ssm_conv_p = seq_tail[:, 5:, SSD_WIDTH:SSD_WIDTH + SSD_XBC]
    shift_p = seq_tail[:, 7, SEG:]

    u_s, dt_s, _ = _norm_proj(xs, norm_mix_w[i], w_in_p, 512, F32)
    conv_state = state_ssm_conv[i]
    hist_conv = jnp.stack([_hist_rows(conv_state, ls, k) for k in (1, 2, 3)], axis=0)
    assert ls >= 3, "the new conv state is read from the last three rows of each sample sequence"
    y_ssd_s, ssm_s, conv_rows_s = _ssd_sample(u_s, dt_s, hist_conv, state_ssm[i].reshape(bs, 4, LANES, LANES), p, ls)
    ssm_conv_s = conv_rows_s.reshape(bs, 3, SSD_XBC)
    hist_shift = _hist_rows(state_shift[i][:, None, :], ls, 1)
    y_rw_s, wkv_s, shift_s = _wkv_sample(u_s, hist_shift, state_wkv[i], p, ls)
    x1_s, q_s = _outproj(xs, y_ssd_s, y_rw_s, w_out_b, norm_xa_w[i], w_q_b, 512)
    o_s = _xattn_cache(q_s, cache_mem_k[i], cache_mem_v[i], 16, ls)
    y_s, ffn_conv_s = _ffn_short(x1_s, o_s, w_o_b, jnp.swapaxes(state_ffn_conv[i], 0, 1), p, final_norm_w, 256, ls)

    e = lambda a: a[None]
    return (y_p.reshape(bp, lp, d), y_s.reshape(bs, ls, d),
            e(ssm_conv_p), e(ssm_conv_s),
            e(ssm_p.reshape(bp, SSD_HEADS, 64, SSD_STATE)), e(ssm_s.reshape(bs, SSD_HEADS, 64, SSD_STATE)),
            e(shift_p), e(shift_s),
            e(_pair_blocks(wkv_bd_p)), e(wkv_s),
            e(ffn_conv_p), e(ffn_conv_s),
            e(mk4.reshape(bp, N_MEM, XA_HEADS, XA_HEAD_DIM)), e(mv4.reshape(bp, N_MEM, XA_HEADS, XA_HEAD_DIM)))
```

```python
import functools
import math

import jax
import jax.numpy as jnp
from jax import lax
from jax.experimental import pallas as pl
from jax.experimental.pallas import tpu as pltpu

F32 = jnp.float32
BF16 = jnp.bfloat16

D_MODEL = 1024
N_MEM = 256
XA_HEADS = 4
XA_HEAD_DIM = D_MODEL // XA_HEADS
SSD_WIDTH = 512
SSD_HEADS = 8
SSD_STATE = 128
SSD_XBC = 1024
SSD_PROJ = SSD_WIDTH + SSD_XBC + SSD_HEADS
RWKV_WIDTH = 512
RWKV_PROJ = 1792
D_FF = 2816
EPS = 1e-6
GN_EPS = 64e-5

LANES = 128
SEG = 1792
DT_OFF = SSD_WIDTH + SSD_XBC
VMEM_LIMIT_BYTES = 56 * 1024 * 1024


def _cparams(*sem):
    return pltpu.CompilerParams(dimension_semantics=sem, vmem_limit_bytes=VMEM_LIMIT_BYTES)


_NN = (((1,), (0,)), ((), ()))
_NT = (((1,), (1,)), ((), ()))
_TN = (((0,), (0,)), ((), ()))


def _bdot(a, b, dims=_NN):
    return lax.dot_general(a.astype(BF16), b.astype(BF16), dims, preferred_element_type=F32)


def _split2(x):
    hi = x.astype(BF16)
    lo = (x - hi.astype(F32)).astype(BF16)
    return hi, lo


def _maskdot(m_bf16, x):
    hi = x.astype(BF16)
    r1 = x - hi.astype(F32)
    mid = r1.astype(BF16)
    lo = (r1 - mid.astype(F32)).astype(BF16)
    return jnp.dot(jnp.concatenate([m_bf16, m_bf16, m_bf16], axis=1),
                   jnp.concatenate([hi, mid, lo], axis=0), preferred_element_type=F32)


def _sigmoid(x):
    return 0.5 + 0.5 * jnp.tanh(0.5 * x)


def _softplus(x):
    return jnp.maximum(x, 0.0) + jnp.log1p(jnp.exp(-jnp.abs(x)))


def _rms(x, g):
    return x * lax.rsqrt(jnp.mean(x * x, axis=-1, keepdims=True) + EPS) * g


def _iota2(shape, dim):
    return lax.broadcasted_iota(jnp.int32, shape, dim)


def _shifted(x, k, tpos, hist):
    return jnp.where(tpos >= k, pltpu.roll(x, k, 0), hist)


def _norm_proj_kernel(x_ref, g_ref, wt_ref, o_ref, dt_ref, tail_ref, hn_sc):
    j = pl.program_id(1)

    @pl.when(j == 0)
    def _():
        hn_sc[...] = _rms(x_ref[...], g_ref[...]).astype(BF16)

    acc = lax.dot_general(hn_sc[...], wt_ref[...], _NT, preferred_element_type=F32)
    o_ref[...] = acc.astype(o_ref.dtype)
    tail_ref[0] = acc[acc.shape[0] - 8:]

    @pl.when(j == 0)
    def _():
        dt_ref[...] = acc[:, DT_OFF:DT_OFF + LANES]


def _norm_proj(x, g, wt_bf16, tm, out_dtype):
    n, d = x.shape
    return pl.pallas_call(
        _norm_proj_kernel,
        grid=(n // tm, 2),
        in_specs=[pl.BlockSpec((tm, d), lambda i, j: (i, 0)),
                  pl.BlockSpec((1, d), lambda i, j: (0, 0)),
                  pl.BlockSpec((SEG, d), lambda i, j: (j, 0))],
        out_specs=[pl.BlockSpec((tm, SEG), lambda i, j: (i, j)),
                   pl.BlockSpec((tm, LANES), lambda i, j: (i, 0)),
                   pl.BlockSpec((1, 8, SEG), lambda i, j: (i, 0, j))],
        out_shape=[jax.ShapeDtypeStruct((n, 2 * SEG), out_dtype),
                   jax.ShapeDtypeStruct((n, LANES), F32),
                   jax.ShapeDtypeStruct((n // tm, 8, 2 * SEG), F32)],
        scratch_shapes=[pltpu.VMEM((tm, d), BF16)],
        compiler_params=_cparams("parallel", "arbitrary"),
        name="norm_proj",
    )(x, g.reshape(1, d), wt_bf16)


def _mem_kv_kernel(x_ref, g_ref, w_ref, k_ref, v_ref, k4_ref, v4_ref):
    hn = _rms(x_ref[...], g_ref[...]).astype(BF16)
    d = k_ref.shape[1]
    k = jnp.dot(hn, w_ref[:, 0:d], preferred_element_type=F32)
    v = jnp.dot(hn, w_ref[:, d:2 * d], preferred_element_type=F32)
    k_ref[...] = k
    v_ref[...] = v
    for h in range(XA_HEADS):
        sl = slice(h * XA_HEAD_DIM, (h + 1) * XA_HEAD_DIM)
        k4_ref[:, h, :] = k[:, sl]
        v4_ref[:, h, :] = v[:, sl]


def _mem_kv(x, g, w_kv_bf16, tm):
    n, d = x.shape
    row_spec = pl.BlockSpec((tm, d), lambda i: (i, 0))
    head_spec = pl.BlockSpec((tm, XA_HEADS, XA_HEAD_DIM), lambda i: (i, 0, 0))
    return pl.pallas_call(
        _mem_kv_kernel,
        grid=(n // tm,),
        in_specs=[row_spec, pl.BlockSpec((1, d), lambda i: (0, 0)), _const_spec(w_kv_bf16)],
        out_specs=[row_spec, row_spec, head_spec, head_spec],
        out_shape=[jax.ShapeDtypeStruct((n, d), F32)] * 2
                  + [jax.ShapeDtypeStruct((n, XA_HEADS, XA_HEAD_DIM), F32)] * 2,
        compiler_params=_cparams("parallel"),
        name="mem_kv",
    )(x, g.reshape(1, d), w_kv_bf16)


def _ssd_kernel(*refs, Q, Lb, prompt, G=1):
    if not prompt:
        return _ssd_block(*refs, Q=Q, Lb=Lb, prompt=False)
    (u_ref, dt_ref, cw_ref, cb_ref, dtb_ref, an_ref, dsk_ref, nw_ref, y_ref, hout_ref, carry_sc, h_sc) = refs

    @pl.when(pl.program_id(1) == 0)
    def _():
        carry_sc[...] = jnp.zeros_like(carry_sc)
        h_sc[...] = jnp.zeros_like(h_sc)

    for gi in range(G):
        _ssd_block(u_ref.at[gi], dt_ref.at[gi], cw_ref, cb_ref, dtb_ref, an_ref, dsk_ref, nw_ref,
                   y_ref.at[gi], hout_ref.at[pl.ds(gi, 1)], carry_sc.at[gi], h_sc.at[pl.ds(4 * gi, 4)],
                   Q=Q, Lb=Lb, prompt=True)


def _ssd_block(*refs, Q, Lb, prompt):
    ns = Q // Lb
    lb = int(math.log2(Lb))
    rows = _iota2((Q, 1), 0)
    tpos = rows & (Lb - 1)
    if prompt:
        (u_ref, dt_ref, cw_ref, cb_ref, dtb_ref, an_ref, dsk_ref, nw_ref,
         y_ref, hout_ref, carry_sc, h_sc) = refs
    else:
        (u_ref, dt_ref, hist_ref, hin_ref, sel_ref, cw_ref, cb_ref, dtb_ref, an_ref, dsk_ref, nw_ref,
         y_ref, hout_ref, cs_ref) = refs
    u = u_ref[...].astype(F32)
    z = u[:, 0:SSD_WIDTH]
    z_gate = z * _sigmoid(z)
    x = u[:, SSD_WIDTH:SSD_WIDTH + SSD_XBC]
    if prompt:
        c8 = carry_sc[...]
        h1 = c8[7:8]
        h2 = jnp.where(rows == 0, c8[6:7], c8[7:8])
        h3 = jnp.where(rows == 0, c8[5:6], jnp.where(rows == 1, c8[6:7], c8[7:8]))
        carry_sc[...] = x[Q - 8:Q]
    else:
        cs_ref[...] = _maskdot(sel_ref[...], x)
        h1, h2, h3 = hist_ref[0], hist_ref[1], hist_ref[2]
    cw = cw_ref[...]
    xc = (cw[0:1] * _shifted(x, 3, tpos, h3) + cw[1:2] * _shifted(x, 2, tpos, h2)
          + cw[2:3] * _shifted(x, 1, tpos, h1) + cw[3:4] * x) + cb_ref[...]
    xc = xc * _sigmoid(xc)
    dtr = dt_ref[...]
    xs = xc[:, 0:SSD_WIDTH]
    bm = xc[:, SSD_WIDTH:SSD_WIDTH + 2 * SSD_STATE]
    cm = xc[:, SSD_WIDTH + 2 * SSD_STATE:]

    dt = _softplus(dtr + dtb_ref[...])
    da = dt * an_ref[...]

    ri = _iota2((Q, Q), 0)
    ci = _iota2((Q, Q), 1)
    same = (ri >> lb) == (ci >> lb)
    tril = same & (ci <= ri)
    sel = ci == (((ri >> lb) << lb) + (Lb - 1))
    acs = _maskdot(tril.astype(BF16), da)
    acs_t = acs.T
    acs_last = _maskdot(sel.astype(BF16), acs)
    dec_end = jnp.exp(acs_last - acs)
    eacs = jnp.exp(acs)
    seqid = rows >> lb

    lane = _iota2((Q, LANES), 1)
    lo_half = lane < 64
    prow = _iota2((LANES, LANES), 0)
    dsk = dsk_ref[...]

    ys = []
    for q in range(4):
        g = q // 2
        h0, h1i = 2 * q, 2 * q + 1
        if q % 2 == 0:
            cg = cm[:, g * SSD_STATE:(g + 1) * SSD_STATE]
            bg = bm[:, g * SSD_STATE:(g + 1) * SSD_STATE]
            cb_g = _bdot(cg, bg, _NT)
            if ns == 1:
                cexp, bexp = cg.astype(BF16), bg.astype(BF16)
            else:
                cexp = jnp.concatenate([jnp.where(seqid == b, cg, 0.0).astype(BF16) for b in range(ns)], axis=1)
                bexp = jnp.concatenate([jnp.where(seqid == b, bg, 0.0).astype(BF16) for b in range(ns)], axis=1)
        m0 = jnp.where(tril, cb_g * jnp.exp(acs[:, h0:h0 + 1] - acs_t[h0:h0 + 1, :]), 0.0)
        m1 = jnp.where(tril, cb_g * jnp.exp(acs[:, h1i:h1i + 1] - acs_t[h1i:h1i + 1, :]), 0.0)
        xp = xs[:, q * LANES:(q + 1) * LANES]
        xdt = xp * jnp.where(lo_half, dt[:, h0:h0 + 1], dt[:, h1i:h1i + 1])
        xdt0 = jnp.where(lo_half, xdt, 0.0)
        xdt1 = jnp.where(lo_half, 0.0, xdt)
        ydiag = _bdot(jnp.concatenate([m0, m1], axis=1), jnp.concatenate([xdt0, xdt1], axis=0))
        if prompt:
            hst = h_sc[q]
        else:
            hst = jnp.concatenate([hin_ref[b, q] for b in range(ns)], axis=1)
        ecs = jnp.where(lo_half, eacs[:, h0:h0 + 1], eacs[:, h1i:h1i + 1])
        yoff = _bdot(cexp, hst, _NT) * ecs
        xd = xdt * jnp.where(lo_half, dec_end[:, h0:h0 + 1], dec_end[:, h1i:h1i + 1])
        incr = _bdot(xd, bexp, _TN)
        scales = []
        for b in range(ns):
            r = b * Lb + Lb - 1
            e = eacs[r:r + 1, :]
            scales.append(jnp.where(prow < 64, e[:, h0:h0 + 1], e[:, h1i:h1i + 1]))
        scale = scales[0] if ns == 1 else jnp.concatenate(scales, axis=1)
        hnew = hst * scale + incr
        if prompt:
            h_sc[q] = hnew
            hout_ref[0, q] = hnew
        else:
            for b in range(ns):
                hout_ref[b, q] = hnew[:, b * LANES:(b + 1) * LANES]
        ys.append(ydiag + yoff + dsk[:, q * LANES:(q + 1) * LANES] * xp)

    y = jnp.concatenate(ys, axis=1)
    y = y * z_gate
    half = SSD_WIDTH // 2
    outs = []
    for g in range(2):
        yg = y[:, g * half:(g + 1) * half]
        outs.append(yg * lax.rsqrt(jnp.mean(yg * yg, axis=-1, keepdims=True) + EPS))
    y_ref[...] = (jnp.concatenate(outs, axis=1) * nw_ref[...]).astype(y_ref.dtype)


def _ssd_params(p):
    an = jnp.zeros((1, LANES), F32).at[0, :SSD_HEADS].set(-jnp.exp(p["ssd_a_log"]))
    dtb = jnp.zeros((1, LANES), F32).at[0, :SSD_HEADS].set(p["ssd_dt_bias"])
    dsk = jnp.repeat(p["ssd_d"], SSD_WIDTH // SSD_HEADS).reshape(1, SSD_WIDTH)
    return [p["ssd_conv_w"], p["ssd_conv_b"].reshape(1, SSD_XBC), dtb, an, dsk,
            p["ssd_norm_w"].reshape(1, SSD_WIDTH)]


def _const_spec(a):
    nd = a.ndim
    return pl.BlockSpec(a.shape, lambda *_: (0,) * nd)


def _ssd_prompt(u, dt, p, batch, seq, Q=128, G=4):
    nc = seq // Q
    params = _ssd_params(p)
    y, hout = pl.pallas_call(
        functools.partial(_ssd_kernel, Q=Q, Lb=Q, prompt=True, G=G),
        grid=(batch // G, nc),
        in_specs=[pl.BlockSpec((G, Q, SEG), lambda b, c: (b, c, 0)),
                  pl.BlockSpec((G, Q, LANES), lambda b, c: (b, c, 0))] + [_const_spec(a) for a in params],
        out_specs=[pl.BlockSpec((G, Q, SSD_WIDTH), lambda b, c: (b, c, 0)),
                   pl.BlockSpec((G, 4, LANES, LANES), lambda b, c: (b, 0, 0, 0))],
        out_shape=[jax.ShapeDtypeStruct((batch, seq, SSD_WIDTH), BF16),
                   jax.ShapeDtypeStruct((batch, 4, LANES, LANES), F32)],
        scratch_shapes=[pltpu.VMEM((G, 8, SSD_XBC), F32), pltpu.VMEM((4 * G, LANES, LANES), F32)],
        compiler_params=_cparams("parallel", "arbitrary"),
        name="ssd_prompt",
    )(u.reshape(batch, seq, 2 * SEG), dt.reshape(batch, seq, LANES), *params)
    return y.reshape(batch * seq, SSD_WIDTH), hout


def _row_selector(nseq, Lb, offsets):
    k = len(offsets)
    rows = jnp.arange(nseq * k)
    target = (rows // k) * Lb + jnp.asarray(offsets)[rows % k]
    return (target[:, None] == jnp.arange(nseq * Lb)[None, :]).astype(BF16)


def _ssd_sample(u, dt, hist, hin, p, Lb, Q=64):
    n = u.shape[0]
    ns = Q // Lb
    params = _ssd_params(p)
    sel = _row_selector(ns, Lb, (Lb - 3, Lb - 2, Lb - 1))
    return pl.pallas_call(
        functools.partial(_ssd_kernel, Q=Q, Lb=Lb, prompt=False),
        grid=(n // Q,),
        in_specs=[pl.BlockSpec((Q, SEG), lambda i: (i, 0)),
                  pl.BlockSpec((Q, LANES), lambda i: (i, 0)),
                  pl.BlockSpec((3, Q, SSD_XBC), lambda i: (0, i, 0)),
                  pl.BlockSpec((ns, 4, LANES, LANES), lambda i: (i, 0, 0, 0)),
                  _const_spec(sel)]
                 + [_const_spec(a) for a in params],
        out_specs=[pl.BlockSpec((Q, SSD_WIDTH), lambda i: (i, 0)),
                   pl.BlockSpec((ns, 4, LANES, LANES), lambda i: (i, 0, 0, 0)),
                   pl.BlockSpec((ns * 3, SSD_XBC), lambda i: (i, 0))],
        out_shape=[jax.ShapeDtypeStruct((n, SSD_WIDTH), BF16),
                   jax.ShapeDtypeStruct(hin.shape, F32),
                   jax.ShapeDtypeStruct((n // Lb * 3, SSD_XBC), F32)],
        compiler_params=_cparams("parallel"),
        name="ssd_sample",
    )(u, dt, hist, hin, sel, *params)


def _dot3s(a, b, dims=_NN):
    ka = dims[0][0][0]
    kb = dims[0][1][0]
    lhs = jnp.concatenate([a[0], a[0], a[1]], axis=ka)
    rhs = jnp.concatenate([b[0], b[1], b[0]], axis=kb)
    return lax.dot_general(lhs, rhs, dims, preferred_element_type=F32)


def _cat2(parts, axis):
    return (jnp.concatenate([p[0] for p in parts], axis=axis), jnp.concatenate([p[1] for p in parts], axis=axis))


def _tri_inverse(mats, lb):
    n = mats[0].shape[0]
    ri = _iota2((n, n), 0)
    ci = _iota2((n, n), 1)
    off1 = ((ri >> 1) == (ci >> 1)) & ((ri & 1) == 1) & ((ci & 1) == 0)
    eye = jnp.where(ri == ci, 1.0, 0.0)
    ts = [eye + jnp.where(off1, a, 0.0) for a in mats]
    for lvl in range(1, lb):
        m = 1 << lvl
        off = ((ri >> (lvl + 1)) == (ci >> (lvl + 1))) & ((ri & (2 * m - 1)) >= m) & ((ci & (2 * m - 1)) < m)
        tsb = [t.astype(BF16) for t in ts]
        ws = [jnp.dot(jnp.where(off, a, 0.0).astype(BF16), tb, preferred_element_type=F32)
              for a, tb in zip(mats, tsb)]
        ts = [t + jnp.dot(tb, w.astype(BF16), preferred_element_type=F32) for t, tb, w in zip(ts, tsb, ws)]
    return [t.astype(BF16) for t in ts]


def _refined_solve(tinvs, mats, rhss):
    d = functools.partial(jnp.dot, preferred_element_type=F32)
    n = range(len(mats))
    rs = [_split2(r) for r in rhss]
    u0 = [d(jnp.concatenate([tinvs[i], tinvs[i]], axis=1), jnp.concatenate(rs[i], axis=0)) for i in n]
    au = [_dot3s(_split2(mats[i]), _split2(u0[i])) for i in n]
    res = [((rhss[i] - u0[i]) + au[i]).astype(BF16) for i in n]
    return [u0[i] + d(tinvs[i], res[i]) for i in n]


def _wkv_kernel(*refs, C, Lb, prompt, G):
    nb = C // Lb
    lb = int(math.log2(Lb))
    R = 2 * C
    T = G * C
    if prompt:
        (u_ref, mu_ref, w0_ref, w2_ref, a0_ref, a2_ref, g2_ref, kk_ref, ka_ref, rk_ref, lnw_ref, lnb_ref,
         y_ref, sout_ref, carry_sc, s_sc) = refs
        c = pl.program_id(1)

        @pl.when(c == 0)
        def _():
            carry_sc[...] = jnp.zeros_like(carry_sc)
            s_sc[...] = jnp.zeros_like(s_sc)
    else:
        (u_ref, hist_ref, sin_ref, sel_ref, mu_ref, w0_ref, w2_ref, a0_ref, a2_ref, g2_ref, kk_ref, ka_ref,
         rk_ref, lnw_ref, lnb_ref, y_ref, sout_ref, shift_ref) = refs

    rows = _iota2((T, 1), 0)
    tpos = rows & (Lb - 1)
    if prompt:
        u = u_ref[...].astype(F32).reshape(T, SEG)
        hist = jnp.concatenate([jnp.broadcast_to(carry_sc[g, 7:8], (C, SEG)) for g in range(G)], axis=0)
        for g in range(G):
            carry_sc[g] = u[g * C + C - 8:(g + 1) * C]
    else:
        u = u_ref[...].astype(F32)
        hist = hist_ref[...]
        shift_ref[...] = _maskdot(sel_ref[...], u)
    um = u + (_shifted(u, 1, tpos, hist) - u) * mu_ref[...]

    W = RWKV_WIDTH
    r = um[:, 0:W]
    k = um[:, W:2 * W]
    v = um[:, 2 * W:3 * W]
    t12 = um[:, 3 * W:3 * W + LANES]
    lg = um[:, 3 * W + LANES:3 * W + 2 * LANES]

    wl = w0_ref[...] + _bdot(jnp.tanh(t12), w2_ref[...])
    logw = -math.exp(-0.5) * _sigmoid(wl)
    a = _sigmoid(a0_ref[...] + _bdot(t12, a2_ref[...]))
    out_gate = _bdot(_sigmoid(lg), g2_ref[...])

    hi = _iota2((W, W), 0)
    hj = _iota2((W, W), 1)
    headsum = ((hi >> 6) == (hj >> 6)).astype(BF16)

    kk = k * kk_ref[...]
    kk = kk * lax.rsqrt(jnp.maximum(_bdot(kk * kk, headsum), 1e-24))
    kmod = k * (1.0 + (a - 1.0) * ka_ref[...])
    beta = kk * a

    ri = _iota2((T, T), 0)
    ci = _iota2((T, T), 1)
    same_c = (ri >> lb) == (ci >> lb)
    lc = _maskdot((same_c & (ci <= ri)).astype(BF16), logw)
    e_pos = jnp.exp(lc)
    e_neg = jnp.exp(-lc)
    at = -kk * jnp.exp(lc - logw)
    rt = r * e_pos
    bt = beta * e_neg
    kt = kmod * e_neg

    si = _iota2((R, R), 0)
    sj = _iota2((R, R), 1)
    same_s = (si >> lb) == (sj >> lb)
    strict = same_s & (sj < si)
    incl = same_s & (sj <= si)
    incl2 = jnp.concatenate([incl, incl], axis=1)
    lane_row = _iota2((1, LANES), 1)
    m_lo = jnp.where(lane_row < 64, 1.0, 0.0).astype(BF16)
    m_hi = jnp.where(lane_row < 64, 0.0, 1.0).astype(BF16)
    if nb > 1:
        seq_s = (_iota2((R, LANES), 0) & (C - 1)) >> lb
        seq_masks = [jnp.where(seq_s == b, 1.0, 0.0).astype(BF16) for b in range(nb)]

    def stack(xp):
        return jnp.concatenate([xp * m_lo, xp * m_hi], axis=0)

    def expand(xs_):
        if nb == 1:
            return xs_
        return jnp.concatenate([xs_ * seq_masks[b] for b in range(nb)], axis=1)

    probs = [(g, p) for g in range(G) for p in range(4)]
    NP = range(len(probs))

    def tile(x, i):
        g, p = probs[i]
        return x[g * C:(g + 1) * C, p * LANES:(p + 1) * LANES]

    def stacked(pair, i):
        return (stack(tile(pair[0], i)), stack(tile(pair[1], i)))

    at2, v2, bt2, kt2 = (_split2(x) for x in (at, v, bt, kt))
    rt_b = rt.astype(BF16)
    a_s = [stacked(at2, i) for i in NP]
    r_s = [stack(tile(rt_b, i)) for i in NP]
    v_s = [stacked(v2, i) for i in NP]
    bk_s = [_cat2([stacked(bt2, i), stacked(kt2, i)], 0) for i in NP]
    gm_a = [_dot3s(a_s[i], bk_s[i], _NT) for i in NP]
    gm_r = [lax.dot_general(r_s[i], bk_s[i][0], _NT, preferred_element_type=F32) for i in NP]
    a_ab = [jnp.where(strict, gm_a[i][:, 0:R], 0.0) for i in NP]
    a_ak = [_split2(jnp.where(strict, gm_a[i][:, R:2 * R], 0.0)) for i in NP]
    a_r = [jnp.where(incl2, gm_r[i], 0.0).astype(BF16) for i in NP]
    akv = [_dot3s(a_ak[i], v_s[i]) for i in NP]
    tinv = _tri_inverse(a_ab, lb)

    if prompt:
        sst = [s_sc[i] for i in NP]
    else:
        z64 = jnp.zeros((64, 64), F32)

        def pair_blockdiag(b, p):
            top = jnp.concatenate([sin_ref[b, 2 * p], z64], axis=1)
            bot = jnp.concatenate([z64, sin_ref[b, 2 * p + 1]], axis=1)
            return jnp.concatenate([top, bot], axis=0)

        sst = [jnp.concatenate([pair_blockdiag(g * nb + b, p) for b in range(nb)], axis=1) for g, p in probs]
    ss = [_split2(s) for s in sst]
    ar0_a = [_dot3s((expand(a_s[i][0]), expand(a_s[i][1])), ss[i], _NT) for i in NP]
    ar0_r = [lax.dot_general(expand(r_s[i]), ss[i][0], _NT, preferred_element_type=F32) for i in NP]
    us = [_split2(x) for x in _refined_solve(tinv, a_ab, [ar0_a[i] + akv[i] for i in NP])]
    uv = [_cat2([us[i], v_s[i]], 0) for i in NP]
    yst = [ar0_r[i] + jnp.dot(a_r[i], uv[i][0], preferred_element_type=F32) for i in NP]
    ys = [yst[i][0:C] + yst[i][C:R] for i in NP]
    for i in NP:
        g, p = probs[i]
        pl_lanes = tile(e_pos, i)
        plast = jnp.concatenate([pl_lanes[b * Lb + Lb - 1:b * Lb + Lb] for b in range(nb)], axis=1)
        bk_x = tuple(jnp.concatenate([expand(x[0:R]), expand(x[R:2 * R])], axis=0) for x in bk_s[i])
        snew = (sst[i] + _dot3s(uv[i], bk_x, _TN)) * plast
        if prompt:
            s_sc[i] = snew
            sout_ref[g, p] = snew
        else:
            for b in range(nb):
                sout_ref[g * nb + b, 2 * p] = snew[0:64, b * LANES:b * LANES + 64]
                sout_ref[g * nb + b, 2 * p + 1] = snew[64:128, b * LANES + 64:(b + 1) * LANES]

    y = jnp.concatenate([jnp.concatenate(ys[4 * g:4 * g + 4], axis=1) for g in range(G)], axis=0)
    inv_d = 1.0 / 64.0
    mean = _bdot(y, headsum) * inv_d
    yc = y - mean
    var = _bdot(yc * yc, headsum) * inv_d
    yn = yc * lax.rsqrt(var + GN_EPS) * lnw_ref[...] + lnb_ref[...]
    yn = yn + _bdot(r * kmod * rk_ref[...], headsum) * v
    y_ref[...] = (yn * out_gate).astype(y_ref.dtype).reshape(y_ref.shape)


def _wkv_params(p):
    z64 = jnp.zeros((64, RWKV_WIDTH), F32)
    w2 = jnp.concatenate([p["rwkv_w2"], z64], axis=0).astype(BF16)
    a2 = jnp.concatenate([z64, p["rwkv_a2"]], axis=0).astype(BF16)
    row = lambda a: a.reshape(1, -1)
    return [row(p["rwkv_mu"]), row(p["rwkv_w0"]), w2, row(p["rwkv_a0"]), a2, p["rwkv_g2"].astype(BF16),
            row(p["rwkv_k_k"]), row(p["rwkv_k_a"]), row(p["rwkv_r_k"]), row(p["rwkv_ln_w"]), row(p["rwkv_ln_b"])]


def _wkv_prompt(u, p, batch, seq, C=64, G=8):
    nc = seq // C
    params = _wkv_params(p)
    y, sout = pl.pallas_call(
        functools.partial(_wkv_kernel, C=C, Lb=C, prompt=True, G=G),
        grid=(batch // G, nc),
        in_specs=[pl.BlockSpec((G, C, SEG), lambda b, c: (b, c, 1))] + [_const_spec(a) for a in params],
        out_specs=[pl.BlockSpec((G, C, RWKV_WIDTH), lambda b, c: (b, c, 0)),
                   pl.BlockSpec((G, 4, LANES, LANES), lambda b, c: (b, 0, 0, 0))],
        out_shape=[jax.ShapeDtypeStruct((batch, seq, RWKV_WIDTH), BF16),
                   jax.ShapeDtypeStruct((batch, 4, LANES, LANES), F32)],
        scratch_shapes=[pltpu.VMEM((G, 8, SEG), F32), pltpu.VMEM((4 * G, LANES, LANES), F32)],
        compiler_params=_cparams("parallel", "arbitrary"),
        name="wkv_prompt",
    )(u.reshape(batch, seq, 2 * SEG), *params)
    return y.reshape(batch * seq, RWKV_WIDTH), sout


def _wkv_sample(u, hist, sin, p, Lb, C=64, G=2):
    n = u.shape[0]
    nb = C // Lb
    T = G * C
    params = _wkv_params(p)
    sel = _row_selector(G * nb, Lb, (Lb - 1,))
    return pl.pallas_call(
        functools.partial(_wkv_kernel, C=C, Lb=Lb, prompt=False, G=G),
        grid=(n // T,),
        in_specs=[pl.BlockSpec((T, SEG), lambda i: (i, 1)),
                  pl.BlockSpec((T, SEG), lambda i: (i, 0)),
                  pl.BlockSpec((G * nb, 8, 64, 64), lambda i: (i, 0, 0, 0)),
                  _const_spec(sel)]
                 + [_const_spec(a) for a in params],
        out_specs=[pl.BlockSpec((T, RWKV_WIDTH), lambda i: (i, 0)),
                   pl.BlockSpec((G * nb, 8, 64, 64), lambda i: (i, 0, 0, 0)),
                   pl.BlockSpec((G * nb, SEG), lambda i: (i, 0))],
        out_shape=[jax.ShapeDtypeStruct((n, RWKV_WIDTH), BF16),
                   jax.ShapeDtypeStruct(sin.shape, F32),
                   jax.ShapeDtypeStruct((n // Lb, SEG), F32)],
        compiler_params=_cparams("parallel"),
        name="wkv_sample",
    )(u, hist, sin, sel, *params)


def _outproj_kernel(x_ref, ys_ref, yr_ref, wo_ref, g_ref, wq_ref, x1_ref, q_ref):
    wo = wo_ref[...]
    x1 = x_ref[...] + (_bdot(ys_ref[...], wo[0:SSD_WIDTH]) + _bdot(yr_ref[...], wo[SSD_WIDTH:]))
    x1_ref[...] = x1
    q_ref[...] = _bdot(_rms(x1, g_ref[...]), wq_ref[...])


def _outproj(x, y_ssd, y_rw, w_out, g, w_q, tm):
    n, d = x.shape
    return pl.pallas_call(
        _outproj_kernel,
        grid=(n // tm,),
        in_specs=[pl.BlockSpec((tm, d), lambda i: (i, 0)),
                  pl.BlockSpec((tm, SSD_WIDTH), lambda i: (i, 0)),
                  pl.BlockSpec((tm, RWKV_WIDTH), lambda i: (i, 0)),
                  _const_spec(w_out), pl.BlockSpec((1, d), lambda i: (0, 0)), _const_spec(w_q)],
        out_specs=[pl.BlockSpec((tm, d), lambda i: (i, 0)), pl.BlockSpec((tm, d), lambda i: (i, 0))],
        out_shape=[jax.ShapeDtypeStruct((n, d), F32), jax.ShapeDtypeStruct((n, d), F32)],
        compiler_params=_cparams("parallel"),
        name="outproj_q",
    )(x, y_ssd, y_rw, w_out, g.reshape(1, d), w_q)


def _xattn_cache_kernel(q_ref, k_ref, v_ref, o_ref, *, seq_len):
    tq = q_ref.shape[0]
    nmem = tq // seq_len
    rows = N_MEM * XA_HEADS
    cols = XA_HEADS * tq
    q = q_ref[...] * (XA_HEAD_DIM ** -0.5)
    qblk = jnp.concatenate([q[:, h * XA_HEAD_DIM:(h + 1) * XA_HEAD_DIM] for h in range(XA_HEADS)], axis=0)
    key_head = _iota2((cols, rows), 1) & (XA_HEADS - 1)
    qrow = _iota2((cols, 1), 0)
    head_ok = key_head == (qrow >> int(math.log2(tq)))
    out_mem = _iota2((tq, 1), 0) >> int(math.log2(seq_len))
    o = None
    for m in range(nmem):
        kall = k_ref[m].reshape(rows, XA_HEAD_DIM)
        vall = v_ref[m].reshape(rows, XA_HEAD_DIM)
        s = jnp.where(head_ok, _bdot(qblk, kall, _NT), -jnp.inf)
        e = jnp.exp(s - jnp.max(s, axis=-1, keepdims=True))
        pr = e / jnp.sum(e, axis=-1, keepdims=True)
        om = _bdot(pr, vall)
        om = jnp.concatenate([om[h * tq:(h + 1) * tq] for h in range(XA_HEADS)], axis=1)
        o = om if o is None else jnp.where(out_mem == m, om, o)
    o_ref[...] = o.astype(o_ref.dtype)


def _xattn_cache(q, ck, cv, tq, seq_len):
    n, d = q.shape
    nmem = tq // seq_len
    row_spec = pl.BlockSpec((tq, d), lambda i: (i, 0))
    kv_spec = pl.BlockSpec((nmem, N_MEM, XA_HEADS, XA_HEAD_DIM), lambda i: (i, 0, 0, 0))
    return pl.pallas_call(
        functools.partial(_xattn_cache_kernel, seq_len=seq_len),
        grid=(n // tq,),
        in_specs=[row_spec, kv_spec, kv_spec],
        out_specs=row_spec,
        out_shape=jax.ShapeDtypeStruct((n, d), BF16),
        compiler_params=_cparams("parallel"),
        name="xattn_cache",
    )(q, ck, cv)


FF_SUB = 256
FF_GROUP = 4


def _ffn_kernel(*refs, tm, Lb, prompt, blocks_per_seq):
    assert not prompt
    x1_ref, o_ref, wo_ref, hist_ref, g_ref, wu_ref, cw_ref, cb_ref, wd_ref, fg_ref, y_ref, s_ref = refs
    x2 = x1_ref[...] + jnp.dot(o_ref[...], wo_ref[...], preferred_element_type=F32)
    _ffn_body(x2, hist_ref, g_ref, wu_ref, cw_ref, cb_ref, wd_ref, fg_ref, y_ref, s_ref, None,
              tm=tm, Lb=Lb, prompt=False, blocks_per_seq=blocks_per_seq)


def _ffn_body(x, hist_ref, g_ref, wu_ref, cw_ref, cb_ref, wd_ref, fg_ref, y_ref, s_ref, carry_sc,
              *, tm, Lb, prompt, blocks_per_seq):
    if prompt:
        @pl.when(pl.program_id(0) % blocks_per_seq == 0)
        def _():
            carry_sc[...] = jnp.zeros_like(carry_sc)

    hn = _rms(x, g_ref[...]).astype(BF16)
    nsub = D_FF // FF_SUB
    r8 = _iota2((8, 1), 0)
    tpos = _iota2((tm, 1), 0) & (Lb - 1)
    if not prompt:
        trow = _iota2((tm, tm // Lb), 0)
        seq0 = _iota2((tm, tm // Lb), 1) * Lb
        at_row0 = (trow == seq0).astype(BF16)
        at_row1 = (trow == seq0 + 1).astype(BF16)

    def conv(up, cols):
        cw = cw_ref[:, cols]
        cb = cb_ref[:, cols]
        if prompt:
            c8 = carry_sc[:, cols]
            carry_sc[:, cols] = up[tm - 8:tm]
            s_ref[0, :, cols] = up[tm - 2:tm]
            top = up[0:8]
            p1 = jnp.where(r8 >= 1, pltpu.roll(top, 1, 0), c8[7:8])
            p2 = jnp.where(r8 >= 2, pltpu.roll(top, 2, 0), jnp.where(r8 == 0, c8[6:7], c8[7:8]))
            head = (cw[0:1] * p2 + cw[1:2] * p1 + cw[2:3] * top) + cb
            body = (cw[0:1] * pltpu.roll(up, 2, 0) + cw[1:2] * pltpu.roll(up, 1, 0) + cw[2:3] * up) + cb
            return jnp.concatenate([head, body[8:]], axis=0)
        for b in range(tm // Lb):
            s_ref[b, :, cols] = up[b * Lb + Lb - 2:b * Lb + Lb]
        st0, st1 = hist_ref[0, :, cols], hist_ref[1, :, cols]
        h1 = _maskdot(at_row0, st1)
        h2 = _maskdot(at_row0, st0) + _maskdot(at_row1, st1)
        return (cw[0:1] * _shifted(up, 2, tpos, h2) + cw[1:2] * _shifted(up, 1, tpos, h1) + cw[2:3] * up) + cb

    def up_proj(k):
        cg = slice(k * FF_SUB, (k + 1) * FF_SUB)
        cv = slice(D_FF + k * FF_SUB, D_FF + (k + 1) * FF_SUB)
        return (jnp.dot(hn, wu_ref[:, cg], preferred_element_type=F32),
                jnp.dot(hn, wu_ref[:, cv], preferred_element_type=F32))

    def gate_mul(k, ug, uv):
        cg = slice(k * FF_SUB, (k + 1) * FF_SUB)
        cv = slice(D_FF + k * FF_SUB, D_FF + (k + 1) * FF_SUB)
        gate = conv(ug, cg)
        return (gate * _sigmoid(gate) * conv(uv, cv)).astype(BF16)

    acc = x
    ups = {}
    acts = []
    for k in range(nsub + 1):
        if k < nsub:
            ups[k] = up_proj(k)
        if k >= 1:
            acts.append(gate_mul(k - 1, *ups.pop(k - 1)))
            if len(acts) == FF_GROUP or k == nsub:
                lo = (k - len(acts)) * FF_SUB
                acc = acc + jnp.dot(jnp.concatenate(acts, axis=1), wd_ref[lo:k * FF_SUB, :],
                                    preferred_element_type=F32)
                acts = []
    y_ref[...] = _rms(acc, fg_ref[...])


def _resident_spec(a):
    nd = a.ndim
    return pl.BlockSpec(a.shape, lambda *_: (0,) * nd, pipeline_mode=pl.Buffered(1))


def _ffn_short(x1, o, w_o, hist, p, final_g, tm, Lb):
    n, d = x1.shape
    w_up = p["ffn_w_up"].astype(BF16)
    w_down = p["ffn_w_down"].astype(BF16)
    cw, cb = p["ffn_conv_w"], p["ffn_conv_b"].reshape(1, 2 * D_FF)
    row_spec = pl.BlockSpec((tm, d), lambda i: (i, 0))
    seq_spec = lambda lead: pl.BlockSpec((lead, tm // Lb, 2 * D_FF), lambda i: (0, i, 0))
    consts = [p["norm_ffn_w"].reshape(1, d), w_up, cw, cb, w_down, final_g.reshape(1, d)]
    return pl.pallas_call(
        functools.partial(_ffn_kernel, tm=tm, Lb=Lb, prompt=False, blocks_per_seq=1),
        grid=(n // tm,),
        in_specs=[row_spec, row_spec, _resident_spec(w_o), seq_spec(2)] + [_resident_spec(a) for a in consts],
        out_specs=[row_spec, pl.BlockSpec((tm // Lb, 2, 2 * D_FF), lambda i: (i, 0, 0))],
        out_shape=[jax.ShapeDtypeStruct((n, d), F32), jax.ShapeDtypeStruct((n // Lb, 2, 2 * D_FF), F32)],
        compiler_params=_cparams("arbitrary"),
        name="convffn",
    )(x1, o, w_o, hist, *consts)


def _post_mix_kernel(x_ref, ys_ref, yr_ref, k_ref, v_ref, wo_ref, gxa_ref, wq_ref, wao_ref,
                     g_ref, wu_ref, cw_ref, cb_ref, wd_ref, fg_ref, y_ref, s_ref, carry_sc,
                     *, tm, seq_len):
    wo = wo_ref[...]
    x1 = x_ref[...] + (jnp.dot(ys_ref[...], wo[0:SSD_WIDTH], preferred_element_type=F32)
                       + jnp.dot(yr_ref[...], wo[SSD_WIDTH:], preferred_element_type=F32))
    q = (_bdot(_rms(x1, gxa_ref[...]), wq_ref[...]) * (XA_HEAD_DIM ** -0.5)).astype(BF16)
    outs = []
    for h in range(XA_HEADS):
        sl = slice(h * XA_HEAD_DIM, (h + 1) * XA_HEAD_DIM)
        s = lax.dot_general(q[:, sl], k_ref[0, :, sl].astype(BF16), _NT, preferred_element_type=F32)
        s = s - jnp.max(s, axis=-1, keepdims=True)
        e = jnp.exp(s)
        pr = e / jnp.sum(e, axis=-1, keepdims=True)
        outs.append(_bdot(pr, v_ref[0, :, sl]))
    x2 = x1 + _bdot(jnp.concatenate(outs, axis=1), wao_ref[...])
    _ffn_body(x2, None, g_ref, wu_ref, cw_ref, cb_ref, wd_ref, fg_ref, y_ref, s_ref, carry_sc,
              tm=tm, Lb=seq_len, prompt=True, blocks_per_seq=seq_len // tm)


def _post_mix(x, y_ssd, y_rw, mk, mv, w_out, g_xa, w_q, w_o, p, final_g, tm, seq_len):
    n, d = x.shape
    bps = seq_len // tm
    row = lambda w: pl.BlockSpec((tm, w), lambda i: (i, 0))
    kv_spec = pl.BlockSpec((1, N_MEM, d), lambda i: (i // bps, 0, 0))
    consts = [w_out, g_xa.reshape(1, d), w_q, w_o, p["norm_ffn_w"].reshape(1, d), p["ffn_w_up"].astype(BF16),
              p["ffn_conv_w"], p["ffn_conv_b"].reshape(1, 2 * D_FF), p["ffn_w_down"].astype(BF16),
              final_g.reshape(1, d)]
    y, st = pl.pallas_call(
        functools.partial(_post_mix_kernel, tm=tm, seq_len=seq_len),
        grid=(n // tm,),
        in_specs=[row(d), row(SSD_WIDTH), row(RWKV_WIDTH), kv_spec, kv_spec] + [_resident_spec(a) for a in consts],
        out_specs=[row(d), pl.BlockSpec((1, 2, 2 * D_FF), lambda i: (i, 0, 0))],
        out_shape=[jax.ShapeDtypeStruct((n, d), F32), jax.ShapeDtypeStruct((n // tm, 2, 2 * D_FF), F32)],
        scratch_shapes=[pltpu.VMEM((8, 2 * D_FF), F32)],
        compiler_params=_cparams("arbitrary"),
        name="post_mix",
    )(x, y_ssd, y_rw, mk, mv, *consts)
    return y, st[bps - 1::bps]


def _pair_blocks(sbd):
    b = sbd.shape[0]
    s0 = sbd[:, :, 0:64, 0:64]
    s1 = sbd[:, :, 64:128, 64:128]
    return jnp.stack([s0, s1], axis=2).reshape(b, 8, 64, 64)


def _hist_rows(state, seq_len, k):
    b, w, c = state.shape
    cols = [state[:, w + t - k] if t < k else jnp.zeros((b, c), state.dtype) for t in range(seq_len)]
    return jnp.stack(cols, axis=1).reshape(b * seq_len, c)


def kernel(x_prompt, x_sample, mem_prompt, state_ssm_conv, state_ssm, state_shift, state_wkv, state_ffn_conv, cache_mem_k, cache_mem_v, norm_mix_w, w_in, ssd_conv_w, ssd_conv_b, ssd_dt_bias, ssd_a_log, ssd_d, ssd_norm_w, rwkv_mu, rwkv_w0, rwkv_w2, rwkv_a0, rwkv_a2, rwkv_g2, rwkv_k_k, rwkv_k_a, rwkv_r_k, rwkv_ln_w, rwkv_ln_b, w_out, norm_xa_w, mem_norm_w, xa_w_q, xa_w_k, xa_w_v, xa_w_o, norm_ffn_w, ffn_w_up, ffn_conv_w, ffn_conv_b, ffn_w_down, final_norm_w):
    depth = w_in.shape[0]
    assert depth == 1, "final rmsnorm is fused into the (single) layer's ConvFFN kernel"
    bp, lp, d = x_prompt.shape
    bs, ls, _ = x_sample.shape
    i = 0
    p = dict(ssd_conv_w=ssd_conv_w[i], ssd_conv_b=ssd_conv_b[i], ssd_dt_bias=ssd_dt_bias[i],
             ssd_a_log=ssd_a_log[i], ssd_d=ssd_d[i], ssd_norm_w=ssd_norm_w[i], rwkv_mu=rwkv_mu[i],
             rwkv_w0=rwkv_w0[i], rwkv_w2=rwkv_w2[i], rwkv_a0=rwkv_a0[i], rwkv_a2=rwkv_a2[i],
             rwkv_g2=rwkv_g2[i], rwkv_k_k=rwkv_k_k[i], rwkv_k_a=rwkv_k_a[i],
             rwkv_r_k=rwkv_r_k[i].reshape(-1), rwkv_ln_w=rwkv_ln_w[i], rwkv_ln_b=rwkv_ln_b[i],
             norm_ffn_w=norm_ffn_w[i], ffn_w_up=ffn_w_up[i], ffn_conv_w=ffn_conv_w[i],
             ffn_conv_b=ffn_conv_b[i], ffn_w_down=ffn_w_down[i])

    w_in_t = jnp.swapaxes(w_in[i], 0, 1)
    w_in_p = jnp.concatenate([w_in_t[:SSD_PROJ], jnp.zeros((SEG - SSD_PROJ, d), F32),
                              w_in_t[SSD_PROJ:]], axis=0).astype(BF16)
    w_out_b = w_out[i].astype(BF16)
    w_q_b = (xa_w_q[i]).astype(BF16)
    w_o_b = xa_w_o[i].astype(BF16)
    w_kv_b = jnp.concatenate([xa_w_k[i], xa_w_v[i]], axis=1).astype(BF16)

    xp = x_prompt.reshape(bp * lp, d)
    xs = x_sample.reshape(bs * ls, d)

    mk, mv, mk4, mv4 = _mem_kv(mem_prompt.reshape(bp * N_MEM, d), mem_norm_w[i], w_kv_b, 512)
    mk = mk.reshape(bp, N_MEM, d)
    mv = mv.reshape(bp, N_MEM, d)
    tm_p = min(1024, lp)
    u_p, dt_p, tail_p = _norm_proj(xp, norm_mix_w[i], w_in_p, tm_p, BF16)
    y_ssd_p, ssm_p = _ssd_prompt(u_p, dt_p, p, bp, lp, G=math.gcd(4, bp))
    y_rw_p, wkv_bd_p = _wkv_prompt(u_p, p, bp, lp, G=math.gcd(8, bp))
    y_p, ffn_conv_p = _post_mix(xp, y_ssd_p, y_rw_p, mk, mv, w_out_b, norm_xa_w[i], w_q_b, w_o_b, p,
                                final_norm_w, 512, lp)
    seq_tail = tail_p[lp // tm_p - 1::lp // tm_p]
    ssm_conv_p = seq_tail[:, 5:, SSD_WIDTH:SSD_WIDTH + SSD_XBC]
    shift_p = seq_tail[:, 7, SEG:]

    u_s, dt_s, _ = _norm_proj(xs, norm_mix_w[i], w_in_p, 512, F32)
    conv_state = state_ssm_conv[i]
    hist_conv = jnp.stack([_hist_rows(conv_state, ls, k) for k in (1, 2, 3)], axis=0)
    assert ls >= 3, "the new conv state is read from the last three rows of each sample sequence"
    y_ssd_s, ssm_s, conv_rows_s = _ssd_sample(u_s, dt_s, hist_conv, state_ssm[i].reshape(bs, 4, LANES, LANES), p, ls)
    ssm_conv_s = conv_rows_s.reshape(bs, 3, SSD_XBC)
    hist_shift = _hist_rows(state_shift[i][:, None, :], ls, 1)
    y_rw_s, wkv_s, shift_s = _wkv_sample(u_s, hist_shift, state_wkv[i], p, ls)
    x1_s, q_s = _outproj(xs, y_ssd_s, y_rw_s, w_out_b, norm_xa_w[i], w_q_b, 512)
    o_s = _xattn_cache(q_s, cache_mem_k[i], cache_mem_v[i], 32, ls)
    y_s, ffn_conv_s = _ffn_short(x1_s, o_s, w_o_b, jnp.swapaxes(state_ffn_conv[i], 0, 1), p, final_norm_w, 256, ls)

    e = lambda a: a[None]
    return (y_p.reshape(bp, lp, d), y_s.reshape(bs, ls, d),
            e(ssm_conv_p), e(ssm_conv_s),
            e(ssm_p.reshape(bp, SSD_HEADS, 64, SSD_STATE)), e(ssm_s.reshape(bs, SSD_HEADS, 64, SSD_STATE)),
            e(shift_p), e(shift_s),
            e(_pair_blocks(wkv_bd_p)), e(wkv_s),
            e(ffn_conv_p), e(ffn_conv_s),
            e(mk4.reshape(bp, N_MEM, XA_HEADS, XA_HEAD_DIM)), e(mv4.reshape(bp, N_MEM, XA_HEADS, XA_HEAD_DIM)))
```

```python
import functools
import math

import jax
import jax.numpy as jnp
from jax import lax
from jax.experimental import pallas as pl
from jax.experimental.pallas import tpu as pltpu

F32 = jnp.float32
BF16 = jnp.bfloat16

D_MODEL = 1024
N_MEM = 256
XA_HEADS = 4
XA_HEAD_DIM = D_MODEL // XA_HEADS
SSD_WIDTH = 512
SSD_HEADS = 8
SSD_STATE = 128
SSD_XBC = 1024
SSD_PROJ = SSD_WIDTH + SSD_XBC + SSD_HEADS
RWKV_WIDTH = 512
RWKV_PROJ = 1792
D_FF = 2816
EPS = 1e-6
GN_EPS = 64e-5

LANES = 128
SEG = 1792
DT_OFF = SSD_WIDTH + SSD_XBC
VMEM_LIMIT_BYTES = 56 * 1024 * 1024


def _cparams(*sem):
    return pltpu.CompilerParams(dimension_semantics=sem, vmem_limit_bytes=VMEM_LIMIT_BYTES)


_NN = (((1,), (0,)), ((), ()))
_NT = (((1,), (1,)), ((), ()))
_TN = (((0,), (0,)), ((), ()))


def _bdot(a, b, dims=_NN):
    return lax.dot_general(a.astype(BF16), b.astype(BF16), dims, preferred_element_type=F32)


def _split2(x):
    hi = x.astype(BF16)
    lo = (x - hi.astype(F32)).astype(BF16)
    return hi, lo


def _maskdot(m_bf16, x):
    hi = x.astype(BF16)
    r1 = x - hi.astype(F32)
    mid = r1.astype(BF16)
    lo = (r1 - mid.astype(F32)).astype(BF16)
    return jnp.dot(jnp.concatenate([m_bf16, m_bf16, m_bf16], axis=1),
                   jnp.concatenate([hi, mid, lo], axis=0), preferred_element_type=F32)


def _sigmoid(x):
    return 0.5 + 0.5 * jnp.tanh(0.5 * x)


def _softplus(x):
    return jnp.maximum(x, 0.0) + jnp.log1p(jnp.exp(-jnp.abs(x)))


def _rms(x, g):
    return x * lax.rsqrt(jnp.mean(x * x, axis=-1, keepdims=True) + EPS) * g


def _iota2(shape, dim):
    return lax.broadcasted_iota(jnp.int32, shape, dim)


def _shifted(x, k, tpos, hist):
    return jnp.where(tpos >= k, pltpu.roll(x, k, 0), hist)


def _norm_proj_kernel(x_ref, g_ref, wt_ref, o_ref, dt_ref, tail_ref, hn_sc):
    j = pl.program_id(1)

    @pl.when(j == 0)
    def _():
        hn_sc[...] = _rms(x_ref[...], g_ref[...]).astype(BF16)

    acc = lax.dot_general(hn_sc[...], wt_ref[...], _NT, preferred_element_type=F32)
    o_ref[...] = acc.astype(o_ref.dtype)
    tail_ref[0] = acc[acc.shape[0] - 8:]

    @pl.when(j == 0)
    def _():
        dt_ref[...] = acc[:, DT_OFF:DT_OFF + LANES]


def _norm_proj(x, g, wt_bf16, tm, out_dtype):
    n, d = x.shape
    return pl.pallas_call(
        _norm_proj_kernel,
        grid=(n // tm, 2),
        in_specs=[pl.BlockSpec((tm, d), lambda i, j: (i, 0)),
                  pl.BlockSpec((1, d), lambda i, j: (0, 0)),
                  pl.BlockSpec((SEG, d), lambda i, j: (j, 0))],
        out_specs=[pl.BlockSpec((tm, SEG), lambda i, j: (i, j)),
                   pl.BlockSpec((tm, LANES), lambda i, j: (i, 0)),
                   pl.BlockSpec((1, 8, SEG), lambda i, j: (i, 0, j))],
        out_shape=[jax.ShapeDtypeStruct((n, 2 * SEG), out_dtype),
                   jax.ShapeDtypeStruct((n, LANES), F32),
                   jax.ShapeDtypeStruct((n // tm, 8, 2 * SEG), F32)],
        scratch_shapes=[pltpu.VMEM((tm, d), BF16)],
        compiler_params=_cparams("parallel", "arbitrary"),
        name="norm_proj",
    )(x, g.reshape(1, d), wt_bf16)


def _mem_kv_kernel(x_ref, g_ref, w_ref, k_ref, v_ref, k4_ref, v4_ref):
    hn = _rms(x_ref[...], g_ref[...]).astype(BF16)
    d = k_ref.shape[1]
    k = jnp.dot(hn, w_ref[:, 0:d], preferred_element_type=F32)
    v = jnp.dot(hn, w_ref[:, d:2 * d], preferred_element_type=F32)
    k_ref[...] = k
    v_ref[...] = v
    for h in range(XA_HEADS):
        sl = slice(h * XA_HEAD_DIM, (h + 1) * XA_HEAD_DIM)
        k4_ref[:, h, :] = k[:, sl]
        v4_ref[:, h, :] = v[:, sl]


def _mem_kv(x, g, w_kv_bf16, tm):
    n, d = x.shape
    row_spec = pl.BlockSpec((tm, d), lambda i: (i, 0))
    head_spec = pl.BlockSpec((tm, XA_HEADS, XA_HEAD_DIM), lambda i: (i, 0, 0))
    return pl.pallas_call(
        _mem_kv_kernel,
        grid=(n // tm,),
        in_specs=[row_spec, pl.BlockSpec((1, d), lambda i: (0, 0)), _const_spec(w_kv_bf16)],
        out_specs=[row_spec, row_spec, head_spec, head_spec],
        out_shape=[jax.ShapeDtypeStruct((n, d), F32)] * 2
                  + [jax.ShapeDtypeStruct((n, XA_HEADS, XA_HEAD_DIM), F32)] * 2,
        compiler_params=_cparams("parallel"),
        name="mem_kv",
    )(x, g.reshape(1, d), w_kv_bf16)


def _ssd_kernel(*refs, Q, Lb, prompt, G=1):
    if not prompt:
        return _ssd_block(*refs, Q=Q, Lb=Lb, prompt=False)
    (u_ref, dt_ref, cw_ref, cb_ref, dtb_ref, an_ref, dsk_ref, nw_ref, y_ref, hout_ref, carry_sc, h_sc) = refs

    @pl.when(pl.program_id(1) == 0)
    def _():
        carry_sc[...] = jnp.zeros_like(carry_sc)
        h_sc[...] = jnp.zeros_like(h_sc)

    for gi in range(G):
        _ssd_block(u_ref.at[gi], dt_ref.at[gi], cw_ref, cb_ref, dtb_ref, an_ref, dsk_ref, nw_ref,
                   y_ref.at[gi], hout_ref.at[pl.ds(gi, 1)], carry_sc.at[gi], h_sc.at[pl.ds(4 * gi, 4)],
                   Q=Q, Lb=Lb, prompt=True)


def _ssd_block(*refs, Q, Lb, prompt):
    ns = Q // Lb
    lb = int(math.log2(Lb))
    rows = _iota2((Q, 1), 0)
    tpos = rows & (Lb - 1)
    if prompt:
        (u_ref, dt_ref, cw_ref, cb_ref, dtb_ref, an_ref, dsk_ref, nw_ref,
         y_ref, hout_ref, carry_sc, h_sc) = refs
    else:
        (u_ref, dt_ref, hist_ref, hin_ref, sel_ref, cw_ref, cb_ref, dtb_ref, an_ref, dsk_ref, nw_ref,
         y_ref, hout_ref, cs_ref) = refs
    u = u_ref[...].astype(F32)
    z = u[:, 0:SSD_WIDTH]
    z_gate = z * _sigmoid(z)
    x = u[:, SSD_WIDTH:SSD_WIDTH + SSD_XBC]
    if prompt:
        c8 = carry_sc[...]
        h1 = c8[7:8]
        h2 = jnp.where(rows == 0, c8[6:7], c8[7:8])
        h3 = jnp.where(rows == 0, c8[5:6], jnp.where(rows == 1, c8[6:7], c8[7:8]))
        carry_sc[...] = x[Q - 8:Q]
    else:
        cs_ref[...] = _maskdot(sel_ref[...], x)
        h1, h2, h3 = hist_ref[0], hist_ref[1], hist_ref[2]
    cw = cw_ref[...]
    xc = (cw[0:1] * _shifted(x, 3, tpos, h3) + cw[1:2] * _shifted(x, 2, tpos, h2)
          + cw[2:3] * _shifted(x, 1, tpos, h1) + cw[3:4] * x) + cb_ref[...]
    xc = xc * _sigmoid(xc)
    dtr = dt_ref[...]
    xs = xc[:, 0:SSD_WIDTH]
    bm = xc[:, SSD_WIDTH:SSD_WIDTH + 2 * SSD_STATE]
    cm = xc[:, SSD_WIDTH + 2 * SSD_STATE:]

    dt = _softplus(dtr + dtb_ref[...])
    da = dt * an_ref[...]

    ri = _iota2((Q, Q), 0)
    ci = _iota2((Q, Q), 1)
    same = (ri >> lb) == (ci >> lb)
    tril = same & (ci <= ri)
    sel = ci == (((ri >> lb) << lb) + (Lb - 1))
    acs = _maskdot(tril.astype(BF16), da)
    acs_t = acs.T
    acs_last = _maskdot(sel.astype(BF16), acs)
    dec_end = jnp.exp(acs_last - acs)
    eacs = jnp.exp(acs)
    seqid = rows >> lb

    lane = _iota2((Q, LANES), 1)
    lo_half = lane < 64
    prow = _iota2((LANES, LANES), 0)
    dsk = dsk_ref[...]

    ys = []
    for q in range(4):
        g = q // 2
        h0, h1i = 2 * q, 2 * q + 1
        if q % 2 == 0:
            cg = cm[:, g * SSD_STATE:(g + 1) * SSD_STATE]
            bg = bm[:, g * SSD_STATE:(g + 1) * SSD_STATE]
            cb_g = _bdot(cg, bg, _NT)
            if ns == 1:
                cexp, bexp = cg.astype(BF16), bg.astype(BF16)
            else:
                cexp = jnp.concatenate([jnp.where(seqid == b, cg, 0.0).astype(BF16) for b in range(ns)], axis=1)
                bexp = jnp.concatenate([jnp.where(seqid == b, bg, 0.0).astype(BF16) for b in range(ns)], axis=1)
        m0 = jnp.where(tril, cb_g * jnp.exp(acs[:, h0:h0 + 1] - acs_t[h0:h0 + 1, :]), 0.0)
        m1 = jnp.where(tril, cb_g * jnp.exp(acs[:, h1i:h1i + 1] - acs_t[h1i:h1i + 1, :]), 0.0)
        xp = xs[:, q * LANES:(q + 1) * LANES]
        xdt = xp * jnp.where(lo_half, dt[:, h0:h0 + 1], dt[:, h1i:h1i + 1])
        xdt0 = jnp.where(lo_half, xdt, 0.0)
        xdt1 = jnp.where(lo_half, 0.0, xdt)
        ydiag = _bdot(jnp.concatenate([m0, m1], axis=1), jnp.concatenate([xdt0, xdt1], axis=0))
        if prompt:
            hst = h_sc[q]
        else:
            hst = jnp.concatenate([hin_ref[b, q] for b in range(ns)], axis=1)
        ecs = jnp.where(lo_half, eacs[:, h0:h0 + 1], eacs[:, h1i:h1i + 1])
        yoff = _bdot(cexp, hst, _NT) * ecs
        xd = xdt * jnp.where(lo_half, dec_end[:, h0:h0 + 1], dec_end[:, h1i:h1i + 1])
        incr = _bdot(xd, bexp, _TN)
        scales = []
        for b in range(ns):
            r = b * Lb + Lb - 1
            e = eacs[r:r + 1, :]
            scales.append(jnp.where(prow < 64, e[:, h0:h0 + 1], e[:, h1i:h1i + 1]))
        scale = scales[0] if ns == 1 else jnp.concatenate(scales, axis=1)
        hnew = hst * scale + incr
        if prompt:
            h_sc[q] = hnew
            hout_ref[0, q] = hnew
        else:
            for b in range(ns):
                hout_ref[b, q] = hnew[:, b * LANES:(b + 1) * LANES]
        ys.append(ydiag + yoff + dsk[:, q * LANES:(q + 1) * LANES] * xp)

    y = jnp.concatenate(ys, axis=1)
    y = y * z_gate
    half = SSD_WIDTH // 2
    outs = []
    for g in range(2):
        yg = y[:, g * half:(g + 1) * half]
        outs.append(yg * lax.rsqrt(jnp.mean(yg * yg, axis=-1, keepdims=True) + EPS))
    y_ref[...] = (jnp.concatenate(outs, axis=1) * nw_ref[...]).astype(y_ref.dtype)


def _ssd_params(p):
    an = jnp.zeros((1, LANES), F32).at[0, :SSD_HEADS].set(-jnp.exp(p["ssd_a_log"]))
    dtb = jnp.zeros((1, LANES), F32).at[0, :SSD_HEADS].set(p["ssd_dt_bias"])
    dsk = jnp.repeat(p["ssd_d"], SSD_WIDTH // SSD_HEADS).reshape(1, SSD_WIDTH)
    return [p["ssd_conv_w"], p["ssd_conv_b"].reshape(1, SSD_XBC), dtb, an, dsk,
            p["ssd_norm_w"].reshape(1, SSD_WIDTH)]


def _const_spec(a):
    nd = a.ndim
    return pl.BlockSpec(a.shape, lambda *_: (0,) * nd)


def _ssd_prompt(u, dt, p, batch, seq, Q=128, G=4):
    nc = seq // Q
    params = _ssd_params(p)
    y, hout = pl.pallas_call(
        functools.partial(_ssd_kernel, Q=Q, Lb=Q, prompt=True, G=G),
        grid=(batch // G, nc),
        in_specs=[pl.BlockSpec((G, Q, SEG), lambda b, c: (b, c, 0)),
                  pl.BlockSpec((G, Q, LANES), lambda b, c: (b, c, 0))] + [_const_spec(a) for a in params],
        out_specs=[pl.BlockSpec((G, Q, SSD_WIDTH), lambda b, c: (b, c, 0)),
                   pl.BlockSpec((G, 4, LANES, LANES), lambda b, c: (b, 0, 0, 0))],
        out_shape=[jax.ShapeDtypeStruct((batch, seq, SSD_WIDTH), BF16),
                   jax.ShapeDtypeStruct((batch, 4, LANES, LANES), F32)],
        scratch_shapes=[pltpu.VMEM((G, 8, SSD_XBC), F32), pltpu.VMEM((4 * G, LANES, LANES), F32)],
        compiler_params=_cparams("parallel", "arbitrary"),
        name="ssd_prompt",
    )(u.reshape(batch, seq, 2 * SEG), dt.reshape(batch, seq, LANES), *params)
    return y.reshape(batch * seq, SSD_WIDTH), hout


def _row_selector(nseq, Lb, offsets):
    k = len(offsets)
    rows = jnp.arange(nseq * k)
    target = (rows // k) * Lb + jnp.asarray(offsets)[rows % k]
    return (target[:, None] == jnp.arange(nseq * Lb)[None, :]).astype(BF16)


def _ssd_sample(u, dt, hist, hin, p, Lb, Q=64):
    n = u.shape[0]
    ns = Q // Lb
    params = _ssd_params(p)
    sel = _row_selector(ns, Lb, (Lb - 3, Lb - 2, Lb - 1))
    return pl.pallas_call(
        functools.partial(_ssd_kernel, Q=Q, Lb=Lb, prompt=False),
        grid=(n // Q,),
        in_specs=[pl.BlockSpec((Q, SEG), lambda i: (i, 0)),
                  pl.BlockSpec((Q, LANES), lambda i: (i, 0)),
                  pl.BlockSpec((3, Q, SSD_XBC), lambda i: (0, i, 0)),
                  pl.BlockSpec((ns, 4, LANES, LANES), lambda i: (i, 0, 0, 0)),
                  _const_spec(sel)]
                 + [_const_spec(a) for a in params],
        out_specs=[pl.BlockSpec((Q, SSD_WIDTH), lambda i: (i, 0)),
                   pl.BlockSpec((ns, 4, LANES, LANES), lambda i: (i, 0, 0, 0)),
                   pl.BlockSpec((ns * 3, SSD_XBC), lambda i: (i, 0))],
        out_shape=[jax.ShapeDtypeStruct((n, SSD_WIDTH), BF16),
                   jax.ShapeDtypeStruct(hin.shape, F32),
                   jax.ShapeDtypeStruct((n // Lb * 3, SSD_XBC), F32)],
        compiler_params=_cparams("parallel"),
        name="ssd_sample",
    )(u, dt, hist, hin, sel, *params)


def _dot3s(a, b, dims=_NN):
    ka = dims[0][0][0]
    kb = dims[0][1][0]
    lhs = jnp.concatenate([a[0], a[0], a[1]], axis=ka)
    rhs = jnp.concatenate([b[0], b[1], b[0]], axis=kb)
    return lax.dot_general(lhs, rhs, dims, preferred_element_type=F32)


def _cat2(parts, axis):
    return (jnp.concatenate([p[0] for p in parts], axis=axis), jnp.concatenate([p[1] for p in parts], axis=axis))


def _tri_inverse(mats, lb):
    n = mats[0].shape[0]
    ri = _iota2((n, n), 0)
    ci = _iota2((n, n), 1)
    off1 = ((ri >> 1) == (ci >> 1)) & ((ri & 1) == 1) & ((ci & 1) == 0)
    eye = jnp.where(ri == ci, 1.0, 0.0)
    ts = [eye + jnp.where(off1, a, 0.0) for a in mats]
    for lvl in range(1, lb):
        m = 1 << lvl
        off = ((ri >> (lvl + 1)) == (ci >> (lvl + 1))) & ((ri & (2 * m - 1)) >= m) & ((ci & (2 * m - 1)) < m)
        tsb = [t.astype(BF16) for t in ts]
        ws = [jnp.dot(jnp.where(off, a, 0.0).astype(BF16), tb, preferred_element_type=F32)
              for a, tb in zip(mats, tsb)]
        ts = [t + jnp.dot(tb, w.astype(BF16), preferred_element_type=F32) for t, tb, w in zip(ts, tsb, ws)]
    return [t.astype(BF16) for t in ts]


def _refined_solve(tinvs, mats, rhss):
    d = functools.partial(jnp.dot, preferred_element_type=F32)
    n = range(len(mats))
    rs = [_split2(r) for r in rhss]
    u0 = [d(jnp.concatenate([tinvs[i], tinvs[i]], axis=1), jnp.concatenate(rs[i], axis=0)) for i in n]
    au = [_dot3s(_split2(mats[i]), _split2(u0[i])) for i in n]
    res = [((rhss[i] - u0[i]) + au[i]).astype(BF16) for i in n]
    return [u0[i] + d(tinvs[i], res[i]) for i in n]


def _wkv_kernel(*refs, C, Lb, prompt, G):
    nb = C // Lb
    lb = int(math.log2(Lb))
    R = 2 * C
    T = G * C
    if prompt:
        (u_ref, mu_ref, w0_ref, w2_ref, a0_ref, a2_ref, g2_ref, kk_ref, ka_ref, rk_ref, lnw_ref, lnb_ref,
         y_ref, sout_ref, carry_sc, s_sc) = refs
        c = pl.program_id(1)

        @pl.when(c == 0)
        def _():
            carry_sc[...] = jnp.zeros_like(carry_sc)
            s_sc[...] = jnp.zeros_like(s_sc)
    else:
        (u_ref, hist_ref, sin_ref, sel_ref, mu_ref, w0_ref, w2_ref, a0_ref, a2_ref, g2_ref, kk_ref, ka_ref,
         rk_ref, lnw_ref, lnb_ref, y_ref, sout_ref, shift_ref) = refs

    rows = _iota2((T, 1), 0)
    tpos = rows & (Lb - 1)
    if prompt:
        u = u_ref[...].astype(F32).reshape(T, SEG)
        hist = jnp.concatenate([jnp.broadcast_to(carry_sc[g, 7:8], (C, SEG)) for g in range(G)], axis=0)
        for g in range(G):
            carry_sc[g] = u[g * C + C - 8:(g + 1) * C]
    else:
        u = u_ref[...].astype(F32)
        hist = hist_ref[...]
        shift_ref[...] = _maskdot(sel_ref[...], u)
    um = u + (_shifted(u, 1, tpos, hist) - u) * mu_ref[...]

    W = RWKV_WIDTH
    r = um[:, 0:W]
    k = um[:, W:2 * W]
    v = um[:, 2 * W:3 * W]
    t12 = um[:, 3 * W:3 * W + LANES]
    lg = um[:, 3 * W + LANES:3 * W + 2 * LANES]

    wl = w0_ref[...] + _bdot(jnp.tanh(t12), w2_ref[...])
    logw = -math.exp(-0.5) * _sigmoid(wl)
    a = _sigmoid(a0_ref[...] + _bdot(t12, a2_ref[...]))
    out_gate = _bdot(_sigmoid(lg), g2_ref[...])

    hi = _iota2((W, W), 0)
    hj = _iota2((W, W), 1)
    headsum = ((hi >> 6) == (hj >> 6)).astype(BF16)

    kk = k * kk_ref[...]
    kk = kk * lax.rsqrt(jnp.maximum(_bdot(kk * kk, headsum), 1e-24))
    kmod = k * (1.0 + (a - 1.0) * ka_ref[...])
    beta = kk * a

    ri = _iota2((T, T), 0)
    ci = _iota2((T, T), 1)
    same_c = (ri >> lb) == (ci >> lb)
    lc = _maskdot((same_c & (ci <= ri)).astype(BF16), logw)
    e_pos = jnp.exp(lc)
    e_neg = jnp.exp(-lc)
    at = -kk * jnp.exp(lc - logw)
    rt = r * e_pos
    bt = beta * e_neg
    kt = kmod * e_neg

    si = _iota2((R, R), 0)
    sj = _iota2((R, R), 1)
    same_s = (si >> lb) == (sj >> lb)
    strict = same_s & (sj < si)
    incl = same_s & (sj <= si)
    incl2 = jnp.concatenate([incl, incl], axis=1)
    lane_row = _iota2((1, LANES), 1)
    m_lo = jnp.where(lane_row < 64, 1.0, 0.0).astype(BF16)
    m_hi = jnp.where(lane_row < 64, 0.0, 1.0).astype(BF16)
    if nb > 1:
        seq_s = (_iota2((R, LANES), 0) & (C - 1)) >> lb
        seq_masks = [jnp.where(seq_s == b, 1.0, 0.0).astype(BF16) for b in range(nb)]

    def stack(xp):
        return jnp.concatenate([xp * m_lo, xp * m_hi], axis=0)

    def expand(xs_):
        if nb == 1:
            return xs_
        return jnp.concatenate([xs_ * seq_masks[b] for b in range(nb)], axis=1)

    probs = [(g, p) for g in range(G) for p in range(4)]
    NP = range(len(probs))

    def tile(x, i):
        g, p = probs[i]
        return x[g * C:(g + 1) * C, p * LANES:(p + 1) * LANES]

    def stacked(pair, i):
        return (stack(tile(pair[0], i)), stack(tile(pair[1], i)))

    at2, v2, bt2, kt2 = (_split2(x) for x in (at, v, bt, kt))
    rt_b = rt.astype(BF16)
    a_s = [stacked(at2, i) for i in NP]
    r_s = [stack(tile(rt_b, i)) for i in NP]
    v_s = [stacked(v2, i) for i in NP]
    bk_s = [_cat2([stacked(bt2, i), stacked(kt2, i)], 0) for i in NP]
    gm_a = [_dot3s(a_s[i], bk_s[i], _NT) for i in NP]
    gm_r = [lax.dot_general(r_s[i], bk_s[i][0], _NT, preferred_element_type=F32) for i in NP]
    a_ab = [jnp.where(strict, gm_a[i][:, 0:R], 0.0) for i in NP]
    a_ak = [_split2(jnp.where(strict, gm_a[i][:, R:2 * R], 0.0)) for i in NP]
    a_r = [jnp.where(incl2, gm_r[i], 0.0).astype(BF16) for i in NP]
    akv = [_dot3s(a_ak[i], v_s[i]) for i in NP]
    tinv = _tri_inverse(a_ab, lb)

    if prompt:
        sst = [s_sc[i] for i in NP]
    else:
        z64 = jnp.zeros((64, 64), F32)

        def pair_blockdiag(b, p):
            top = jnp.concatenate([sin_ref[b, 2 * p], z64], axis=1)
            bot = jnp.concatenate([z64, sin_ref[b, 2 * p + 1]], axis=1)
            return jnp.concatenate([top, bot], axis=0)

        sst = [jnp.concatenate([pair_blockdiag(g * nb + b, p) for b in range(nb)], axis=1) for g, p in probs]
    ss = [_split2(s) for s in sst]
    ar0_a = [_dot3s((expand(a_s[i][0]), expand(a_s[i][1])), ss[i], _NT) for i in NP]
    ar0_r = [lax.dot_general(expand(r_s[i]), ss[i][0], _NT, preferred_element_type=F32) for i in NP]
    us = [_split2(x) for x in _refined_solve(tinv, a_ab, [ar0_a[i] + akv[i] for i in NP])]
    uv = [_cat2([us[i], v_s[i]], 0) for i in NP]
    yst = [ar0_r[i] + jnp.dot(a_r[i], uv[i][0], preferred_element_type=F32) for i in NP]
    ys = [yst[i][0:C] + yst[i][C:R] for i in NP]
    for i in NP:
        g, p = probs[i]
        pl_lanes = tile(e_pos, i)
        plast = jnp.concatenate([pl_lanes[b * Lb + Lb - 1:b * Lb + Lb] for b in range(nb)], axis=1)
        bk_x = tuple(jnp.concatenate([expand(x[0:R]), expand(x[R:2 * R])], axis=0) for x in bk_s[i])
        snew = (sst[i] + _dot3s(uv[i], bk_x, _TN)) * plast
        if prompt:
            s_sc[i] = snew
            sout_ref[g, p] = snew
        else:
            for b in range(nb):
                sout_ref[g * nb + b, 2 * p] = snew[0:64, b * LANES:b * LANES + 64]
                sout_ref[g * nb + b, 2 * p + 1] = snew[64:128, b * LANES + 64:(b + 1) * LANES]

    y = jnp.concatenate([jnp.concatenate(ys[4 * g:4 * g + 4], axis=1) for g in range(G)], axis=0)
    inv_d = 1.0 / 64.0
    mean = _bdot(y, headsum) * inv_d
    yc = y - mean
    var = _bdot(yc * yc, headsum) * inv_d
    yn = yc * lax.rsqrt(var + GN_EPS) * lnw_ref[...] + lnb_ref[...]
    yn = yn + _bdot(r * kmod * rk_ref[...], headsum) * v
    y_ref[...] = (yn * out_gate).astype(y_ref.dtype).reshape(y_ref.shape)


def _wkv_params(p):
    z64 = jnp.zeros((64, RWKV_WIDTH), F32)
    w2 = jnp.concatenate([p["rwkv_w2"], z64], axis=0).astype(BF16)
    a2 = jnp.concatenate([z64, p["rwkv_a2"]], axis=0).astype(BF16)
    row = lambda a: a.reshape(1, -1)
    return [row(p["rwkv_mu"]), row(p["rwkv_w0"]), w2, row(p["rwkv_a0"]), a2, p["rwkv_g2"].astype(BF16),
            row(p["rwkv_k_k"]), row(p["rwkv_k_a"]), row(p["rwkv_r_k"]), row(p["rwkv_ln_w"]), row(p["rwkv_ln_b"])]


def _wkv_prompt(u, p, batch, seq, C=64, G=8):
    nc = seq // C
    params = _wkv_params(p)
    y, sout = pl.pallas_call(
        functools.partial(_wkv_kernel, C=C, Lb=C, prompt=True, G=G),
        grid=(batch // G, nc),
        in_specs=[pl.BlockSpec((G, C, SEG), lambda b, c: (b, c, 1))] + [_const_spec(a) for a in params],
        out_specs=[pl.BlockSpec((G, C, RWKV_WIDTH), lambda b, c: (b, c, 0)),
                   pl.BlockSpec((G, 4, LANES, LANES), lambda b, c: (b, 0, 0, 0))],
        out_shape=[jax.ShapeDtypeStruct((batch, seq, RWKV_WIDTH), BF16),
                   jax.ShapeDtypeStruct((batch, 4, LANES, LANES), F32)],
        scratch_shapes=[pltpu.VMEM((G, 8, SEG), F32), pltpu.VMEM((4 * G, LANES, LANES), F32)],
        compiler_params=_cparams("parallel", "arbitrary"),
        name="wkv_prompt",
    )(u.reshape(batch, seq, 2 * SEG), *params)
    return y.reshape(batch * seq, RWKV_WIDTH), sout


def _wkv_sample(u, hist, sin, p, Lb, C=64, G=2):
    n = u.shape[0]
    nb = C // Lb
    T = G * C
    params = _wkv_params(p)
    sel = _row_selector(G * nb, Lb, (Lb - 1,))
    return pl.pallas_call(
        functools.partial(_wkv_kernel, C=C, Lb=Lb, prompt=False, G=G),
        grid=(n // T,),
        in_specs=[pl.BlockSpec((T, SEG), lambda i: (i, 1)),
                  pl.BlockSpec((T, SEG), lambda i: (i, 0)),
                  pl.BlockSpec((G * nb, 8, 64, 64), lambda i: (i, 0, 0, 0)),
                  _const_spec(sel)]
                 + [_const_spec(a) for a in params],
        out_specs=[pl.BlockSpec((T, RWKV_WIDTH), lambda i: (i, 0)),
                   pl.BlockSpec((G * nb, 8, 64, 64), lambda i: (i, 0, 0, 0)),
                   pl.BlockSpec((G * nb, SEG), lambda i: (i, 0))],
        out_shape=[jax.ShapeDtypeStruct((n, RWKV_WIDTH), BF16),
                   jax.ShapeDtypeStruct(sin.shape, F32),
                   jax.ShapeDtypeStruct((n // Lb, SEG), F32)],
        compiler_params=_cparams("parallel"),
        name="wkv_sample",
    )(u, hist, sin, sel, *params)


def _outproj_kernel(x_ref, ys_ref, yr_ref, wo_ref, g_ref, wq_ref, x1_ref, q_ref):
    wo = wo_ref[...]
    x1 = x_ref[...] + (_bdot(ys_ref[...], wo[0:SSD_WIDTH]) + _bdot(yr_ref[...], wo[SSD_WIDTH:]))
    x1_ref[...] = x1
    q_ref[...] = _bdot(_rms(x1, g_ref[...]), wq_ref[...])


def _outproj(x, y_ssd, y_rw, w_out, g, w_q, tm):
    n, d = x.shape
    return pl.pallas_call(
        _outproj_kernel,
        grid=(n // tm,),
        in_specs=[pl.BlockSpec((tm, d), lambda i: (i, 0)),
                  pl.BlockSpec((tm, SSD_WIDTH), lambda i: (i, 0)),
                  pl.BlockSpec((tm, RWKV_WIDTH), lambda i: (i, 0)),
                  _const_spec(w_out), pl.BlockSpec((1, d), lambda i: (0, 0)), _const_spec(w_q)],
        out_specs=[pl.BlockSpec((tm, d), lambda i: (i, 0)), pl.BlockSpec((tm, d), lambda i: (i, 0))],
        out_shape=[jax.ShapeDtypeStruct((n, d), F32), jax.ShapeDtypeStruct((n, d), F32)],
        compiler_params=_cparams("parallel"),
        name="outproj_q",
    )(x, y_ssd, y_rw, w_out, g.reshape(1, d), w_q)


def _xattn_cache_kernel(q_ref, k_ref, v_ref, o_ref, *, seq_len):
    tq = q_ref.shape[0]
    nmem = tq // seq_len
    rows = N_MEM * XA_HEADS
    cols = XA_HEADS * tq
    q = q_ref[...] * (XA_HEAD_DIM ** -0.5)
    qblk = jnp.concatenate([q[:, h * XA_HEAD_DIM:(h + 1) * XA_HEAD_DIM] for h in range(XA_HEADS)], axis=0)
    key_head = _iota2((cols, rows), 1) & (XA_HEADS - 1)
    qrow = _iota2((cols, 1), 0)
    head_ok = key_head == (qrow >> int(math.log2(tq)))
    out_mem = _iota2((tq, 1), 0) >> int(math.log2(seq_len))
    o = None
    for m in range(nmem):
        kall = k_ref[m].reshape(rows, XA_HEAD_DIM)
        vall = v_ref[m].reshape(rows, XA_HEAD_DIM)
        s = jnp.where(head_ok, _bdot(qblk, kall, _NT), -jnp.inf)
        e = jnp.exp(s - jnp.max(s, axis=-1, keepdims=True))
        pr = e / jnp.sum(e, axis=-1, keepdims=True)
        om = _bdot(pr, vall)
        om = jnp.concatenate([om[h * tq:(h + 1) * tq] for h in range(XA_HEADS)], axis=1)
        o = om if o is None else jnp.where(out_mem == m, om, o)
    o_ref[...] = o.astype(o_ref.dtype)


def _xattn_cache(q, ck, cv, tq, seq_len):
    n, d = q.shape
    nmem = tq // seq_len
    row_spec = pl.BlockSpec((tq, d), lambda i: (i, 0))
    kv_spec = pl.BlockSpec((nmem, N_MEM, XA_HEADS, XA_HEAD_DIM), lambda i: (i, 0, 0, 0))
    return pl.pallas_call(
        functools.partial(_xattn_cache_kernel, seq_len=seq_len),
        grid=(n // tq,),
        in_specs=[row_spec, kv_spec, kv_spec],
        out_specs=row_spec,
        out_shape=jax.ShapeDtypeStruct((n, d), BF16),
        compiler_params=_cparams("parallel"),
        name="xattn_cache",
    )(q, ck, cv)


FF_SUB = 256
FF_SKEW = 1


def _ffn_kernel(*refs, tm, Lb, prompt, blocks_per_seq):
    assert not prompt
    x1_ref, o_ref, wo_ref, hist_ref, g_ref, wu_ref, cw_ref, cb_ref, wd_ref, fg_ref, y_ref, s_ref = refs
    x2 = x1_ref[...] + jnp.dot(o_ref[...], wo_ref[...], preferred_element_type=F32)
    _ffn_body(x2, hist_ref, g_ref, wu_ref, cw_ref, cb_ref, wd_ref, fg_ref, y_ref, s_ref, None,
              tm=tm, Lb=Lb, prompt=False, blocks_per_seq=blocks_per_seq)


def _ffn_body(x, hist_ref, g_ref, wu_ref, cw_ref, cb_ref, wd_ref, fg_ref, y_ref, s_ref, carry_sc,
              *, tm, Lb, prompt, blocks_per_seq):
    if prompt:
        @pl.when(pl.program_id(0) % blocks_per_seq == 0)
        def _():
            carry_sc[...] = jnp.zeros_like(carry_sc)

    hn = _rms(x, g_ref[...]).astype(BF16)
    nsub = D_FF // FF_SUB
    r8 = _iota2((8, 1), 0)
    tpos = _iota2((tm, 1), 0) & (Lb - 1)
    if not prompt:
        trow = _iota2((tm, tm // Lb), 0)
        seq0 = _iota2((tm, tm // Lb), 1) * Lb
        at_row0 = (trow == seq0).astype(BF16)
        at_row1 = (trow == seq0 + 1).astype(BF16)

    def conv(up, cols):
        cw = cw_ref[:, cols]
        cb = cb_ref[:, cols]
        if prompt:
            c8 = carry_sc[:, cols]
            carry_sc[:, cols] = up[tm - 8:tm]
            s_ref[0, :, cols] = up[tm - 2:tm]
            top = up[0:8]
            p1 = jnp.where(r8 >= 1, pltpu.roll(top, 1, 0), c8[7:8])
            p2 = jnp.where(r8 >= 2, pltpu.roll(top, 2, 0), jnp.where(r8 == 0, c8[6:7], c8[7:8]))
            head = (cw[0:1] * p2 + cw[1:2] * p1 + cw[2:3] * top) + cb
            body = (cw[0:1] * pltpu.roll(up, 2, 0) + cw[1:2] * pltpu.roll(up, 1, 0) + cw[2:3] * up) + cb
            return jnp.concatenate([head, body[8:]], axis=0)
        for b in range(tm // Lb):
            s_ref[b, :, cols] = up[b * Lb + Lb - 2:b * Lb + Lb]
        st0, st1 = hist_ref[0, :, cols], hist_ref[1, :, cols]
        h1 = _maskdot(at_row0, st1)
        h2 = _maskdot(at_row0, st0) + _maskdot(at_row1, st1)
        return (cw[0:1] * _shifted(up, 2, tpos, h2) + cw[1:2] * _shifted(up, 1, tpos, h1) + cw[2:3] * up) + cb

    def up_proj(k):
        cg = slice(k * FF_SUB, (k + 1) * FF_SUB)
        cv = slice(D_FF + k * FF_SUB, D_FF + (k + 1) * FF_SUB)
        return (jnp.dot(hn, wu_ref[:, cg], preferred_element_type=F32),
                jnp.dot(hn, wu_ref[:, cv], preferred_element_type=F32))

    def gate_mul(k, ug, uv):
        cg = slice(k * FF_SUB, (k + 1) * FF_SUB)
        cv = slice(D_FF + k * FF_SUB, D_FF + (k + 1) * FF_SUB)
        gate = conv(ug, cg)
        return (gate * _sigmoid(gate) * conv(uv, cv)).astype(BF16)

    ups = {}
    acts = []
    for k in range(nsub + FF_SKEW):
        if k < nsub:
            ups[k] = up_proj(k)
        if k >= FF_SKEW:
            acts.append(gate_mul(k - FF_SKEW, *ups.pop(k - FF_SKEW)))
    acc = x + jnp.dot(jnp.concatenate(acts, axis=1), wd_ref[...], preferred_element_type=F32)
    y_ref[...] = _rms(acc, fg_ref[...])


def _resident_spec(a):
    nd = a.ndim
    return pl.BlockSpec(a.shape, lambda *_: (0,) * nd, pipeline_mode=pl.Buffered(1))


def _ffn_short(x1, o, w_o, hist, p, final_g, tm, Lb):
    n, d = x1.shape
    w_up = p["ffn_w_up"].astype(BF16)
    w_down = p["ffn_w_down"].astype(BF16)
    cw, cb = p["ffn_conv_w"], p["ffn_conv_b"].reshape(1, 2 * D_FF)
    row_spec = pl.BlockSpec((tm, d), lambda i: (i, 0))
    seq_spec = lambda lead: pl.BlockSpec((lead, tm // Lb, 2 * D_FF), lambda i: (0, i, 0))
    consts = [p["norm_ffn_w"].reshape(1, d), w_up, cw, cb, w_down, final_g.reshape(1, d)]
    return pl.pallas_call(
        functools.partial(_ffn_kernel, tm=tm, Lb=Lb, prompt=False, blocks_per_seq=1),
        grid=(n // tm,),
        in_specs=[row_spec, row_spec, _resident_spec(w_o), seq_spec(2)] + [_resident_spec(a) for a in consts],
        out_specs=[row_spec, pl.BlockSpec((tm // Lb, 2, 2 * D_FF), lambda i: (i, 0, 0))],
        out_shape=[jax.ShapeDtypeStruct((n, d), F32), jax.ShapeDtypeStruct((n // Lb, 2, 2 * D_FF), F32)],
        compiler_params=_cparams("arbitrary"),
        name="convffn",
    )(x1, o, w_o, hist, *consts)


def _post_mix_kernel(x_ref, ys_ref, yr_ref, k_ref, v_ref, wo_ref, gxa_ref, wq_ref, wao_ref,
                     g_ref, wu_ref, cw_ref, cb_ref, wd_ref, fg_ref, y_ref, s_ref, carry_sc,
                     *, tm, seq_len):
    wo = wo_ref[...]
    x1 = x_ref[...] + (jnp.dot(ys_ref[...], wo[0:SSD_WIDTH], preferred_element_type=F32)
                       + jnp.dot(yr_ref[...], wo[SSD_WIDTH:], preferred_element_type=F32))
    q = (_bdot(_rms(x1, gxa_ref[...]), wq_ref[...]) * (XA_HEAD_DIM ** -0.5)).astype(BF16)
    outs = []
    for h in range(XA_HEADS):
        sl = slice(h * XA_HEAD_DIM, (h + 1) * XA_HEAD_DIM)
        s = lax.dot_general(q[:, sl], k_ref[0, :, sl].astype(BF16), _NT, preferred_element_type=F32)
        s = s - jnp.max(s, axis=-1, keepdims=True)
        e = jnp.exp(s)
        pr = e / jnp.sum(e, axis=-1, keepdims=True)
        outs.append(_bdot(pr, v_ref[0, :, sl]))
    x2 = x1 + _bdot(jnp.concatenate(outs, axis=1), wao_ref[...])
    _ffn_body(x2, None, g_ref, wu_ref, cw_ref, cb_ref, wd_ref, fg_ref, y_ref, s_ref, carry_sc,
              tm=tm, Lb=seq_len, prompt=True, blocks_per_seq=seq_len // tm)


def _post_mix(x, y_ssd, y_rw, mk, mv, w_out, g_xa, w_q, w_o, p, final_g, tm, seq_len):
    n, d = x.shape
    bps = seq_len // tm
    row = lambda w: pl.BlockSpec((tm, w), lambda i: (i, 0))
    kv_spec = pl.BlockSpec((1, N_MEM, d), lambda i: (i // bps, 0, 0))
    consts = [w_out, g_xa.reshape(1, d), w_q, w_o, p["norm_ffn_w"].reshape(1, d), p["ffn_w_up"].astype(BF16),
              p["ffn_conv_w"], p["ffn_conv_b"].reshape(1, 2 * D_FF), p["ffn_w_down"].astype(BF16),
              final_g.reshape(1, d)]
    y, st = pl.pallas_call(
        functools.partial(_post_mix_kernel, tm=tm, seq_len=seq_len),
        grid=(n // tm,),
        in_specs=[row(d), row(SSD_WIDTH), row(RWKV_WIDTH), kv_spec, kv_spec] + [_resident_spec(a) for a in consts],
        out_specs=[row(d), pl.BlockSpec((1, 2, 2 * D_FF), lambda i: (i, 0, 0))],
        out_shape=[jax.ShapeDtypeStruct((n, d), F32), jax.ShapeDtypeStruct((n // tm, 2, 2 * D_FF), F32)],
        scratch_shapes=[pltpu.VMEM((8, 2 * D_FF), F32)],
        compiler_params=_cparams("arbitrary"),
        name="post_mix",
    )(x, y_ssd, y_rw, mk, mv, *consts)
    return y, st[bps - 1::bps]


def _pair_blocks(sbd):
    b = sbd.shape[0]
    s0 = sbd[:, :, 0:64, 0:64]
    s1 = sbd[:, :, 64:128, 64:128]
    return jnp.stack([s0, s1], axis=2).reshape(b, 8, 64, 64)


def _hist_rows(state, seq_len, k):
    b, w, c = state.shape
    cols = [state[:, w + t - k] if t < k else jnp.zeros((b, c), state.dtype) for t in range(seq_len)]
    return jnp.stack(cols, axis=1).reshape(b * seq_len, c)


def kernel(x_prompt, x_sample, mem_prompt, state_ssm_conv, state_ssm, state_shift, state_wkv, state_ffn_conv, cache_mem_k, cache_mem_v, norm_mix_w, w_in, ssd_conv_w, ssd_conv_b, ssd_dt_bias, ssd_a_log, ssd_d, ssd_norm_w, rwkv_mu, rwkv_w0, rwkv_w2, rwkv_a0, rwkv_a2, rwkv_g2, rwkv_k_k, rwkv_k_a, rwkv_r_k, rwkv_ln_w, rwkv_ln_b, w_out, norm_xa_w, mem_norm_w, xa_w_q, xa_w_k, xa_w_v, xa_w_o, norm_ffn_w, ffn_w_up, ffn_conv_w, ffn_conv_b, ffn_w_down, final_norm_w):
    depth = w_in.shape[0]
    assert depth == 1, "final rmsnorm is fused into the (single) layer's ConvFFN kernel"
    bp, lp, d = x_prompt.shape
    bs, ls, _ = x_sample.shape
    i = 0
    p = dict(ssd_conv_w=ssd_conv_w[i], ssd_conv_b=ssd_conv_b[i], ssd_dt_bias=ssd_dt_bias[i],
             ssd_a_log=ssd_a_log[i], ssd_d=ssd_d[i], ssd_norm_w=ssd_norm_w[i], rwkv_mu=rwkv_mu[i],
             rwkv_w0=rwkv_w0[i], rwkv_w2=rwkv_w2[i], rwkv_a0=rwkv_a0[i], rwkv_a2=rwkv_a2[i],
             rwkv_g2=rwkv_g2[i], rwkv_k_k=rwkv_k_k[i], rwkv_k_a=rwkv_k_a[i],
             rwkv_r_k=rwkv_r_k[i].reshape(-1), rwkv_ln_w=rwkv_ln_w[i], rwkv_ln_b=rwkv_ln_b[i],
             norm_ffn_w=norm_ffn_w[i], ffn_w_up=ffn_w_up[i], ffn_conv_w=ffn_conv_w[i],
             ffn_conv_b=ffn_conv_b[i], ffn_w_down=ffn_w_down[i])

    w_in_t = jnp.swapaxes(w_in[i], 0, 1)
    w_in_p = jnp.concatenate([w_in_t[:SSD_PROJ], jnp.zeros((SEG - SSD_PROJ, d), F32),
                              w_in_t[SSD_PROJ:]], axis=0).astype(BF16)
    w_out_b = w_out[i].astype(BF16)
    w_q_b = (xa_w_q[i]).astype(BF16)
    w_o_b = xa_w_o[i].astype(BF16)
    w_kv_b = jnp.concatenate([xa_w_k[i], xa_w_v[i]], axis=1).astype(BF16)

    xp = x_prompt.reshape(bp * lp, d)
    xs = x_sample.reshape(bs * ls, d)

    mk, mv, mk4, mv4 = _mem_kv(mem_prompt.reshape(bp * N_MEM, d), mem_norm_w[i], w_kv_b, 512)
    mk = mk.reshape(bp, N_MEM, d)
    mv = mv.reshape(bp, N_MEM, d)
    tm_p = min(1024, lp)
    u_p, dt_p, tail_p = _norm_proj(xp, norm_mix_w[i], w_in_p, tm_p, BF16)
    y_ssd_p, ssm_p = _ssd_prompt(u_p, dt_p, p, bp, lp, G=math.gcd(4, bp))
    y_rw_p, wkv_bd_p = _wkv_prompt(u_p, p, bp, lp, G=math.gcd(8, bp))
    y_p, ffn_conv_p = _post_mix(xp, y_ssd_p, y_rw_p, mk, mv, w_out_b, norm_xa_w[i], w_q_b, w_o_b, p,
                                final_norm_w, 512, lp)
    seq_tail = tail_p[lp // tm_p - 1::lp // tm_p]
    ssm_conv_p = seq_tail[:, 5:, SSD_WIDTH:SSD_WIDTH + SSD_XBC]
    shift_p = seq_tail[:, 7, SEG:]

    u_s, dt_s, _ = _norm_proj(xs, norm_mix_w[i], w_in_p, 512, F32)
    conv_state = state_ssm_conv[i]
    hist_conv = jnp.stack([_hist_rows(conv_state, ls, k) for k in (1, 2, 3)], axis=0)
    assert ls >= 3, "the new conv state is read from the last three rows of each sample sequence"
    y_ssd_s, ssm_s, conv_rows_s = _ssd_sample(u_s, dt_s, hist_conv, state_ssm[i].reshape(bs, 4, LANES, LANES), p, ls)
    ssm_conv_s = conv_rows_s.reshape(bs, 3, SSD_XBC)
    hist_shift = _hist_rows(state_shift[i][:, None, :], ls, 1)
    y_rw_s, wkv_s, shift_s = _wkv_sample(u_s, hist_shift, state_wkv[i], p, ls)
    x1_s, q_s = _outproj(xs, y_ssd_s, y_rw_s, w_out_b, norm_xa_w[i], w_q_b, 512)
    o_s = _xattn_cache(q_s, cache_mem_k[i], cache_mem_v[i], 32, ls)
    y_s, ffn_conv_s = _ffn_short(x1_s, o_s, w_o_b, jnp.swapaxes(state_ffn_conv[i], 0, 1), p, final_norm_w, 256, ls)

    e = lambda a: a[None]
    return (y_p.reshape(bp, lp, d), y_s.reshape(bs, ls, d),
            e(ssm_conv_p), e(ssm_conv_s),
            e(ssm_p.reshape(bp, SSD_HEADS, 64, SSD_STATE)), e(ssm_s.reshape(bs, SSD_HEADS, 64, SSD_STATE)),
            e(shift_p), e(shift_s),
            e(_pair_blocks(wkv_bd_p)), e(wkv_s),
            e(ffn_conv_p), e(ffn_conv_s),
            e(mk4.reshape(bp, N_MEM, XA_HEADS, XA_HEAD_DIM)), e(mv4.reshape(bp, N_MEM, XA_HEADS, XA_HEAD_DIM)))
```

```python
import functools
import math

import jax
import jax.numpy as jnp
from jax import lax
from jax.experimental import pallas as pl
from jax.experimental.pallas import tpu as pltpu

F32 = jnp.float32
BF16 = jnp.bfloat16

D_MODEL = 1024
N_MEM = 256
XA_HEADS = 4
XA_HEAD_DIM = D_MODEL // XA_HEADS
SSD_WIDTH = 512
SSD_HEADS = 8
SSD_STATE = 128
SSD_XBC = 1024
SSD_PROJ = SSD_WIDTH + SSD_XBC + SSD_HEADS
RWKV_WIDTH = 512
RWKV_PROJ = 1792
D_FF = 2816
EPS = 1e-6
GN_EPS = 64e-5

LANES = 128
SEG = 1792
DT_OFF = SSD_WIDTH + SSD_XBC
VMEM_LIMIT_BYTES = 56 * 1024 * 1024


def _cparams(*sem):
    return pltpu.CompilerParams(dimension_semantics=sem, vmem_limit_bytes=VMEM_LIMIT_BYTES)


_NN = (((1,), (0,)), ((), ()))
_NT = (((1,), (1,)), ((), ()))
_TN = (((0,), (0,)), ((), ()))


def _bdot(a, b, dims=_NN):
    return lax.dot_general(a.astype(BF16), b.astype(BF16), dims, preferred_element_type=F32)


def _split2(x):
    hi = x.astype(BF16)
    lo = (x - hi.astype(F32)).astype(BF16)
    return hi, lo


def _maskdot(m_bf16, x):
    hi = x.astype(BF16)
    r1 = x - hi.astype(F32)
    mid = r1.astype(BF16)
    lo = (r1 - mid.astype(F32)).astype(BF16)
    return jnp.dot(jnp.concatenate([m_bf16, m_bf16, m_bf16], axis=1),
                   jnp.concatenate([hi, mid, lo], axis=0), preferred_element_type=F32)


def _sigmoid(x):
    return 0.5 + 0.5 * jnp.tanh(0.5 * x)


def _softplus(x):
    return jnp.maximum(x, 0.0) + jnp.log1p(jnp.exp(-jnp.abs(x)))


def _rms(x, g):
    return x * lax.rsqrt(jnp.mean(x * x, axis=-1, keepdims=True) + EPS) * g


def _iota2(shape, dim):
    return lax.broadcasted_iota(jnp.int32, shape, dim)


def _place_rows(rows, nseq, Lb, t, state_rows):
    r = _iota2((rows, nseq), 0)
    first = _iota2((rows, nseq), 1) * Lb
    return _maskdot((r == first + t).astype(BF16), state_rows)


def _shifted(x, k, tpos, hist):
    return jnp.where(tpos >= k, pltpu.roll(x, k, 0), hist)


def _norm_proj_kernel(x_ref, g_ref, wt_ref, o_ref, dt_ref, tail_ref, hn_sc):
    j = pl.program_id(1)

    @pl.when(j == 0)
    def _():
        hn_sc[...] = _rms(x_ref[...], g_ref[...]).astype(BF16)

    acc = lax.dot_general(hn_sc[...], wt_ref[...], _NT, preferred_element_type=F32)
    o_ref[...] = acc.astype(o_ref.dtype)
    tail_ref[0] = acc[acc.shape[0] - 8:]

    @pl.when(j == 0)
    def _():
        dt_ref[...] = acc[:, DT_OFF:DT_OFF + LANES]


def _norm_proj(x, g, wt_bf16, tm, out_dtype):
    n, d = x.shape
    return pl.pallas_call(
        _norm_proj_kernel,
        grid=(n // tm, 2),
        in_specs=[pl.BlockSpec((tm, d), lambda i, j: (i, 0)),
                  pl.BlockSpec((1, d), lambda i, j: (0, 0)),
                  pl.BlockSpec((SEG, d), lambda i, j: (j, 0))],
        out_specs=[pl.BlockSpec((tm, SEG), lambda i, j: (i, j)),
                   pl.BlockSpec((tm, LANES), lambda i, j: (i, 0)),
                   pl.BlockSpec((1, 8, SEG), lambda i, j: (i, 0, j))],
        out_shape=[jax.ShapeDtypeStruct((n, 2 * SEG), out_dtype),
                   jax.ShapeDtypeStruct((n, LANES), F32),
                   jax.ShapeDtypeStruct((n // tm, 8, 2 * SEG), F32)],
        scratch_shapes=[pltpu.VMEM((tm, d), BF16)],
        compiler_params=_cparams("parallel", "arbitrary"),
        name="norm_proj",
    )(x, g.reshape(1, d), wt_bf16)


def _mem_kv_kernel(x_ref, g_ref, w_ref, k_ref, v_ref, k4_ref, v4_ref):
    hn = _rms(x_ref[...], g_ref[...]).astype(BF16)
    d = k_ref.shape[1]
    k = jnp.dot(hn, w_ref[:, 0:d], preferred_element_type=F32)
    v = jnp.dot(hn, w_ref[:, d:2 * d], preferred_element_type=F32)
    k_ref[...] = k
    v_ref[...] = v
    for h in range(XA_HEADS):
        sl = slice(h * XA_HEAD_DIM, (h + 1) * XA_HEAD_DIM)
        k4_ref[:, h, :] = k[:, sl]
        v4_ref[:, h, :] = v[:, sl]


def _mem_kv(x, g, w_kv_bf16, tm):
    n, d = x.shape
    row_spec = pl.BlockSpec((tm, d), lambda i: (i, 0))
    head_spec = pl.BlockSpec((tm, XA_HEADS, XA_HEAD_DIM), lambda i: (i, 0, 0))
    return pl.pallas_call(
        _mem_kv_kernel,
        grid=(n // tm,),
        in_specs=[row_spec, pl.BlockSpec((1, d), lambda i: (0, 0)), _const_spec(w_kv_bf16)],
        out_specs=[row_spec, row_spec, head_spec, head_spec],
        out_shape=[jax.ShapeDtypeStruct((n, d), F32)] * 2
                  + [jax.ShapeDtypeStruct((n, XA_HEADS, XA_HEAD_DIM), F32)] * 2,
        compiler_params=_cparams("parallel"),
        name="mem_kv",
    )(x, g.reshape(1, d), w_kv_bf16)


def _ssd_kernel(*refs, Q, Lb, prompt, G=1):
    if not prompt:
        return _ssd_block(*refs, Q=Q, Lb=Lb, prompt=False)
    (u_ref, dt_ref, cw_ref, cb_ref, dtb_ref, an_ref, dsk_ref, nw_ref, y_ref, hout_ref, carry_sc, h_sc) = refs

    @pl.when(pl.program_id(1) == 0)
    def _():
        carry_sc[...] = jnp.zeros_like(carry_sc)
        h_sc[...] = jnp.zeros_like(h_sc)

    for gi in range(G):
        _ssd_block(u_ref.at[gi], dt_ref.at[gi], cw_ref, cb_ref, dtb_ref, an_ref, dsk_ref, nw_ref,
                   y_ref.at[gi], hout_ref.at[pl.ds(gi, 1)], carry_sc.at[gi], h_sc.at[pl.ds(4 * gi, 4)],
                   Q=Q, Lb=Lb, prompt=True)


def _ssd_block(*refs, Q, Lb, prompt):
    ns = Q // Lb
    lb = int(math.log2(Lb))
    rows = _iota2((Q, 1), 0)
    tpos = rows & (Lb - 1)
    if prompt:
        (u_ref, dt_ref, cw_ref, cb_ref, dtb_ref, an_ref, dsk_ref, nw_ref,
         y_ref, hout_ref, carry_sc, h_sc) = refs
    else:
        (u_ref, dt_ref, hist_ref, hin_ref, sel_ref, cw_ref, cb_ref, dtb_ref, an_ref, dsk_ref, nw_ref,
         y_ref, hout_ref, cs_ref) = refs
    u = u_ref[...].astype(F32)
    z = u[:, 0:SSD_WIDTH]
    z_gate = z * _sigmoid(z)
    x = u[:, SSD_WIDTH:SSD_WIDTH + SSD_XBC]
    if prompt:
        c8 = carry_sc[...]
        h1 = c8[7:8]
        h2 = jnp.where(rows == 0, c8[6:7], c8[7:8])
        h3 = jnp.where(rows == 0, c8[5:6], jnp.where(rows == 1, c8[6:7], c8[7:8]))
        carry_sc[...] = x[Q - 8:Q]
    else:
        cs_ref[...] = _maskdot(sel_ref[...], x)
        st = [hist_ref[j] for j in range(3)]
        put = functools.partial(_place_rows, Q, ns, Lb)
        h1 = put(0, st[2])
        h2 = put(0, st[1]) + put(1, st[2])
        h3 = put(0, st[0]) + put(1, st[1]) + put(2, st[2])
    cw = cw_ref[...]
    xc = (cw[0:1] * _shifted(x, 3, tpos, h3) + cw[1:2] * _shifted(x, 2, tpos, h2)
          + cw[2:3] * _shifted(x, 1, tpos, h1) + cw[3:4] * x) + cb_ref[...]
    xc = xc * _sigmoid(xc)
    dtr = dt_ref[...]
    xs = xc[:, 0:SSD_WIDTH]
    bm = xc[:, SSD_WIDTH:SSD_WIDTH + 2 * SSD_STATE]
    cm = xc[:, SSD_WIDTH + 2 * SSD_STATE:]

    dt = _softplus(dtr + dtb_ref[...])
    da = dt * an_ref[...]

    ri = _iota2((Q, Q), 0)
    ci = _iota2((Q, Q), 1)
    same = (ri >> lb) == (ci >> lb)
    tril = same & (ci <= ri)
    sel = ci == (((ri >> lb) << lb) + (Lb - 1))
    acs = _maskdot(tril.astype(BF16), da)
    acs_t = acs.T
    acs_last = _maskdot(sel.astype(BF16), acs)
    dec_end = jnp.exp(acs_last - acs)
    eacs = jnp.exp(acs)
    seqid = rows >> lb

    lane = _iota2((Q, LANES), 1)
    lo_half = lane < 64
    prow = _iota2((LANES, LANES), 0)
    dsk = dsk_ref[...]

    ys = []
    for q in range(4):
        g = q // 2
        h0, h1i = 2 * q, 2 * q + 1
        if q % 2 == 0:
            cg = cm[:, g * SSD_STATE:(g + 1) * SSD_STATE]
            bg = bm[:, g * SSD_STATE:(g + 1) * SSD_STATE]
            cb_g = _bdot(cg, bg, _NT)
            if ns == 1:
                cexp, bexp = cg.astype(BF16), bg.astype(BF16)
            else:
                cexp = jnp.concatenate([jnp.where(seqid == b, cg, 0.0).astype(BF16) for b in range(ns)], axis=1)
                bexp = jnp.concatenate([jnp.where(seqid == b, bg, 0.0).astype(BF16) for b in range(ns)], axis=1)
        m0 = jnp.where(tril, cb_g * jnp.exp(acs[:, h0:h0 + 1] - acs_t[h0:h0 + 1, :]), 0.0)
        m1 = jnp.where(tril, cb_g * jnp.exp(acs[:, h1i:h1i + 1] - acs_t[h1i:h1i + 1, :]), 0.0)
        xp = xs[:, q * LANES:(q + 1) * LANES]
        xdt = xp * jnp.where(lo_half, dt[:, h0:h0 + 1], dt[:, h1i:h1i + 1])
        xdt0 = jnp.where(lo_half, xdt, 0.0)
        xdt1 = jnp.where(lo_half, 0.0, xdt)
        ydiag = _bdot(jnp.concatenate([m0, m1], axis=1), jnp.concatenate([xdt0, xdt1], axis=0))
        if prompt:
            hst = h_sc[q]
        else:
            hst = jnp.concatenate([hin_ref[b, q] for b in range(ns)], axis=1)
        ecs = jnp.where(lo_half, eacs[:, h0:h0 + 1], eacs[:, h1i:h1i + 1])
        yoff = _bdot(cexp, hst, _NT) * ecs
        xd = xdt * jnp.where(lo_half, dec_end[:, h0:h0 + 1], dec_end[:, h1i:h1i + 1])
        incr = _bdot(xd, bexp, _TN)
        scales = []
        for b in range(ns):
            r = b * Lb + Lb - 1
            e = eacs[r:r + 1, :]
            scales.append(jnp.where(prow < 64, e[:, h0:h0 + 1], e[:, h1i:h1i + 1]))
        scale = scales[0] if ns == 1 else jnp.concatenate(scales, axis=1)
        hnew = hst * scale + incr
        if prompt:
            h_sc[q] = hnew
            hout_ref[0, q] = hnew
        else:
            for b in range(ns):
                hout_ref[b, q] = hnew[:, b * LANES:(b + 1) * LANES]
        ys.append(ydiag + yoff + dsk[:, q * LANES:(q + 1) * LANES] * xp)

    y = jnp.concatenate(ys, axis=1)
    y = y * z_gate
    half = SSD_WIDTH // 2
    outs = []
    for g in range(2):
        yg = y[:, g * half:(g + 1) * half]
        outs.append(yg * lax.rsqrt(jnp.mean(yg * yg, axis=-1, keepdims=True) + EPS))
    y_ref[...] = (jnp.concatenate(outs, axis=1) * nw_ref[...]).astype(y_ref.dtype)


def _ssd_params(p):
    an = jnp.zeros((1, LANES), F32).at[0, :SSD_HEADS].set(-jnp.exp(p["ssd_a_log"]))
    dtb = jnp.zeros((1, LANES), F32).at[0, :SSD_HEADS].set(p["ssd_dt_bias"])
    dsk = jnp.repeat(p["ssd_d"], SSD_WIDTH // SSD_HEADS).reshape(1, SSD_WIDTH)
    return [p["ssd_conv_w"], p["ssd_conv_b"].reshape(1, SSD_XBC), dtb, an, dsk,
            p["ssd_norm_w"].reshape(1, SSD_WIDTH)]


def _const_spec(a):
    nd = a.ndim
    return pl.BlockSpec(a.shape, lambda *_: (0,) * nd)


def _ssd_prompt(u, dt, p, batch, seq, Q=128, G=4):
    nc = seq // Q
    params = _ssd_params(p)
    y, hout = pl.pallas_call(
        functools.partial(_ssd_kernel, Q=Q, Lb=Q, prompt=True, G=G),
        grid=(batch // G, nc),
        in_specs=[pl.BlockSpec((G, Q, SEG), lambda b, c: (b, c, 0)),
                  pl.BlockSpec((G, Q, LANES), lambda b, c: (b, c, 0))] + [_const_spec(a) for a in params],
        out_specs=[pl.BlockSpec((G, Q, SSD_WIDTH), lambda b, c: (b, c, 0)),
                   pl.BlockSpec((G, 4, LANES, LANES), lambda b, c: (b, 0, 0, 0))],
        out_shape=[jax.ShapeDtypeStruct((batch, seq, SSD_WIDTH), BF16),
                   jax.ShapeDtypeStruct((batch, 4, LANES, LANES), F32)],
        scratch_shapes=[pltpu.VMEM((G, 8, SSD_XBC), F32), pltpu.VMEM((4 * G, LANES, LANES), F32)],
        compiler_params=_cparams("parallel", "arbitrary"),
        name="ssd_prompt",
    )(u.reshape(batch, seq, 2 * SEG), dt.reshape(batch, seq, LANES), *params)
    return y.reshape(batch * seq, SSD_WIDTH), hout


def _row_selector(nseq, Lb, offsets):
    k = len(offsets)
    rows = jnp.arange(nseq * k)
    target = (rows // k) * Lb + jnp.asarray(offsets)[rows % k]
    return (target[:, None] == jnp.arange(nseq * Lb)[None, :]).astype(BF16)


def _ssd_sample(u, dt, hist, hin, p, Lb, Q=64):
    n = u.shape[0]
    ns = Q // Lb
    params = _ssd_params(p)
    sel = _row_selector(ns, Lb, (Lb - 3, Lb - 2, Lb - 1))
    return pl.pallas_call(
        functools.partial(_ssd_kernel, Q=Q, Lb=Lb, prompt=False),
        grid=(n // Q,),
        in_specs=[pl.BlockSpec((Q, SEG), lambda i: (i, 0)),
                  pl.BlockSpec((Q, LANES), lambda i: (i, 0)),
                  pl.BlockSpec((3, ns, SSD_XBC), lambda i: (0, i, 0)),
                  pl.BlockSpec((ns, 4, LANES, LANES), lambda i: (i, 0, 0, 0)),
                  _const_spec(sel)]
                 + [_const_spec(a) for a in params],
        out_specs=[pl.BlockSpec((Q, SSD_WIDTH), lambda i: (i, 0)),
                   pl.BlockSpec((ns, 4, LANES, LANES), lambda i: (i, 0, 0, 0)),
                   pl.BlockSpec((ns * 3, SSD_XBC), lambda i: (i, 0))],
        out_shape=[jax.ShapeDtypeStruct((n, SSD_WIDTH), BF16),
                   jax.ShapeDtypeStruct(hin.shape, F32),
                   jax.ShapeDtypeStruct((n // Lb * 3, SSD_XBC), F32)],
        compiler_params=_cparams("parallel"),
        name="ssd_sample",
    )(u, dt, hist, hin, sel, *params)


def _dot3s(a, b, dims=_NN):
    ka = dims[0][0][0]
    kb = dims[0][1][0]
    lhs = jnp.concatenate([a[0], a[0], a[1]], axis=ka)
    rhs = jnp.concatenate([b[0], b[1], b[0]], axis=kb)
    return lax.dot_general(lhs, rhs, dims, preferred_element_type=F32)


def _cat2(parts, axis):
    return (jnp.concatenate([p[0] for p in parts], axis=axis), jnp.concatenate([p[1] for p in parts], axis=axis))


def _tri_inverse(mats, lb):
    n = mats[0].shape[0]
    ri = _iota2((n, n), 0)
    ci = _iota2((n, n), 1)
    off1 = ((ri >> 1) == (ci >> 1)) & ((ri & 1) == 1) & ((ci & 1) == 0)
    eye = jnp.where(ri == ci, 1.0, 0.0)
    ts = [eye + jnp.where(off1, a, 0.0) for a in mats]
    for lvl in range(1, lb):
        m = 1 << lvl
        off = ((ri >> (lvl + 1)) == (ci >> (lvl + 1))) & ((ri & (2 * m - 1)) >= m) & ((ci & (2 * m - 1)) < m)
        tsb = [t.astype(BF16) for t in ts]
        ws = [jnp.dot(jnp.where(off, a, 0.0).astype(BF16), tb, preferred_element_type=F32)
              for a, tb in zip(mats, tsb)]
        ts = [t + jnp.dot(tb, w.astype(BF16), preferred_element_type=F32) for t, tb, w in zip(ts, tsb, ws)]
    return [t.astype(BF16) for t in ts]


def _refined_solve(tinvs, mats, rhss):
    d = functools.partial(jnp.dot, preferred_element_type=F32)
    n = range(len(mats))
    rs = [_split2(r) for r in rhss]
    u0 = [d(jnp.concatenate([tinvs[i], tinvs[i]], axis=1), jnp.concatenate(rs[i], axis=0)) for i in n]
    au = [_dot3s(_split2(mats[i]), _split2(u0[i])) for i in n]
    res = [((rhss[i] - u0[i]) + au[i]).astype(BF16) for i in n]
    return [u0[i] + d(tinvs[i], res[i]) for i in n]


def _wkv_kernel(*refs, C, Lb, prompt, G):
    nb = C // Lb
    lb = int(math.log2(Lb))
    R = 2 * C
    T = G * C
    if prompt:
        (u_ref, mu_ref, w0_ref, w2_ref, a0_ref, a2_ref, g2_ref, kk_ref, ka_ref, rk_ref, lnw_ref, lnb_ref,
         y_ref, sout_ref, carry_sc, s_sc) = refs
        c = pl.program_id(1)

        @pl.when(c == 0)
        def _():
            carry_sc[...] = jnp.zeros_like(carry_sc)
            s_sc[...] = jnp.zeros_like(s_sc)
    else:
        (u_ref, hist_ref, sin_ref, sel_ref, mu_ref, w0_ref, w2_ref, a0_ref, a2_ref, g2_ref, kk_ref, ka_ref,
         rk_ref, lnw_ref, lnb_ref, y_ref, sout_ref, shift_ref) = refs

    rows = _iota2((T, 1), 0)
    tpos = rows & (Lb - 1)
    if prompt:
        u = u_ref[...].astype(F32).reshape(T, SEG)
        hist = jnp.concatenate([jnp.broadcast_to(carry_sc[g, 7:8], (C, SEG)) for g in range(G)], axis=0)
        for g in range(G):
            carry_sc[g] = u[g * C + C - 8:(g + 1) * C]
    else:
        u = u_ref[...].astype(F32)
        hist = _place_rows(T, G * nb, Lb, 0, hist_ref[...])
        shift_ref[...] = _maskdot(sel_ref[...], u)
    um = u + (_shifted(u, 1, tpos, hist) - u) * mu_ref[...]

    W = RWKV_WIDTH
    r = um[:, 0:W]
    k = um[:, W:2 * W]
    v = um[:, 2 * W:3 * W]
    t12 = um[:, 3 * W:3 * W + LANES]
    lg = um[:, 3 * W + LANES:3 * W + 2 * LANES]

    wl = w0_ref[...] + _bdot(jnp.tanh(t12), w2_ref[...])
    logw = -math.exp(-0.5) * _sigmoid(wl)
    a = _sigmoid(a0_ref[...] + _bdot(t12, a2_ref[...]))
    out_gate = _bdot(_sigmoid(lg), g2_ref[...])

    hi = _iota2((W, W), 0)
    hj = _iota2((W, W), 1)
    headsum = ((hi >> 6) == (hj >> 6)).astype(BF16)

    kk = k * kk_ref[...]
    kk = kk * lax.rsqrt(jnp.maximum(_bdot(kk * kk, headsum), 1e-24))
    kmod = k * (1.0 + (a - 1.0) * ka_ref[...])
    beta = kk * a

    ri = _iota2((T, T), 0)
    ci = _iota2((T, T), 1)
    same_c = (ri >> lb) == (ci >> lb)
    lc = _maskdot((same_c & (ci <= ri)).astype(BF16), logw)
    e_pos = jnp.exp(lc)
    e_neg = jnp.exp(-lc)
    at = -kk * jnp.exp(lc - logw)
    rt = r * e_pos
    bt = beta * e_neg
    kt = kmod * e_neg

    si = _iota2((R, R), 0)
    sj = _iota2((R, R), 1)
    same_s = (si >> lb) == (sj >> lb)
    strict = same_s & (sj < si)
    incl = same_s & (sj <= si)
    incl2 = jnp.concatenate([incl, incl], axis=1)
    lane_row = _iota2((1, LANES), 1)
    m_lo = jnp.where(lane_row < 64, 1.0, 0.0).astype(BF16)
    m_hi = jnp.where(lane_row < 64, 0.0, 1.0).astype(BF16)
    if nb > 1:
        seq_s = (_iota2((R, LANES), 0) & (C - 1)) >> lb
        seq_masks = [jnp.where(seq_s == b, 1.0, 0.0).astype(BF16) for b in range(nb)]

    def stack(xp):
        return jnp.concatenate([xp * m_lo, xp * m_hi], axis=0)

    def expand(xs_):
        if nb == 1:
            return xs_
        return jnp.concatenate([xs_ * seq_masks[b] for b in range(nb)], axis=1)

    probs = [(g, p) for g in range(G) for p in range(4)]
    NP = range(len(probs))

    def tile(x, i):
        g, p = probs[i]
        return x[g * C:(g + 1) * C, p * LANES:(p + 1) * LANES]

    def stacked(pair, i):
        return (stack(tile(pair[0], i)), stack(tile(pair[1], i)))

    at2, v2, bt2, kt2 = (_split2(x) for x in (at, v, bt, kt))
    rt_b = rt.astype(BF16)
    a_s = [stacked(at2, i) for i in NP]
    r_s = [stack(tile(rt_b, i)) for i in NP]
    v_s = [stacked(v2, i) for i in NP]
    bk_s = [_cat2([stacked(bt2, i), stacked(kt2, i)], 0) for i in NP]
    gm_a = [_dot3s(a_s[i], bk_s[i], _NT) for i in NP]
    gm_r = [lax.dot_general(r_s[i], bk_s[i][0], _NT, preferred_element_type=F32) for i in NP]
    a_ab = [jnp.where(strict, gm_a[i][:, 0:R], 0.0) for i in NP]
    a_ak = [_split2(jnp.where(strict, gm_a[i][:, R:2 * R], 0.0)) for i in NP]
    a_r = [jnp.where(incl2, gm_r[i], 0.0).astype(BF16) for i in NP]
    akv = [_dot3s(a_ak[i], v_s[i]) for i in NP]
    tinv = _tri_inverse(a_ab, lb)

    if prompt:
        sst = [s_sc[i] for i in NP]
    else:
        z64 = jnp.zeros((64, 64), F32)

        def pair_blockdiag(b, p):
            top = jnp.concatenate([sin_ref[b, 2 * p], z64], axis=1)
            bot = jnp.concatenate([z64, sin_ref[b, 2 * p + 1]], axis=1)
            return jnp.concatenate([top, bot], axis=0)

        sst = [jnp.concatenate([pair_blockdiag(g * nb + b, p) for b in range(nb)], axis=1) for g, p in probs]
    ss = [_split2(s) for s in sst]
    ar0_a = [_dot3s((expand(a_s[i][0]), expand(a_s[i][1])), ss[i], _NT) for i in NP]
    ar0_r = [lax.dot_general(expand(r_s[i]), ss[i][0], _NT, preferred_element_type=F32) for i in NP]
    us = [_split2(x) for x in _refined_solve(tinv, a_ab, [ar0_a[i] + akv[i] for i in NP])]
    uv = [_cat2([us[i], v_s[i]], 0) for i in NP]
    yst = [ar0_r[i] + jnp.dot(a_r[i], uv[i][0], preferred_element_type=F32) for i in NP]
    ys = [yst[i][0:C] + yst[i][C:R] for i in NP]
    for i in NP:
        g, p = probs[i]
        pl_lanes = tile(e_pos, i)
        plast = jnp.concatenate([pl_lanes[b * Lb + Lb - 1:b * Lb + Lb] for b in range(nb)], axis=1)
        bk_x = tuple(jnp.concatenate([expand(x[0:R]), expand(x[R:2 * R])], axis=0) for x in bk_s[i])
        snew = (sst[i] + _dot3s(uv[i], bk_x, _TN)) * plast
        if prompt:
            s_sc[i] = snew
            sout_ref[g, p] = snew
        else:
            for b in range(nb):
                sout_ref[g * nb + b, 2 * p] = snew[0:64, b * LANES:b * LANES + 64]
                sout_ref[g * nb + b, 2 * p + 1] = snew[64:128, b * LANES + 64:(b + 1) * LANES]

    y = jnp.concatenate([jnp.concatenate(ys[4 * g:4 * g + 4], axis=1) for g in range(G)], axis=0)
    inv_d = 1.0 / 64.0
    mean = _bdot(y, headsum) * inv_d
    yc = y - mean
    var = _bdot(yc * yc, headsum) * inv_d
    yn = yc * lax.rsqrt(var + GN_EPS) * lnw_ref[...] + lnb_ref[...]
    yn = yn + _bdot(r * kmod * rk_ref[...], headsum) * v
    y_ref[...] = (yn * out_gate).astype(y_ref.dtype).reshape(y_ref.shape)


def _wkv_params(p):
    z64 = jnp.zeros((64, RWKV_WIDTH), F32)
    w2 = jnp.concatenate([p["rwkv_w2"], z64], axis=0).astype(BF16)
    a2 = jnp.concatenate([z64, p["rwkv_a2"]], axis=0).astype(BF16)
    row = lambda a: a.reshape(1, -1)
    return [row(p["rwkv_mu"]), row(p["rwkv_w0"]), w2, row(p["rwkv_a0"]), a2, p["rwkv_g2"].astype(BF16),
            row(p["rwkv_k_k"]), row(p["rwkv_k_a"]), row(p["rwkv_r_k"]), row(p["rwkv_ln_w"]), row(p["rwkv_ln_b"])]


def _wkv_prompt(u, p, batch, seq, C=64, G=8):
    nc = seq // C
    params = _wkv_params(p)
    y, sout = pl.pallas_call(
        functools.partial(_wkv_kernel, C=C, Lb=C, prompt=True, G=G),
        grid=(batch // G, nc),
        in_specs=[pl.BlockSpec((G, C, SEG), lambda b, c: (b, c, 1))] + [_const_spec(a) for a in params],
        out_specs=[pl.BlockSpec((G, C, RWKV_WIDTH), lambda b, c: (b, c, 0)),
                   pl.BlockSpec((G, 4, LANES, LANES), lambda b, c: (b, 0, 0, 0))],
        out_shape=[jax.ShapeDtypeStruct((batch, seq, RWKV_WIDTH), BF16),
                   jax.ShapeDtypeStruct((batch, 4, LANES, LANES), F32)],
        scratch_shapes=[pltpu.VMEM((G, 8, SEG), F32), pltpu.VMEM((4 * G, LANES, LANES), F32)],
        compiler_params=_cparams("parallel", "arbitrary"),
        name="wkv_prompt",
    )(u.reshape(batch, seq, 2 * SEG), *params)
    return y.reshape(batch * seq, RWKV_WIDTH), sout


def _wkv_sample(u, hist, sin, p, Lb, C=64, G=2):
    n = u.shape[0]
    nb = C // Lb
    T = G * C
    params = _wkv_params(p)
    sel = _row_selector(G * nb, Lb, (Lb - 1,))
    return pl.pallas_call(
        functools.partial(_wkv_kernel, C=C, Lb=Lb, prompt=False, G=G),
        grid=(n // T,),
        in_specs=[pl.BlockSpec((T, SEG), lambda i: (i, 1)),
                  pl.BlockSpec((G * nb, SEG), lambda i: (i, 0)),
                  pl.BlockSpec((G * nb, 8, 64, 64), lambda i: (i, 0, 0, 0)),
                  _const_spec(sel)]
                 + [_const_spec(a) for a in params],
        out_specs=[pl.BlockSpec((T, RWKV_WIDTH), lambda i: (i, 0)),
                   pl.BlockSpec((G * nb, 8, 64, 64), lambda i: (i, 0, 0, 0)),
                   pl.BlockSpec((G * nb, SEG), lambda i: (i, 0))],
        out_shape=[jax.ShapeDtypeStruct((n, RWKV_WIDTH), BF16),
                   jax.ShapeDtypeStruct(sin.shape, F32),
                   jax.ShapeDtypeStruct((n // Lb, SEG), F32)],
        compiler_params=_cparams("parallel"),
        name="wkv_sample",
    )(u, hist, sin, sel, *params)


def _outproj_kernel(x_ref, ys_ref, yr_ref, wo_ref, g_ref, wq_ref, x1_ref, q_ref):
    wo = wo_ref[...]
    x1 = x_ref[...] + (_bdot(ys_ref[...], wo[0:SSD_WIDTH]) + _bdot(yr_ref[...], wo[SSD_WIDTH:]))
    x1_ref[...] = x1
    q_ref[...] = _bdot(_rms(x1, g_ref[...]), wq_ref[...])


def _outproj(x, y_ssd, y_rw, w_out, g, w_q, tm):
    n, d = x.shape
    return pl.pallas_call(
        _outproj_kernel,
        grid=(n // tm,),
        in_specs=[pl.BlockSpec((tm, d), lambda i: (i, 0)),
                  pl.BlockSpec((tm, SSD_WIDTH), lambda i: (i, 0)),
                  pl.BlockSpec((tm, RWKV_WIDTH), lambda i: (i, 0)),
                  _const_spec(w_out), pl.BlockSpec((1, d), lambda i: (0, 0)), _const_spec(w_q)],
        out_specs=[pl.BlockSpec((tm, d), lambda i: (i, 0)), pl.BlockSpec((tm, d), lambda i: (i, 0))],
        out_shape=[jax.ShapeDtypeStruct((n, d), F32), jax.ShapeDtypeStruct((n, d), F32)],
        compiler_params=_cparams("parallel"),
        name="outproj_q",
    )(x, y_ssd, y_rw, w_out, g.reshape(1, d), w_q)


def _xattn_cache_kernel(q_ref, k_ref, v_ref, o_ref, *, seq_len):
    tq = q_ref.shape[0]
    nmem = tq // seq_len
    rows = N_MEM * XA_HEADS
    cols = XA_HEADS * tq
    q = q_ref[...] * (XA_HEAD_DIM ** -0.5)
    qblk = jnp.concatenate([q[:, h * XA_HEAD_DIM:(h + 1) * XA_HEAD_DIM] for h in range(XA_HEADS)], axis=0)
    key_head = _iota2((cols, rows), 1) & (XA_HEADS - 1)
    qrow = _iota2((cols, 1), 0)
    head_ok = key_head == (qrow >> int(math.log2(tq)))
    out_mem = _iota2((tq, 1), 0) >> int(math.log2(seq_len))
    o = None
    for m in range(nmem):
        kall = k_ref[m].reshape(rows, XA_HEAD_DIM)
        vall = v_ref[m].reshape(rows, XA_HEAD_DIM)
        s = jnp.where(head_ok, _bdot(qblk, kall, _NT), -jnp.inf)
        e = jnp.exp(s - jnp.max(s, axis=-1, keepdims=True))
        pr = e / jnp.sum(e, axis=-1, keepdims=True)
        om = _bdot(pr, vall)
        om = jnp.concatenate([om[h * tq:(h + 1) * tq] for h in range(XA_HEADS)], axis=1)
        o = om if o is None else jnp.where(out_mem == m, om, o)
    o_ref[...] = o.astype(o_ref.dtype)


def _xattn_cache(q, ck, cv, tq, seq_len):
    n, d = q.shape
    nmem = tq // seq_len
    row_spec = pl.BlockSpec((tq, d), lambda i: (i, 0))
    kv_spec = pl.BlockSpec((nmem, N_MEM, XA_HEADS, XA_HEAD_DIM), lambda i: (i, 0, 0, 0))
    return pl.pallas_call(
        functools.partial(_xattn_cache_kernel, seq_len=seq_len),
        grid=(n // tq,),
        in_specs=[row_spec, kv_spec, kv_spec],
        out_specs=row_spec,
        out_shape=jax.ShapeDtypeStruct((n, d), BF16),
        compiler_params=_cparams("parallel"),
        name="xattn_cache",
    )(q, ck, cv)


FF_SUB = 256
FF_SKEW = 1


def _ffn_kernel(*refs, tm, Lb, prompt, blocks_per_seq):
    assert not prompt
    x1_ref, o_ref, wo_ref, hist_ref, g_ref, wu_ref, cw_ref, cb_ref, wd_ref, fg_ref, y_ref, s_ref = refs
    x2 = x1_ref[...] + jnp.dot(o_ref[...], wo_ref[...], preferred_element_type=F32)
    _ffn_body(x2, hist_ref, g_ref, wu_ref, cw_ref, cb_ref, wd_ref, fg_ref, y_ref, s_ref, None,
              tm=tm, Lb=Lb, prompt=False, blocks_per_seq=blocks_per_seq)


def _ffn_body(x, hist_ref, g_ref, wu_ref, cw_ref, cb_ref, wd_ref, fg_ref, y_ref, s_ref, carry_sc,
              *, tm, Lb, prompt, blocks_per_seq):
    if prompt:
        @pl.when(pl.program_id(0) % blocks_per_seq == 0)
        def _():
            carry_sc[...] = jnp.zeros_like(carry_sc)

    hn = _rms(x, g_ref[...]).astype(BF16)
    nsub = D_FF // FF_SUB
    r8 = _iota2((8, 1), 0)
    tpos = _iota2((tm, 1), 0) & (Lb - 1)
    if not prompt:
        trow = _iota2((tm, tm // Lb), 0)
        seq0 = _iota2((tm, tm // Lb), 1) * Lb
        at_row0 = (trow == seq0).astype(BF16)
        at_row1 = (trow == seq0 + 1).astype(BF16)

    def conv(up, cols):
        cw = cw_ref[:, cols]
        cb = cb_ref[:, cols]
        if prompt:
            c8 = carry_sc[:, cols]
            carry_sc[:, cols] = up[tm - 8:tm]
            s_ref[0, :, cols] = up[tm - 2:tm]
            top = up[0:8]
            p1 = jnp.where(r8 >= 1, pltpu.roll(top, 1, 0), c8[7:8])
            p2 = jnp.where(r8 >= 2, pltpu.roll(top, 2, 0), jnp.where(r8 == 0, c8[6:7], c8[7:8]))
            head = (cw[0:1] * p2 + cw[1:2] * p1 + cw[2:3] * top) + cb
            body = (cw[0:1] * pltpu.roll(up, 2, 0) + cw[1:2] * pltpu.roll(up, 1, 0) + cw[2:3] * up) + cb
            return jnp.concatenate([head, body[8:]], axis=0)
        for b in range(tm // Lb):
            s_ref[b, :, cols] = up[b * Lb + Lb - 2:b * Lb + Lb]
        st0, st1 = hist_ref[0, :, cols], hist_ref[1, :, cols]
        h1 = _maskdot(at_row0, st1)
        h2 = _maskdot(at_row0, st0) + _maskdot(at_row1, st1)
        return (cw[0:1] * _shifted(up, 2, tpos, h2) + cw[1:2] * _shifted(up, 1, tpos, h1) + cw[2:3] * up) + cb

    def up_proj(k):
        cg = slice(k * FF_SUB, (k + 1) * FF_SUB)
        cv = slice(D_FF + k * FF_SUB, D_FF + (k + 1) * FF_SUB)
        return (jnp.dot(hn, wu_ref[:, cg], preferred_element_type=F32),
                jnp.dot(hn, wu_ref[:, cv], preferred_element_type=F32))

    def gate_mul(k, ug, uv):
        cg = slice(k * FF_SUB, (k + 1) * FF_SUB)
        cv = slice(D_FF + k * FF_SUB, D_FF + (k + 1) * FF_SUB)
        gate = conv(ug, cg)
        return (gate * _sigmoid(gate) * conv(uv, cv)).astype(BF16)

    ups = {}
    acts = []
    for k in range(nsub + FF_SKEW):
        if k < nsub:
            ups[k] = up_proj(k)
        if k >= FF_SKEW:
            acts.append(gate_mul(k - FF_SKEW, *ups.pop(k - FF_SKEW)))
    acc = x + jnp.dot(jnp.concatenate(acts, axis=1), wd_ref[...], preferred_element_type=F32)
    y_ref[...] = _rms(acc, fg_ref[...])


def _resident_spec(a):
    nd = a.ndim
    return pl.BlockSpec(a.shape, lambda *_: (0,) * nd, pipeline_mode=pl.Buffered(1))


def _ffn_short(x1, o, w_o, hist, p, final_g, tm, Lb):
    n, d = x1.shape
    w_up = p["ffn_w_up"].astype(BF16)
    w_down = p["ffn_w_down"].astype(BF16)
    cw, cb = p["ffn_conv_w"], p["ffn_conv_b"].reshape(1, 2 * D_FF)
    row_spec = pl.BlockSpec((tm, d), lambda i: (i, 0))
    seq_spec = lambda lead: pl.BlockSpec((lead, tm // Lb, 2 * D_FF), lambda i: (0, i, 0))
    consts = [p["norm_ffn_w"].reshape(1, d), w_up, cw, cb, w_down, final_g.reshape(1, d)]
    return pl.pallas_call(
        functools.partial(_ffn_kernel, tm=tm, Lb=Lb, prompt=False, blocks_per_seq=1),
        grid=(n // tm,),
        in_specs=[row_spec, row_spec, _resident_spec(w_o), seq_spec(2)] + [_resident_spec(a) for a in consts],
        out_specs=[row_spec, pl.BlockSpec((tm // Lb, 2, 2 * D_FF), lambda i: (i, 0, 0))],
        out_shape=[jax.ShapeDtypeStruct((n, d), F32), jax.ShapeDtypeStruct((n // Lb, 2, 2 * D_FF), F32)],
        compiler_params=_cparams("arbitrary"),
        name="convffn",
    )(x1, o, w_o, hist, *consts)


def _post_mix_kernel(x_ref, ys_ref, yr_ref, k_ref, v_ref, wo_ref, gxa_ref, wq_ref, wao_ref,
                     g_ref, wu_ref, cw_ref, cb_ref, wd_ref, fg_ref, y_ref, s_ref, carry_sc,
                     *, tm, seq_len):
    wo = wo_ref[...]
    x1 = x_ref[...] + (jnp.dot(ys_ref[...], wo[0:SSD_WIDTH], preferred_element_type=F32)
                       + jnp.dot(yr_ref[...], wo[SSD_WIDTH:], preferred_element_type=F32))
    q = (_bdot(_rms(x1, gxa_ref[...]), wq_ref[...]) * (XA_HEAD_DIM ** -0.5)).astype(BF16)
    outs = []
    for h in range(XA_HEADS):
        sl = slice(h * XA_HEAD_DIM, (h + 1) * XA_HEAD_DIM)
        s = lax.dot_general(q[:, sl], k_ref[0, :, sl].astype(BF16), _NT, preferred_element_type=F32)
        s = s - jnp.max(s, axis=-1, keepdims=True)
        e = jnp.exp(s)
        pr = e / jnp.sum(e, axis=-1, keepdims=True)
        outs.append(_bdot(pr, v_ref[0, :, sl]))
    x2 = x1 + _bdot(jnp.concatenate(outs, axis=1), wao_ref[...])
    _ffn_body(x2, None, g_ref, wu_ref, cw_ref, cb_ref, wd_ref, fg_ref, y_ref, s_ref, carry_sc,
              tm=tm, Lb=seq_len, prompt=True, blocks_per_seq=seq_len // tm)


def _post_mix(x, y_ssd, y_rw, mk, mv, w_out, g_xa, w_q, w_o, p, final_g, tm, seq_len):
    n, d = x.shape
    bps = seq_len // tm
    row = lambda w: pl.BlockSpec((tm, w), lambda i: (i, 0))
    kv_spec = pl.BlockSpec((1, N_MEM, d), lambda i: (i // bps, 0, 0))
    consts = [w_out, g_xa.reshape(1, d), w_q, w_o, p["norm_ffn_w"].reshape(1, d), p["ffn_w_up"].astype(BF16),
              p["ffn_conv_w"], p["ffn_conv_b"].reshape(1, 2 * D_FF), p["ffn_w_down"].astype(BF16),
              final_g.reshape(1, d)]
    y, st = pl.pallas_call(
        functools.partial(_post_mix_kernel, tm=tm, seq_len=seq_len),
        grid=(n // tm,),
        in_specs=[row(d), row(SSD_WIDTH), row(RWKV_WIDTH), kv_spec, kv_spec] + [_resident_spec(a) for a in consts],
        out_specs=[row(d), pl.BlockSpec((1, 2, 2 * D_FF), lambda i: (i, 0, 0))],
        out_shape=[jax.ShapeDtypeStruct((n, d), F32), jax.ShapeDtypeStruct((n // tm, 2, 2 * D_FF), F32)],
        scratch_shapes=[pltpu.VMEM((8, 2 * D_FF), F32)],
        compiler_params=_cparams("arbitrary"),
        name="post_mix",
    )(x, y_ssd, y_rw, mk, mv, *consts)
    return y, st[bps - 1::bps]


def _pair_blocks(sbd):
    b = sbd.shape[0]
    s0 = sbd[:, :, 0:64, 0:64]
    s1 = sbd[:, :, 64:128, 64:128]
    return jnp.stack([s0, s1], axis=2).reshape(b, 8, 64, 64)


def kernel(x_prompt, x_sample, mem_prompt, state_ssm_conv, state_ssm, state_shift, state_wkv, state_ffn_conv, cache_mem_k, cache_mem_v, norm_mix_w, w_in, ssd_conv_w, ssd_conv_b, ssd_dt_bias, ssd_a_log, ssd_d, ssd_norm_w, rwkv_mu, rwkv_w0, rwkv_w2, rwkv_a0, rwkv_a2, rwkv_g2, rwkv_k_k, rwkv_k_a, rwkv_r_k, rwkv_ln_w, rwkv_ln_b, w_out, norm_xa_w, mem_norm_w, xa_w_q, xa_w_k, xa_w_v, xa_w_o, norm_ffn_w, ffn_w_up, ffn_conv_w, ffn_conv_b, ffn_w_down, final_norm_w):
    depth = w_in.shape[0]
    assert depth == 1, "final rmsnorm is fused into the (single) layer's ConvFFN kernel"
    bp, lp, d = x_prompt.shape
    bs, ls, _ = x_sample.shape
    i = 0
    p = dict(ssd_conv_w=ssd_conv_w[i], ssd_conv_b=ssd_conv_b[i], ssd_dt_bias=ssd_dt_bias[i],
             ssd_a_log=ssd_a_log[i], ssd_d=ssd_d[i], ssd_norm_w=ssd_norm_w[i], rwkv_mu=rwkv_mu[i],
             rwkv_w0=rwkv_w0[i], rwkv_w2=rwkv_w2[i], rwkv_a0=rwkv_a0[i], rwkv_a2=rwkv_a2[i],
             rwkv_g2=rwkv_g2[i], rwkv_k_k=rwkv_k_k[i], rwkv_k_a=rwkv_k_a[i],
             rwkv_r_k=rwkv_r_k[i].reshape(-1), rwkv_ln_w=rwkv_ln_w[i], rwkv_ln_b=rwkv_ln_b[i],
             norm_ffn_w=norm_ffn_w[i], ffn_w_up=ffn_w_up[i], ffn_conv_w=ffn_conv_w[i],
             ffn_conv_b=ffn_conv_b[i], ffn_w_down=ffn_w_down[i])

    w_in_t = jnp.swapaxes(w_in[i], 0, 1)
    w_in_p = jnp.concatenate([w_in_t[:SSD_PROJ], jnp.zeros((SEG - SSD_PROJ, d), F32),
                              w_in_t[SSD_PROJ:]], axis=0).astype(BF16)
    w_out_b = w_out[i].astype(BF16)
    w_q_b = (xa_w_q[i]).astype(BF16)
    w_o_b = xa_w_o[i].astype(BF16)
    w_kv_b = jnp.concatenate([xa_w_k[i], xa_w_v[i]], axis=1).astype(BF16)

    xp = x_prompt.reshape(bp * lp, d)
    xs = x_sample.reshape(bs * ls, d)

    mk, mv, mk4, mv4 = _mem_kv(mem_prompt.reshape(bp * N_MEM, d), mem_norm_w[i], w_kv_b, 512)
    mk = mk.reshape(bp, N_MEM, d)
    mv = mv.reshape(bp, N_MEM, d)
    tm_p = min(1024, lp)
    u_p, dt_p, tail_p = _norm_proj(xp, norm_mix_w[i], w_in_p, tm_p, BF16)
    y_ssd_p, ssm_p = _ssd_prompt(u_p, dt_p, p, bp, lp, G=math.gcd(4, bp))
    y_rw_p, wkv_bd_p = _wkv_prompt(u_p, p, bp, lp, G=math.gcd(8, bp))
    y_p, ffn_conv_p = _post_mix(xp, y_ssd_p, y_rw_p, mk, mv, w_out_b, norm_xa_w[i], w_q_b, w_o_b, p,
                                final_norm_w, 512, lp)
    seq_tail = tail_p[lp // tm_p - 1::lp // tm_p]
    ssm_conv_p = seq_tail[:, 5:, SSD_WIDTH:SSD_WIDTH + SSD_XBC]
    shift_p = seq_tail[:, 7, SEG:]

    u_s, dt_s, _ = _norm_proj(xs, norm_mix_w[i], w_in_p, 512, F32)
    assert ls >= 3, "the new conv state is read from the last three rows of each sample sequence"
    y_ssd_s, ssm_s, conv_rows_s = _ssd_sample(u_s, dt_s, jnp.swapaxes(state_ssm_conv[i], 0, 1),
                                              state_ssm[i].reshape(bs, 4, LANES, LANES), p, ls)
    ssm_conv_s = conv_rows_s.reshape(bs, 3, SSD_XBC)
    y_rw_s, wkv_s, shift_s = _wkv_sample(u_s, state_shift[i], state_wkv[i], p, ls)
    x1_s, q_s = _outproj(xs, y_ssd_s, y_rw_s, w_out_b, norm_xa_w[i], w_q_b, 512)
    o_s = _xattn_cache(q_s, cache_mem_k[i], cache_mem_v[i], 32, ls)
    y_s, ffn_conv_s = _ffn_short(x1_s, o_s, w_o_b, jnp.swapaxes(state_ffn_conv[i], 0, 1), p, final_norm_w, 256, ls)

    e = lambda a: a[None]
    return (y_p.reshape(bp, lp, d), y_s.reshape(bs, ls, d),
            e(ssm_conv_p), e(ssm_conv_s),
            e(ssm_p.reshape(bp, SSD_HEADS, 64, SSD_STATE)), e(ssm_s.reshape(bs, SSD_HEADS, 64, SSD_STATE)),
            e(shift_p), e(shift_s),
            e(_pair_blocks(wkv_bd_p)), e(wkv_s),
            e(ffn_conv_p), e(ffn_conv_s),
            e(mk4.reshape(bp, N_MEM, XA_HEADS, XA_HEAD_DIM)), e(mv4.reshape(bp, N_MEM, XA_HEADS, XA_HEAD_DIM)))
```

```python
import functools
import math

import jax
import jax.numpy as jnp
from jax import lax
from jax.experimental import pallas as pl
from jax.experimental.pallas import tpu as pltpu

F32 = jnp.float32
BF16 = jnp.bfloat16

D_MODEL = 1024
N_MEM = 256
XA_HEADS = 4
XA_HEAD_DIM = D_MODEL // XA_HEADS
SSD_WIDTH = 512
SSD_HEADS = 8
SSD_STATE = 128
SSD_XBC = 1024
SSD_PROJ = SSD_WIDTH + SSD_XBC + SSD_HEADS
RWKV_WIDTH = 512
RWKV_PROJ = 1792
D_FF = 2816
EPS = 1e-6
GN_EPS = 64e-5

LANES = 128
SEG = 1792
DT_OFF = SSD_WIDTH + SSD_XBC
VMEM_LIMIT_BYTES = 56 * 1024 * 1024


def _cparams(*sem):
    return pltpu.CompilerParams(dimension_semantics=sem, vmem_limit_bytes=VMEM_LIMIT_BYTES)


_NN = (((1,), (0,)), ((), ()))
_NT = (((1,), (1,)), ((), ()))
_TN = (((0,), (0,)), ((), ()))


def _bdot(a, b, dims=_NN):
    return lax.dot_general(a.astype(BF16), b.astype(BF16), dims, preferred_element_type=F32)


def _split2(x):
    hi = x.astype(BF16)
    lo = (x - hi.astype(F32)).astype(BF16)
    return hi, lo


def _maskdot(m_bf16, x):
    hi = x.astype(BF16)
    r1 = x - hi.astype(F32)
    mid = r1.astype(BF16)
    lo = (r1 - mid.astype(F32)).astype(BF16)
    return jnp.dot(jnp.concatenate([m_bf16, m_bf16, m_bf16], axis=1),
                   jnp.concatenate([hi, mid, lo], axis=0), preferred_element_type=F32)


def _sigmoid(x):
    return 0.5 + 0.5 * jnp.tanh(0.5 * x)


def _softplus(x):
    return jnp.maximum(x, 0.0) + jnp.log1p(jnp.exp(-jnp.abs(x)))


def _rms(x, g):
    return x * lax.rsqrt(jnp.mean(x * x, axis=-1, keepdims=True) + EPS) * g


def _iota2(shape, dim):
    return lax.broadcasted_iota(jnp.int32, shape, dim)


def _place_rows(rows, nseq, Lb, t, state_rows):
    r = _iota2((rows, nseq), 0)
    first = _iota2((rows, nseq), 1) * Lb
    return _maskdot((r == first + t).astype(BF16), state_rows)


def _shifted(x, k, tpos, hist):
    return jnp.where(tpos >= k, pltpu.roll(x, k, 0), hist)


def _norm_proj_kernel(x_ref, g_ref, wt_ref, o_ref, dt_ref, tail_ref, hn_sc):
    j = pl.program_id(1)

    @pl.when(j == 0)
    def _():
        hn_sc[...] = _rms(x_ref[...], g_ref[...]).astype(BF16)

    acc = lax.dot_general(hn_sc[...], wt_ref[...], _NT, preferred_element_type=F32)
    o_ref[...] = acc.astype(o_ref.dtype)
    tail_ref[0] = acc[acc.shape[0] - 8:]

    @pl.when(j == 0)
    def _():
        dt_ref[...] = acc[:, DT_OFF:DT_OFF + LANES]


def _norm_proj(x, g, wt_bf16, tm, out_dtype):
    n, d = x.shape
    return pl.pallas_call(
        _norm_proj_kernel,
        grid=(n // tm, 2),
        in_specs=[pl.BlockSpec((tm, d), lambda i, j: (i, 0)),
                  pl.BlockSpec((1, d), lambda i, j: (0, 0)),
                  pl.BlockSpec((SEG, d), lambda i, j: (j, 0))],
        out_specs=[pl.BlockSpec((tm, SEG), lambda i, j: (i, j)),
                   pl.BlockSpec((tm, LANES), lambda i, j: (i, 0)),
                   pl.BlockSpec((1, 8, SEG), lambda i, j: (i, 0, j))],
        out_shape=[jax.ShapeDtypeStruct((n, 2 * SEG), out_dtype),
                   jax.ShapeDtypeStruct((n, LANES), F32),
                   jax.ShapeDtypeStruct((n // tm, 8, 2 * SEG), F32)],
        scratch_shapes=[pltpu.VMEM((tm, d), BF16)],
        compiler_params=_cparams("parallel", "arbitrary"),
        name="norm_proj",
    )(x, g.reshape(1, d), wt_bf16)


def _mem_kv_kernel(x_ref, g_ref, w_ref, k_ref, v_ref, k4_ref, v4_ref):
    hn = _rms(x_ref[...], g_ref[...]).astype(BF16)
    d = k_ref.shape[1]
    k = jnp.dot(hn, w_ref[:, 0:d], preferred_element_type=F32)
    v = jnp.dot(hn, w_ref[:, d:2 * d], preferred_element_type=F32)
    k_ref[...] = k
    v_ref[...] = v
    for h in range(XA_HEADS):
        sl = slice(h * XA_HEAD_DIM, (h + 1) * XA_HEAD_DIM)
        k4_ref[:, h, :] = k[:, sl]
        v4_ref[:, h, :] = v[:, sl]


def _mem_kv(x, g, w_kv_bf16, tm):
    n, d = x.shape
    row_spec = pl.BlockSpec((tm, d), lambda i: (i, 0))
    head_spec = pl.BlockSpec((tm, XA_HEADS, XA_HEAD_DIM), lambda i: (i, 0, 0))
    return pl.pallas_call(
        _mem_kv_kernel,
        grid=(n // tm,),
        in_specs=[row_spec, pl.BlockSpec((1, d), lambda i: (0, 0)), _const_spec(w_kv_bf16)],
        out_specs=[row_spec, row_spec, head_spec, head_spec],
        out_shape=[jax.ShapeDtypeStruct((n, d), F32)] * 2
                  + [jax.ShapeDtypeStruct((n, XA_HEADS, XA_HEAD_DIM), F32)] * 2,
        compiler_params=_cparams("parallel"),
        name="mem_kv",
    )(x, g.reshape(1, d), w_kv_bf16)


def _ssd_kernel(*refs, Q, Lb, prompt, G=1):
    if not prompt:
        return _ssd_block(*refs, Q=Q, Lb=Lb, prompt=False)
    (u_ref, dt_ref, cw_ref, cb_ref, dtb_ref, an_ref, dsk_ref, nw_ref, y_ref, hout_ref, carry_sc, h_sc) = refs

    @pl.when(pl.program_id(1) == 0)
    def _():
        carry_sc[...] = jnp.zeros_like(carry_sc)
        h_sc[...] = jnp.zeros_like(h_sc)

    for gi in range(G):
        _ssd_block(u_ref.at[gi], dt_ref.at[gi], cw_ref, cb_ref, dtb_ref, an_ref, dsk_ref, nw_ref,
                   y_ref.at[gi], hout_ref.at[pl.ds(gi, 1)], carry_sc.at[gi], h_sc.at[pl.ds(4 * gi, 4)],
                   Q=Q, Lb=Lb, prompt=True)


def _ssd_block(*refs, Q, Lb, prompt):
    ns = Q // Lb
    lb = int(math.log2(Lb))
    rows = _iota2((Q, 1), 0)
    tpos = rows & (Lb - 1)
    if prompt:
        (u_ref, dt_ref, cw_ref, cb_ref, dtb_ref, an_ref, dsk_ref, nw_ref,
         y_ref, hout_ref, carry_sc, h_sc) = refs
    else:
        (u_ref, dt_ref, hist_ref, hin_ref, sel_ref, cw_ref, cb_ref, dtb_ref, an_ref, dsk_ref, nw_ref,
         y_ref, hout_ref, cs_ref) = refs
    u = u_ref[...].astype(F32)
    z = u[:, 0:SSD_WIDTH]
    z_gate = z * _sigmoid(z)
    x = u[:, SSD_WIDTH:SSD_WIDTH + SSD_XBC]
    if prompt:
        c8 = carry_sc[...]
        h1 = c8[7:8]
        h2 = jnp.where(rows == 0, c8[6:7], c8[7:8])
        h3 = jnp.where(rows == 0, c8[5:6], jnp.where(rows == 1, c8[6:7], c8[7:8]))
        carry_sc[...] = x[Q - 8:Q]
    else:
        cs_ref[...] = _maskdot(sel_ref[...], x)
        st = [hist_ref[j] for j in range(3)]
        put = functools.partial(_place_rows, Q, ns, Lb)
        h1 = put(0, st[2])
        h2 = put(0, st[1]) + put(1, st[2])
        h3 = put(0, st[0]) + put(1, st[1]) + put(2, st[2])
    cw = cw_ref[...]
    xc = (cw[0:1] * _shifted(x, 3, tpos, h3) + cw[1:2] * _shifted(x, 2, tpos, h2)
          + cw[2:3] * _shifted(x, 1, tpos, h1) + cw[3:4] * x) + cb_ref[...]
    xc = xc * _sigmoid(xc)
    dtr = dt_ref[...]
    xs = xc[:, 0:SSD_WIDTH]
    bm = xc[:, SSD_WIDTH:SSD_WIDTH + 2 * SSD_STATE]
    cm = xc[:, SSD_WIDTH + 2 * SSD_STATE:]

    dt = _softplus(dtr + dtb_ref[...])
    da = dt * an_ref[...]

    ri = _iota2((Q, Q), 0)
    ci = _iota2((Q, Q), 1)
    same = (ri >> lb) == (ci >> lb)
    tril = same & (ci <= ri)
    sel = ci == (((ri >> lb) << lb) + (Lb - 1))
    acs = _maskdot(tril.astype(BF16), da)
    acs_t = acs.T
    acs_last = _maskdot(sel.astype(BF16), acs)
    dec_end = jnp.exp(acs_last - acs)
    eacs = jnp.exp(acs)
    seqid = rows >> lb

    lane = _iota2((Q, LANES), 1)
    lo_half = lane < 64
    prow = _iota2((LANES, LANES), 0)
    dsk = dsk_ref[...]

    ys = []
    for q in range(4):
        g = q // 2
        h0, h1i = 2 * q, 2 * q + 1
        if q % 2 == 0:
            cg = cm[:, g * SSD_STATE:(g + 1) * SSD_STATE]
            bg = bm[:, g * SSD_STATE:(g + 1) * SSD_STATE]
            cb_g = _bdot(cg, bg, _NT)
            if ns == 1:
                cexp, bexp = cg.astype(BF16), bg.astype(BF16)
            else:
                cexp = jnp.concatenate([jnp.where(seqid == b, cg, 0.0).astype(BF16) for b in range(ns)], axis=1)
                bexp = jnp.concatenate([jnp.where(seqid == b, bg, 0.0).astype(BF16) for b in range(ns)], axis=1)
        m0 = jnp.where(tril, cb_g * jnp.exp(acs[:, h0:h0 + 1] - acs_t[h0:h0 + 1, :]), 0.0)
        m1 = jnp.where(tril, cb_g * jnp.exp(acs[:, h1i:h1i + 1] - acs_t[h1i:h1i + 1, :]), 0.0)
        xp = xs[:, q * LANES:(q + 1) * LANES]
        xdt = xp * jnp.where(lo_half, dt[:, h0:h0 + 1], dt[:, h1i:h1i + 1])
        xdt0 = jnp.where(lo_half, xdt, 0.0)
        xdt1 = jnp.where(lo_half, 0.0, xdt)
        ydiag = _bdot(jnp.concatenate([m0, m1], axis=1), jnp.concatenate([xdt0, xdt1], axis=0))
        if prompt:
            hst = h_sc[q]
        else:
            hst = jnp.concatenate([hin_ref[b, q] for b in range(ns)], axis=1)
        ecs = jnp.where(lo_half, eacs[:, h0:h0 + 1], eacs[:, h1i:h1i + 1])
        yoff = _bdot(cexp, hst, _NT) * ecs
        xd = xdt * jnp.where(lo_half, dec_end[:, h0:h0 + 1], dec_end[:, h1i:h1i + 1])
        incr = _bdot(xd, bexp, _TN)
        scales = []
        for b in range(ns):
            r = b * Lb + Lb - 1
            e = eacs[r:r + 1, :]
            scales.append(jnp.where(prow < 64, e[:, h0:h0 + 1], e[:, h1i:h1i + 1]))
        scale = scales[0] if ns == 1 else jnp.concatenate(scales, axis=1)
        hnew = hst * scale + incr
        if prompt:
            h_sc[q] = hnew
            hout_ref[0, q] = hnew
        else:
            for b in range(ns):
                hout_ref[b, q] = hnew[:, b * LANES:(b + 1) * LANES]
        ys.append(ydiag + yoff + dsk[:, q * LANES:(q + 1) * LANES] * xp)

    y = jnp.concatenate(ys, axis=1)
    y = y * z_gate
    half = SSD_WIDTH // 2
    outs = []
    for g in range(2):
        yg = y[:, g * half:(g + 1) * half]
        outs.append(yg * lax.rsqrt(jnp.mean(yg * yg, axis=-1, keepdims=True) + EPS))
    y_ref[...] = (jnp.concatenate(outs, axis=1) * nw_ref[...]).astype(y_ref.dtype)


def _ssd_params(p):
    an = jnp.zeros((1, LANES), F32).at[0, :SSD_HEADS].set(-jnp.exp(p["ssd_a_log"]))
    dtb = jnp.zeros((1, LANES), F32).at[0, :SSD_HEADS].set(p["ssd_dt_bias"])
    dsk = jnp.repeat(p["ssd_d"], SSD_WIDTH // SSD_HEADS).reshape(1, SSD_WIDTH)
    return [p["ssd_conv_w"], p["ssd_conv_b"].reshape(1, SSD_XBC), dtb, an, dsk,
            p["ssd_norm_w"].reshape(1, SSD_WIDTH)]


def _const_spec(a):
    nd = a.ndim
    return pl.BlockSpec(a.shape, lambda *_: (0,) * nd)


def _ssd_prompt(u, dt, p, batch, seq, Q=128, G=4):
    nc = seq // Q
    params = _ssd_params(p)
    y, hout = pl.pallas_call(
        functools.partial(_ssd_kernel, Q=Q, Lb=Q, prompt=True, G=G),
        grid=(batch // G, nc),
        in_specs=[pl.BlockSpec((G, Q, SEG), lambda b, c: (b, c, 0)),
                  pl.BlockSpec((G, Q, LANES), lambda b, c: (b, c, 0))] + [_const_spec(a) for a in params],
        out_specs=[pl.BlockSpec((G, Q, SSD_WIDTH), lambda b, c: (b, c, 0)),
                   pl.BlockSpec((G, 4, LANES, LANES), lambda b, c: (b, 0, 0, 0))],
        out_shape=[jax.ShapeDtypeStruct((batch, seq, SSD_WIDTH), BF16),
                   jax.ShapeDtypeStruct((batch, 4, LANES, LANES), F32)],
        scratch_shapes=[pltpu.VMEM((G, 8, SSD_XBC), F32), pltpu.VMEM((4 * G, LANES, LANES), F32)],
        compiler_params=_cparams("parallel", "arbitrary"),
        name="ssd_prompt",
    )(u.reshape(batch, seq, 2 * SEG), dt.reshape(batch, seq, LANES), *params)
    return y.reshape(batch * seq, SSD_WIDTH), hout


def _row_selector(nseq, Lb, offsets):
    k = len(offsets)
    rows = jnp.arange(nseq * k)
    target = (rows // k) * Lb + jnp.asarray(offsets)[rows % k]
    return (target[:, None] == jnp.arange(nseq * Lb)[None, :]).astype(BF16)


def _ssd_sample(u, dt, hist, hin, p, Lb, Q=64):
    n = u.shape[0]
    ns = Q // Lb
    params = _ssd_params(p)
    sel = _row_selector(ns, Lb, (Lb - 3, Lb - 2, Lb - 1))
    return pl.pallas_call(
        functools.partial(_ssd_kernel, Q=Q, Lb=Lb, prompt=False),
        grid=(n // Q,),
        in_specs=[pl.BlockSpec((Q, SEG), lambda i: (i, 0)),
                  pl.BlockSpec((Q, LANES), lambda i: (i, 0)),
                  pl.BlockSpec((3, ns, SSD_XBC), lambda i: (0, i, 0)),
                  pl.BlockSpec((ns, 4, LANES, LANES), lambda i: (i, 0, 0, 0)),
                  _const_spec(sel)]
                 + [_const_spec(a) for a in params],
        out_specs=[pl.BlockSpec((Q, SSD_WIDTH), lambda i: (i, 0)),
                   pl.BlockSpec((ns, 4, LANES, LANES), lambda i: (i, 0, 0, 0)),
                   pl.BlockSpec((ns * 3, SSD_XBC), lambda i: (i, 0))],
        out_shape=[jax.ShapeDtypeStruct((n, SSD_WIDTH), BF16),
                   jax.ShapeDtypeStruct(hin.shape, F32),
                   jax.ShapeDtypeStruct((n // Lb * 3, SSD_XBC), F32)],
        compiler_params=_cparams("parallel"),
        name="ssd_sample",
    )(u, dt, hist, hin, sel, *params)


def _dot3s(a, b, dims=_NN):
    ka = dims[0][0][0]
    kb = dims[0][1][0]
    lhs = jnp.concatenate([a[0], a[0], a[1]], axis=ka)
    rhs = jnp.concatenate([b[0], b[1], b[0]], axis=kb)
    return lax.dot_general(lhs, rhs, dims, preferred_element_type=F32)


def _cat2(parts, axis):
    return (jnp.concatenate([p[0] for p in parts], axis=axis), jnp.concatenate([p[1] for p in parts], axis=axis))


def _tri_inverse(mats, lb):
    n = mats[0].shape[0]
    ri = _iota2((n, n), 0)
    ci = _iota2((n, n), 1)
    off1 = ((ri >> 1) == (ci >> 1)) & ((ri & 1) == 1) & ((ci & 1) == 0)
    eye = jnp.where(ri == ci, 1.0, 0.0)
    ts = [eye + jnp.where(off1, a, 0.0) for a in mats]
    for lvl in range(1, lb):
        m = 1 << lvl
        off = ((ri >> (lvl + 1)) == (ci >> (lvl + 1))) & ((ri & (2 * m - 1)) >= m) & ((ci & (2 * m - 1)) < m)
        tsb = [t.astype(BF16) for t in ts]
        ws = [jnp.dot(jnp.where(off, a, 0.0).astype(BF16), tb, preferred_element_type=F32)
              for a, tb in zip(mats, tsb)]
        ts = [t + jnp.dot(tb, w.astype(BF16), preferred_element_type=F32) for t, tb, w in zip(ts, tsb, ws)]
    return [t.astype(BF16) for t in ts]


def _refined_solve(tinvs, mats, rhss):
    d = functools.partial(jnp.dot, preferred_element_type=F32)
    n = range(len(mats))
    rs = [_split2(r) for r in rhss]
    u0 = [d(jnp.concatenate([tinvs[i], tinvs[i]], axis=1), jnp.concatenate(rs[i], axis=0)) for i in n]
    au = [_dot3s(_split2(mats[i]), _split2(u0[i])) for i in n]
    res = [((rhss[i] - u0[i]) + au[i]).astype(BF16) for i in n]
    return [u0[i] + d(tinvs[i], res[i]) for i in n]


def _wkv_kernel(*refs, C, Lb, prompt, G):
    nb = C // Lb
    lb = int(math.log2(Lb))
    R = 2 * C
    T = G * C
    if prompt:
        (u_ref, mu_ref, w0_ref, w2_ref, a0_ref, a2_ref, g2_ref, kk_ref, ka_ref, rk_ref, lnw_ref, lnb_ref,
         y_ref, sout_ref, carry_sc, s_sc) = refs
        c = pl.program_id(1)

        @pl.when(c == 0)
        def _():
            carry_sc[...] = jnp.zeros_like(carry_sc)
            s_sc[...] = jnp.zeros_like(s_sc)
    else:
        (u_ref, hist_ref, sin_ref, sel_ref, mu_ref, w0_ref, w2_ref, a0_ref, a2_ref, g2_ref, kk_ref, ka_ref,
         rk_ref, lnw_ref, lnb_ref, y_ref, sout_ref, shift_ref) = refs

    rows = _iota2((T, 1), 0)
    tpos = rows & (Lb - 1)
    if prompt:
        u = u_ref[...].astype(F32).reshape(T, SEG)
        hist = jnp.concatenate([jnp.broadcast_to(carry_sc[g, 7:8], (C, SEG)) for g in range(G)], axis=0)
        for g in range(G):
            carry_sc[g] = u[g * C + C - 8:(g + 1) * C]
    else:
        u = u_ref[...].astype(F32)
        hist = _place_rows(T, G * nb, Lb, 0, hist_ref[...])
        shift_ref[...] = _maskdot(sel_ref[...], u)
    um = u + (_shifted(u, 1, tpos, hist) - u) * mu_ref[...]

    W = RWKV_WIDTH
    r = um[:, 0:W]
    k = um[:, W:2 * W]
    v = um[:, 2 * W:3 * W]
    t12 = um[:, 3 * W:3 * W + LANES]
    lg = um[:, 3 * W + LANES:3 * W + 2 * LANES]

    wl = w0_ref[...] + _bdot(jnp.tanh(t12), w2_ref[...])
    logw = -math.exp(-0.5) * _sigmoid(wl)
    a = _sigmoid(a0_ref[...] + _bdot(t12, a2_ref[...]))
    out_gate = _bdot(_sigmoid(lg), g2_ref[...])

    hi = _iota2((W, W), 0)
    hj = _iota2((W, W), 1)
    headsum = ((hi >> 6) == (hj >> 6)).astype(BF16)

    kk = k * kk_ref[...]
    kk = kk * lax.rsqrt(jnp.maximum(_bdot(kk * kk, headsum), 1e-24))
    kmod = k * (1.0 + (a - 1.0) * ka_ref[...])
    beta = kk * a

    ri = _iota2((T, T), 0)
    ci = _iota2((T, T), 1)
    same_c = (ri >> lb) == (ci >> lb)
    lc = _maskdot((same_c & (ci <= ri)).astype(BF16), logw)
    e_pos = jnp.exp(lc)
    e_neg = jnp.exp(-lc)
    at = -kk * jnp.exp(lc - logw)
    rt = r * e_pos
    bt = beta * e_neg
    kt = kmod * e_neg

    si = _iota2((R, R), 0)
    sj = _iota2((R, R), 1)
    same_s = (si >> lb) == (sj >> lb)
    strict = same_s & (sj < si)
    incl = same_s & (sj <= si)
    incl2 = jnp.concatenate([incl, incl], axis=1)
    lane_row = _iota2((1, LANES), 1)
    m_lo = jnp.where(lane_row < 64, 1.0, 0.0).astype(BF16)
    m_hi = jnp.where(lane_row < 64, 0.0, 1.0).astype(BF16)
    if nb > 1:
        seq_s = (_iota2((R, LANES), 0) & (C - 1)) >> lb
        seq_masks = [jnp.where(seq_s == b, 1.0, 0.0).astype(BF16) for b in range(nb)]

    def stack(xp):
        return jnp.concatenate([xp * m_lo, xp * m_hi], axis=0)

    def expand(xs_):
        if nb == 1:
            return xs_
        return jnp.concatenate([xs_ * seq_masks[b] for b in range(nb)], axis=1)

    probs = [(g, p) for g in range(G) for p in range(4)]
    NP = range(len(probs))

    def tile(x, i):
        g, p = probs[i]
        return x[g * C:(g + 1) * C, p * LANES:(p + 1) * LANES]

    def stacked(pair, i):
        return (stack(tile(pair[0], i)), stack(tile(pair[1], i)))

    at2, v2, bt2, kt2 = (_split2(x) for x in (at, v, bt, kt))
    rt_b = rt.astype(BF16)
    a_s = [stacked(at2, i) for i in NP]
    r_s = [stack(tile(rt_b, i)) for i in NP]
    v_s = [stacked(v2, i) for i in NP]
    bk_s = [_cat2([stacked(bt2, i), stacked(kt2, i)], 0) for i in NP]
    gm_a = [_dot3s(a_s[i], bk_s[i], _NT) for i in NP]
    gm_r = [lax.dot_general(r_s[i], bk_s[i][0], _NT, preferred_element_type=F32) for i in NP]
    a_ab = [jnp.where(strict, gm_a[i][:, 0:R], 0.0) for i in NP]
    a_ak = [_split2(jnp.where(strict, gm_a[i][:, R:2 * R], 0.0)) for i in NP]
    a_r = [jnp.where(incl2, gm_r[i], 0.0).astype(BF16) for i in NP]
    akv = [_dot3s(a_ak[i], v_s[i]) for i in NP]
    tinv = _tri_inverse(a_ab, lb)

    if prompt:
        sst = [s_sc[i] for i in NP]
    else:
        z64 = jnp.zeros((64, 64), F32)

        def pair_blockdiag(b, p):
            top = jnp.concatenate([sin_ref[b, 2 * p], z64], axis=1)
            bot = jnp.concatenate([z64, sin_ref[b, 2 * p + 1]], axis=1)
            return jnp.concatenate([top, bot], axis=0)

        sst = [jnp.concatenate([pair_blockdiag(g * nb + b, p) for b in range(nb)], axis=1) for g, p in probs]
    ss = [_split2(s) for s in sst]
    ar0_a = [_dot3s((expand(a_s[i][0]), expand(a_s[i][1])), ss[i], _NT) for i in NP]
    ar0_r = [lax.dot_general(expand(r_s[i]), ss[i][0], _NT, preferred_element_type=F32) for i in NP]
    us = [_split2(x) for x in _refined_solve(tinv, a_ab, [ar0_a[i] + akv[i] for i in NP])]
    uv = [_cat2([us[i], v_s[i]], 0) for i in NP]
    yst = [ar0_r[i] + jnp.dot(a_r[i], uv[i][0], preferred_element_type=F32) for i in NP]
    ys = [yst[i][0:C] + yst[i][C:R] for i in NP]
    for i in NP:
        g, p = probs[i]
        pl_lanes = tile(e_pos, i)
        plast = jnp.concatenate([pl_lanes[b * Lb + Lb - 1:b * Lb + Lb] for b in range(nb)], axis=1)
        bk_x = tuple(jnp.concatenate([expand(x[0:R]), expand(x[R:2 * R])], axis=0) for x in bk_s[i])
        snew = (sst[i] + _dot3s(uv[i], bk_x, _TN)) * plast
        if prompt:
            s_sc[i] = snew
            sout_ref[g, p] = snew
        else:
            for b in range(nb):
                sout_ref[g * nb + b, 2 * p] = snew[0:64, b * LANES:b * LANES + 64]
                sout_ref[g * nb + b, 2 * p + 1] = snew[64:128, b * LANES + 64:(b + 1) * LANES]

    y = jnp.concatenate([jnp.concatenate(ys[4 * g:4 * g + 4], axis=1) for g in range(G)], axis=0)
    inv_d = 1.0 / 64.0
    mean = _bdot(y, headsum) * inv_d
    yc = y - mean
    var = _bdot(yc * yc, headsum) * inv_d
    yn = yc * lax.rsqrt(var + GN_EPS) * lnw_ref[...] + lnb_ref[...]
    yn = yn + _bdot(r * kmod * rk_ref[...], headsum) * v
    y_ref[...] = (yn * out_gate).astype(y_ref.dtype).reshape(y_ref.shape)


def _wkv_params(p):
    z64 = jnp.zeros((64, RWKV_WIDTH), F32)
    w2 = jnp.concatenate([p["rwkv_w2"], z64], axis=0).astype(BF16)
    a2 = jnp.concatenate([z64, p["rwkv_a2"]], axis=0).astype(BF16)
    row = lambda a: a.reshape(1, -1)
    return [row(p["rwkv_mu"]), row(p["rwkv_w0"]), w2, row(p["rwkv_a0"]), a2, p["rwkv_g2"].astype(BF16),
            row(p["rwkv_k_k"]), row(p["rwkv_k_a"]), row(p["rwkv_r_k"]), row(p["rwkv_ln_w"]), row(p["rwkv_ln_b"])]


def _wkv_prompt(u, p, batch, seq, C=64, G=8):
    nc = seq // C
    params = _wkv_params(p)
    y, sout = pl.pallas_call(
        functools.partial(_wkv_kernel, C=C, Lb=C, prompt=True, G=G),
        grid=(batch // G, nc),
        in_specs=[pl.BlockSpec((G, C, SEG), lambda b, c: (b, c, 1))] + [_const_spec(a) for a in params],
        out_specs=[pl.BlockSpec((G, C, RWKV_WIDTH), lambda b, c: (b, c, 0)),
                   pl.BlockSpec((G, 4, LANES, LANES), lambda b, c: (b, 0, 0, 0))],
        out_shape=[jax.ShapeDtypeStruct((batch, seq, RWKV_WIDTH), BF16),
                   jax.ShapeDtypeStruct((batch, 4, LANES, LANES), F32)],
        scratch_shapes=[pltpu.VMEM((G, 8, SEG), F32), pltpu.VMEM((4 * G, LANES, LANES), F32)],
        compiler_params=_cparams("parallel", "arbitrary"),
        name="wkv_prompt",
    )(u.reshape(batch, seq, 2 * SEG), *params)
    return y.reshape(batch * seq, RWKV_WIDTH), sout


def _wkv_sample(u, hist, sin, p, Lb, C=64, G=2):
    n = u.shape[0]
    nb = C // Lb
    T = G * C
    params = _wkv_params(p)
    sel = _row_selector(G * nb, Lb, (Lb - 1,))
    return pl.pallas_call(
        functools.partial(_wkv_kernel, C=C, Lb=Lb, prompt=False, G=G),
        grid=(n // T,),
        in_specs=[pl.BlockSpec((T, SEG), lambda i: (i, 1)),
                  pl.BlockSpec((G * nb, SEG), lambda i: (i, 0)),
                  pl.BlockSpec((G * nb, 8, 64, 64), lambda i: (i, 0, 0, 0)),
                  _const_spec(sel)]
                 + [_const_spec(a) for a in params],
        out_specs=[pl.BlockSpec((T, RWKV_WIDTH), lambda i: (i, 0)),
                   pl.BlockSpec((G * nb, 8, 64, 64), lambda i: (i, 0, 0, 0)),
                   pl.BlockSpec((G * nb, SEG), lambda i: (i, 0))],
        out_shape=[jax.ShapeDtypeStruct((n, RWKV_WIDTH), BF16),
                   jax.ShapeDtypeStruct(sin.shape, F32),
                   jax.ShapeDtypeStruct((n // Lb, SEG), F32)],
        compiler_params=_cparams("parallel"),
        name="wkv_sample",
    )(u, hist, sin, sel, *params)


def _outproj_kernel(x_ref, ys_ref, yr_ref, wo_ref, g_ref, wq_ref, x1_ref, q_ref):
    wo = wo_ref[...]
    x1 = x_ref[...] + (_bdot(ys_ref[...], wo[0:SSD_WIDTH]) + _bdot(yr_ref[...], wo[SSD_WIDTH:]))
    x1_ref[...] = x1
    q_ref[...] = _bdot(_rms(x1, g_ref[...]), wq_ref[...])


def _outproj(x, y_ssd, y_rw, w_out, g, w_q, tm):
    n, d = x.shape
    return pl.pallas_call(
        _outproj_kernel,
        grid=(n // tm,),
        in_specs=[pl.BlockSpec((tm, d), lambda i: (i, 0)),
                  pl.BlockSpec((tm, SSD_WIDTH), lambda i: (i, 0)),
                  pl.BlockSpec((tm, RWKV_WIDTH), lambda i: (i, 0)),
                  _const_spec(w_out), pl.BlockSpec((1, d), lambda i: (0, 0)), _const_spec(w_q)],
        out_specs=[pl.BlockSpec((tm, d), lambda i: (i, 0)), pl.BlockSpec((tm, d), lambda i: (i, 0))],
        out_shape=[jax.ShapeDtypeStruct((n, d), F32), jax.ShapeDtypeStruct((n, d), F32)],
        compiler_params=_cparams("parallel"),
        name="outproj_q",
    )(x, y_ssd, y_rw, w_out, g.reshape(1, d), w_q)


CACHE_BUFFERS = 3


def _xattn_cache_kernel(q_ref, k_hbm, v_hbm, o_ref, kbuf, vbuf, sem, *, seq_len):
    tq = q_ref.shape[0]
    nmem = tq // seq_len
    step = pl.program_id(0)
    nsteps = pl.num_programs(0)

    def fetch(s, slot):
        src = pl.ds(s * nmem, nmem)
        return (pltpu.make_async_copy(k_hbm.at[src], kbuf.at[slot], sem.at[0, slot]),
                pltpu.make_async_copy(v_hbm.at[src], vbuf.at[slot], sem.at[1, slot]))

    @pl.when(step == 0)
    def _():
        for s in range(CACHE_BUFFERS - 1):
            for c in fetch(s, s):
                c.start()

    ahead = step + (CACHE_BUFFERS - 1)

    @pl.when(ahead < nsteps)
    def _():
        for c in fetch(ahead, ahead % CACHE_BUFFERS):
            c.start()

    slot = step % CACHE_BUFFERS
    for c in fetch(step, slot):
        c.wait()
    k_ref = kbuf.at[slot]
    v_ref = vbuf.at[slot]
    rows = N_MEM * XA_HEADS
    cols = XA_HEADS * tq
    q = q_ref[...] * (XA_HEAD_DIM ** -0.5)
    qblk = jnp.concatenate([q[:, h * XA_HEAD_DIM:(h + 1) * XA_HEAD_DIM] for h in range(XA_HEADS)], axis=0)
    key_head = _iota2((cols, rows), 1) & (XA_HEADS - 1)
    qrow = _iota2((cols, 1), 0)
    head_ok = key_head == (qrow >> int(math.log2(tq)))
    out_mem = _iota2((tq, 1), 0) >> int(math.log2(seq_len))
    o = None
    for m in range(nmem):
        kall = k_ref[m].reshape(rows, XA_HEAD_DIM)
        vall = v_ref[m].reshape(rows, XA_HEAD_DIM)
        s = jnp.where(head_ok, _bdot(qblk, kall, _NT), -jnp.inf)
        e = jnp.exp(s - jnp.max(s, axis=-1, keepdims=True))
        pr = e / jnp.sum(e, axis=-1, keepdims=True)
        om = _bdot(pr, vall)
        om = jnp.concatenate([om[h * tq:(h + 1) * tq] for h in range(XA_HEADS)], axis=1)
        o = om if o is None else jnp.where(out_mem == m, om, o)
    o_ref[...] = o.astype(o_ref.dtype)


def _xattn_cache(q, ck, cv, tq, seq_len):
    n, d = q.shape
    nmem = tq // seq_len
    assert n // tq >= CACHE_BUFFERS - 1
    row_spec = pl.BlockSpec((tq, d), lambda i: (i, 0))
    hbm_spec = pl.BlockSpec(memory_space=pl.ANY)
    slot_shape = (CACHE_BUFFERS, nmem, N_MEM, XA_HEADS, XA_HEAD_DIM)
    return pl.pallas_call(
        functools.partial(_xattn_cache_kernel, seq_len=seq_len),
        grid=(n // tq,),
        in_specs=[row_spec, hbm_spec, hbm_spec],
        out_specs=row_spec,
        out_shape=jax.ShapeDtypeStruct((n, d), BF16),
        scratch_shapes=[pltpu.VMEM(slot_shape, F32), pltpu.VMEM(slot_shape, F32),
                        pltpu.SemaphoreType.DMA((2, CACHE_BUFFERS))],
        compiler_params=_cparams("arbitrary"),
        name="xattn_cache",
    )(q, ck, cv)


FF_SUB = 256
FF_SKEW = 1


def _ffn_kernel(*refs, tm, Lb, prompt, blocks_per_seq):
    assert not prompt
    x1_ref, o_ref, wo_ref, hist_ref, g_ref, wu_ref, cw_ref, cb_ref, wd_ref, fg_ref, y_ref, s_ref = refs
    x2 = x1_ref[...] + jnp.dot(o_ref[...], wo_ref[...], preferred_element_type=F32)
    _ffn_body(x2, hist_ref, g_ref, wu_ref, cw_ref, cb_ref, wd_ref, fg_ref, y_ref, s_ref, None,
              tm=tm, Lb=Lb, prompt=False, blocks_per_seq=blocks_per_seq)


def _ffn_body(x, hist_ref, g_ref, wu_ref, cw_ref, cb_ref, wd_ref, fg_ref, y_ref, s_ref, carry_sc,
              *, tm, Lb, prompt, blocks_per_seq):
    if prompt:
        @pl.when(pl.program_id(0) % blocks_per_seq == 0)
        def _():
            carry_sc[...] = jnp.zeros_like(carry_sc)

    hn = _rms(x, g_ref[...]).astype(BF16)
    nsub = D_FF // FF_SUB
    r8 = _iota2((8, 1), 0)
    tpos = _iota2((tm, 1), 0) & (Lb - 1)
    if not prompt:
        trow = _iota2((tm, tm // Lb), 0)
        seq0 = _iota2((tm, tm // Lb), 1) * Lb
        at_row0 = (trow == seq0).astype(BF16)
        at_row1 = (trow == seq0 + 1).astype(BF16)

    def conv(up, cols):
        cw = cw_ref[:, cols]
        cb = cb_ref[:, cols]
        if prompt:
            c8 = carry_sc[:, cols]
            carry_sc[:, cols] = up[tm - 8:tm]
            s_ref[0, :, cols] = up[tm - 2:tm]
            top = up[0:8]
            p1 = jnp.where(r8 >= 1, pltpu.roll(top, 1, 0), c8[7:8])
            p2 = jnp.where(r8 >= 2, pltpu.roll(top, 2, 0), jnp.where(r8 == 0, c8[6:7], c8[7:8]))
            head = (cw[0:1] * p2 + cw[1:2] * p1 + cw[2:3] * top) + cb
            body = (cw[0:1] * pltpu.roll(up, 2, 0) + cw[1:2] * pltpu.roll(up, 1, 0) + cw[2:3] * up) + cb
            return jnp.concatenate([head, body[8:]], axis=0)
        for b in range(tm // Lb):
            s_ref[b, :, cols] = up[b * Lb + Lb - 2:b * Lb + Lb]
        st0, st1 = hist_ref[0, :, cols], hist_ref[1, :, cols]
        h1 = _maskdot(at_row0, st1)
        h2 = _maskdot(at_row0, st0) + _maskdot(at_row1, st1)
        return (cw[0:1] * _shifted(up, 2, tpos, h2) + cw[1:2] * _shifted(up, 1, tpos, h1) + cw[2:3] * up) + cb

    def up_proj(k):
        cg = slice(k * FF_SUB, (k + 1) * FF_SUB)
        cv = slice(D_FF + k * FF_SUB, D_FF + (k + 1) * FF_SUB)
        return (jnp.dot(hn, wu_ref[:, cg], preferred_element_type=F32),
                jnp.dot(hn, wu_ref[:, cv], preferred_element_type=F32))

    def gate_mul(k, ug, uv):
        cg = slice(k * FF_SUB, (k + 1) * FF_SUB)
        cv = slice(D_FF + k * FF_SUB, D_FF + (k + 1) * FF_SUB)
        gate = conv(ug, cg)
        return (gate * _sigmoid(gate) * conv(uv, cv)).astype(BF16)

    ups = {}
    acts = []
    for k in range(nsub + FF_SKEW):
        if k < nsub:
            ups[k] = up_proj(k)
        if k >= FF_SKEW:
            acts.append(gate_mul(k - FF_SKEW, *ups.pop(k - FF_SKEW)))
    acc = x + jnp.dot(jnp.concatenate(acts, axis=1), wd_ref[...], preferred_element_type=F32)
    y_ref[...] = _rms(acc, fg_ref[...])


def _resident_spec(a):
    nd = a.ndim
    return pl.BlockSpec(a.shape, lambda *_: (0,) * nd, pipeline_mode=pl.Buffered(1))


def _ffn_short(x1, o, w_o, hist, p, final_g, tm, Lb):
    n, d = x1.shape
    w_up = p["ffn_w_up"].astype(BF16)
    w_down = p["ffn_w_down"].astype(BF16)
    cw, cb = p["ffn_conv_w"], p["ffn_conv_b"].reshape(1, 2 * D_FF)
    row_spec = pl.BlockSpec((tm, d), lambda i: (i, 0))
    seq_spec = lambda lead: pl.BlockSpec((lead, tm // Lb, 2 * D_FF), lambda i: (0, i, 0))
    consts = [p["norm_ffn_w"].reshape(1, d), w_up, cw, cb, w_down, final_g.reshape(1, d)]
    return pl.pallas_call(
        functools.partial(_ffn_kernel, tm=tm, Lb=Lb, prompt=False, blocks_per_seq=1),
        grid=(n // tm,),
        in_specs=[row_spec, row_spec, _resident_spec(w_o), seq_spec(2)] + [_resident_spec(a) for a in consts],
        out_specs=[row_spec, pl.BlockSpec((tm // Lb, 2, 2 * D_FF), lambda i: (i, 0, 0))],
        out_shape=[jax.ShapeDtypeStruct((n, d), F32), jax.ShapeDtypeStruct((n // Lb, 2, 2 * D_FF), F32)],
        compiler_params=_cparams("arbitrary"),
        name="convffn",
    )(x1, o, w_o, hist, *consts)


def _post_mix_kernel(x_ref, ys_ref, yr_ref, k_ref, v_ref, wo_ref, gxa_ref, wq_ref, wao_ref,
                     g_ref, wu_ref, cw_ref, cb_ref, wd_ref, fg_ref, y_ref, s_ref, carry_sc,
                     *, tm, seq_len):
    wo = wo_ref[...]
    x1 = x_ref[...] + (jnp.dot(ys_ref[...], wo[0:SSD_WIDTH], preferred_element_type=F32)
                       + jnp.dot(yr_ref[...], wo[SSD_WIDTH:], preferred_element_type=F32))
    q = (_bdot(_rms(x1, gxa_ref[...]), wq_ref[...]) * (XA_HEAD_DIM ** -0.5)).astype(BF16)
    outs = []
    for h in range(XA_HEADS):
        sl = slice(h * XA_HEAD_DIM, (h + 1) * XA_HEAD_DIM)
        s = lax.dot_general(q[:, sl], k_ref[0, :, sl].astype(BF16), _NT, preferred_element_type=F32)
        s = s - jnp.max(s, axis=-1, keepdims=True)
        e = jnp.exp(s)
        pr = e / jnp.sum(e, axis=-1, keepdims=True)
        outs.append(_bdot(pr, v_ref[0, :, sl]))
    x2 = x1 + _bdot(jnp.concatenate(outs, axis=1), wao_ref[...])
    _ffn_body(x2, None, g_ref, wu_ref, cw_ref, cb_ref, wd_ref, fg_ref, y_ref, s_ref, carry_sc,
              tm=tm, Lb=seq_len, prompt=True, blocks_per_seq=seq_len // tm)


def _post_mix(x, y_ssd, y_rw, mk, mv, w_out, g_xa, w_q, w_o, p, final_g, tm, seq_len):
    n, d = x.shape
    bps = seq_len // tm
    row = lambda w: pl.BlockSpec((tm, w), lambda i: (i, 0))
    kv_spec = pl.BlockSpec((1, N_MEM, d), lambda i: (i // bps, 0, 0))
    consts = [w_out, g_xa.reshape(1, d), w_q, w_o, p["norm_ffn_w"].reshape(1, d), p["ffn_w_up"].astype(BF16),
              p["ffn_conv_w"], p["ffn_conv_b"].reshape(1, 2 * D_FF), p["ffn_w_down"].astype(BF16),
              final_g.reshape(1, d)]
    y, st = pl.pallas_call(
        functools.partial(_post_mix_kernel, tm=tm, seq_len=seq_len),
        grid=(n // tm,),
        in_specs=[row(d), row(SSD_WIDTH), row(RWKV_WIDTH), kv_spec, kv_spec] + [_resident_spec(a) for a in consts],
        out_specs=[row(d), pl.BlockSpec((1, 2, 2 * D_FF), lambda i: (i, 0, 0))],
        out_shape=[jax.ShapeDtypeStruct((n, d), F32), jax.ShapeDtypeStruct((n // tm, 2, 2 * D_FF), F32)],
        scratch_shapes=[pltpu.VMEM((8, 2 * D_FF), F32)],
        compiler_params=_cparams("arbitrary"),
        name="post_mix",
    )(x, y_ssd, y_rw, mk, mv, *consts)
    return y, st[bps - 1::bps]


def _pair_blocks(sbd):
    b = sbd.shape[0]
    s0 = sbd[:, :, 0:64, 0:64]
    s1 = sbd[:, :, 64:128, 64:128]
    return jnp.stack([s0, s1], axis=2).reshape(b, 8, 64, 64)


def kernel(x_prompt, x_sample, mem_prompt, state_ssm_conv, state_ssm, state_shift, state_wkv, state_ffn_conv, cache_mem_k, cache_mem_v, norm_mix_w, w_in, ssd_conv_w, ssd_conv_b, ssd_dt_bias, ssd_a_log, ssd_d, ssd_norm_w, rwkv_mu, rwkv_w0, rwkv_w2, rwkv_a0, rwkv_a2, rwkv_g2, rwkv_k_k, rwkv_k_a, rwkv_r_k, rwkv_ln_w, rwkv_ln_b, w_out, norm_xa_w, mem_norm_w, xa_w_q, xa_w_k, xa_w_v, xa_w_o, norm_ffn_w, ffn_w_up, ffn_conv_w, ffn_conv_b, ffn_w_down, final_norm_w):
    depth = w_in.shape[0]
    assert depth == 1, "final rmsnorm is fused into the (single) layer's ConvFFN kernel"
    bp, lp, d = x_prompt.shape
    bs, ls, _ = x_sample.shape
    i = 0
    p = dict(ssd_conv_w=ssd_conv_w[i], ssd_conv_b=ssd_conv_b[i], ssd_dt_bias=ssd_dt_bias[i],
             ssd_a_log=ssd_a_log[i], ssd_d=ssd_d[i], ssd_norm_w=ssd_norm_w[i], rwkv_mu=rwkv_mu[i],
             rwkv_w0=rwkv_w0[i], rwkv_w2=rwkv_w2[i], rwkv_a0=rwkv_a0[i], rwkv_a2=rwkv_a2[i],
             rwkv_g2=rwkv_g2[i], rwkv_k_k=rwkv_k_k[i], rwkv_k_a=rwkv_k_a[i],
             rwkv_r_k=rwkv_r_k[i].reshape(-1), rwkv_ln_w=rwkv_ln_w[i], rwkv_ln_b=rwkv_ln_b[i],
             norm_ffn_w=norm_ffn_w[i], ffn_w_up=ffn_w_up[i], ffn_conv_w=ffn_conv_w[i],
             ffn_conv_b=ffn_conv_b[i], ffn_w_down=ffn_w_down[i])

    w_in_t = jnp.swapaxes(w_in[i], 0, 1)
    w_in_p = jnp.concatenate([w_in_t[:SSD_PROJ], jnp.zeros((SEG - SSD_PROJ, d), F32),
                              w_in_t[SSD_PROJ:]], axis=0).astype(BF16)
    w_out_b = w_out[i].astype(BF16)
    w_q_b = (xa_w_q[i]).astype(BF16)
    w_o_b = xa_w_o[i].astype(BF16)
    w_kv_b = jnp.concatenate([xa_w_k[i], xa_w_v[i]], axis=1).astype(BF16)

    xp = x_prompt.reshape(bp * lp, d)
    xs = x_sample.reshape(bs * ls, d)

    mk, mv, mk4, mv4 = _mem_kv(mem_prompt.reshape(bp * N_MEM, d), mem_norm_w[i], w_kv_b, 512)
    mk = mk.reshape(bp, N_MEM, d)
    mv = mv.reshape(bp, N_MEM, d)
    tm_p = min(1024, lp)
    u_p, dt_p, tail_p = _norm_proj(xp, norm_mix_w[i], w_in_p, tm_p, BF16)
    y_ssd_p, ssm_p = _ssd_prompt(u_p, dt_p, p, bp, lp, G=math.gcd(4, bp))
    y_rw_p, wkv_bd_p = _wkv_prompt(u_p, p, bp, lp, G=math.gcd(8, bp))
    y_p, ffn_conv_p = _post_mix(xp, y_ssd_p, y_rw_p, mk, mv, w_out_b, norm_xa_w[i], w_q_b, w_o_b, p,
                                final_norm_w, 512, lp)
    seq_tail = tail_p[lp // tm_p - 1::lp // tm_p]
    ssm_conv_p = seq_tail[:, 5:, SSD_WIDTH:SSD_WIDTH + SSD_XBC]
    shift_p = seq_tail[:, 7, SEG:]

    u_s, dt_s, _ = _norm_proj(xs, norm_mix_w[i], w_in_p, 512, F32)
    assert ls >= 3, "the new conv state is read from the last three rows of each sample sequence"
    y_ssd_s, ssm_s, conv_rows_s = _ssd_sample(u_s, dt_s, jnp.swapaxes(state_ssm_conv[i], 0, 1),
                                              state_ssm[i].reshape(bs, 4, LANES, LANES), p, ls)
    ssm_conv_s = conv_rows_s.reshape(bs, 3, SSD_XBC)
    y_rw_s, wkv_s, shift_s = _wkv_sample(u_s, state_shift[i], state_wkv[i], p, ls)
    x1_s, q_s = _outproj(xs, y_ssd_s, y_rw_s, w_out_b, norm_xa_w[i], w_q_b, 512)
    o_s = _xattn_cache(q_s, cache_mem_k[i], cache_mem_v[i], 16, ls)
    y_s, ffn_conv_s = _ffn_short(x1_s, o_s, w_o_b, jnp.swapaxes(state_ffn_conv[i], 0, 1), p, final_norm_w, 256, ls)

    e = lambda a: a[None]
    return (y_p.reshape(bp, lp, d), y_s.reshape(bs, ls, d),
            e(ssm_conv_p), e(ssm_conv_s),
            e(ssm_p.reshape(bp, SSD_HEADS, 64, SSD_STATE)), e(ssm_s.reshape(bs, SSD_HEADS, 64, SSD_STATE)),
            e(shift_p), e(shift_s),
            e(_pair_blocks(wkv_bd_p)), e(wkv_s),
            e(ffn_conv_p), e(ffn_conv_s),
            e(mk4.reshape(bp, N_MEM, XA_HEADS, XA_HEAD_DIM)), e(mv4.reshape(bp, N_MEM, XA_HEADS, XA_HEAD_DIM)))
```

```python
import functools
import math

import jax
import jax.numpy as jnp
from jax import lax
from jax.experimental import pallas as pl
from jax.experimental.pallas import tpu as pltpu

F32 = jnp.float32
BF16 = jnp.bfloat16

D_MODEL = 1024
N_MEM = 256
XA_HEADS = 4
XA_HEAD_DIM = D_MODEL // XA_HEADS
SSD_WIDTH = 512
SSD_HEADS = 8
SSD_STATE = 128
SSD_XBC = 1024
SSD_PROJ = SSD_WIDTH + SSD_XBC + SSD_HEADS
RWKV_WIDTH = 512
RWKV_PROJ = 1792
D_FF = 2816
EPS = 1e-6
GN_EPS = 64e-5

LANES = 128
SEG = 1792
DT_OFF = SSD_WIDTH + SSD_XBC
VMEM_LIMIT_BYTES = 56 * 1024 * 1024


def _cparams(*sem):
    return pltpu.CompilerParams(dimension_semantics=sem, vmem_limit_bytes=VMEM_LIMIT_BYTES)


_NN = (((1,), (0,)), ((), ()))
_NT = (((1,), (1,)), ((), ()))
_TN = (((0,), (0,)), ((), ()))


def _bdot(a, b, dims=_NN):
    return lax.dot_general(a.astype(BF16), b.astype(BF16), dims, preferred_element_type=F32)


def _split2(x):
    hi = x.astype(BF16)
    lo = (x - hi.astype(F32)).astype(BF16)
    return hi, lo


def _maskdot(m_bf16, x):
    hi = x.astype(BF16)
    r1 = x - hi.astype(F32)
    mid = r1.astype(BF16)
    lo = (r1 - mid.astype(F32)).astype(BF16)
    return jnp.dot(jnp.concatenate([m_bf16, m_bf16, m_bf16], axis=1),
                   jnp.concatenate([hi, mid, lo], axis=0), preferred_element_type=F32)


def _sigmoid(x):
    return 0.5 + 0.5 * jnp.tanh(0.5 * x)


def _softplus(x):
    return jnp.maximum(x, 0.0) + jnp.log1p(jnp.exp(-jnp.abs(x)))


def _rms(x, g):
    return x * lax.rsqrt(jnp.mean(x * x, axis=-1, keepdims=True) + EPS) * g


def _iota2(shape, dim):
    return lax.broadcasted_iota(jnp.int32, shape, dim)


def _place_rows(rows, nseq, Lb, t, state_rows):
    r = _iota2((rows, nseq), 0)
    first = _iota2((rows, nseq), 1) * Lb
    return _maskdot((r == first + t).astype(BF16), state_rows)


def _shifted(x, k, tpos, hist):
    return jnp.where(tpos >= k, pltpu.roll(x, k, 0), hist)


def _norm_proj_kernel(x_ref, g_ref, wt_ref, o_ref, dt_ref, tail_ref, hn_sc):
    j = pl.program_id(1)

    @pl.when(j == 0)
    def _():
        hn_sc[...] = _rms(x_ref[...], g_ref[...]).astype(BF16)

    acc = lax.dot_general(hn_sc[...], wt_ref[...], _NT, preferred_element_type=F32)
    o_ref[...] = acc.astype(o_ref.dtype)
    tail_ref[0] = acc[acc.shape[0] - 8:]

    @pl.when(j == 0)
    def _():
        dt_ref[...] = acc[:, DT_OFF:DT_OFF + LANES]


def _norm_proj(x, g, wt_bf16, tm, out_dtype):
    n, d = x.shape
    return pl.pallas_call(
        _norm_proj_kernel,
        grid=(n // tm, 2),
        in_specs=[pl.BlockSpec((tm, d), lambda i, j: (i, 0)),
                  pl.BlockSpec((1, d), lambda i, j: (0, 0)),
                  pl.BlockSpec((SEG, d), lambda i, j: (j, 0))],
        out_specs=[pl.BlockSpec((tm, SEG), lambda i, j: (i, j)),
                   pl.BlockSpec((tm, LANES), lambda i, j: (i, 0)),
                   pl.BlockSpec((1, 8, SEG), lambda i, j: (i, 0, j))],
        out_shape=[jax.ShapeDtypeStruct((n, 2 * SEG), out_dtype),
                   jax.ShapeDtypeStruct((n, LANES), F32),
                   jax.ShapeDtypeStruct((n // tm, 8, 2 * SEG), F32)],
        scratch_shapes=[pltpu.VMEM((tm, d), BF16)],
        compiler_params=_cparams("parallel", "arbitrary"),
        name="norm_proj",
    )(x, g.reshape(1, d), wt_bf16)


def _mem_kv_kernel(x_ref, g_ref, w_ref, k_ref, v_ref, k4_ref, v4_ref):
    hn = _rms(x_ref[...], g_ref[...]).astype(BF16)
    d = k_ref.shape[1]
    k = jnp.dot(hn, w_ref[:, 0:d], preferred_element_type=F32)
    v = jnp.dot(hn, w_ref[:, d:2 * d], preferred_element_type=F32)
    k_ref[...] = k
    v_ref[...] = v
    for h in range(XA_HEADS):
        sl = slice(h * XA_HEAD_DIM, (h + 1) * XA_HEAD_DIM)
        k4_ref[:, h, :] = k[:, sl]
        v4_ref[:, h, :] = v[:, sl]


def _mem_kv(x, g, w_kv_bf16, tm):
    n, d = x.shape
    row_spec = pl.BlockSpec((tm, d), lambda i: (i, 0))
    head_spec = pl.BlockSpec((tm, XA_HEADS, XA_HEAD_DIM), lambda i: (i, 0, 0))
    return pl.pallas_call(
        _mem_kv_kernel,
        grid=(n // tm,),
        in_specs=[row_spec, pl.BlockSpec((1, d), lambda i: (0, 0)), _const_spec(w_kv_bf16)],
        out_specs=[row_spec, row_spec, head_spec, head_spec],
        out_shape=[jax.ShapeDtypeStruct((n, d), F32)] * 2
                  + [jax.ShapeDtypeStruct((n, XA_HEADS, XA_HEAD_DIM), F32)] * 2,
        compiler_params=_cparams("parallel"),
        name="mem_kv",
    )(x, g.reshape(1, d), w_kv_bf16)


def _ssd_kernel(*refs, Q, Lb, prompt, G=1):
    if not prompt:
        return _ssd_block(*refs, Q=Q, Lb=Lb, prompt=False)
    (u_ref, dt_ref, cw_ref, cb_ref, dtb_ref, an_ref, dsk_ref, nw_ref, y_ref, hout_ref, carry_sc, h_sc) = refs

    @pl.when(pl.program_id(1) == 0)
    def _():
        carry_sc[...] = jnp.zeros_like(carry_sc)
        h_sc[...] = jnp.zeros_like(h_sc)

    for gi in range(G):
        _ssd_block(u_ref.at[gi], dt_ref.at[gi], cw_ref, cb_ref, dtb_ref, an_ref, dsk_ref, nw_ref,
                   y_ref.at[gi], hout_ref.at[pl.ds(gi, 1)], carry_sc.at[gi], h_sc.at[pl.ds(4 * gi, 4)],
                   Q=Q, Lb=Lb, prompt=True)


def _ssd_block(*refs, Q, Lb, prompt):
    ns = Q // Lb
    lb = int(math.log2(Lb))
    rows = _iota2((Q, 1), 0)
    tpos = rows & (Lb - 1)
    if prompt:
        (u_ref, dt_ref, cw_ref, cb_ref, dtb_ref, an_ref, dsk_ref, nw_ref,
         y_ref, hout_ref, carry_sc, h_sc) = refs
    else:
        (u_ref, dt_ref, hist_ref, hin_ref, sel_ref, cw_ref, cb_ref, dtb_ref, an_ref, dsk_ref, nw_ref,
         y_ref, hout_ref, cs_ref) = refs
    u = u_ref[...].astype(F32)
    z = u[:, 0:SSD_WIDTH]
    z_gate = z * _sigmoid(z)
    x = u[:, SSD_WIDTH:SSD_WIDTH + SSD_XBC]
    if prompt:
        c8 = carry_sc[...]
        h1 = c8[7:8]
        h2 = jnp.where(rows == 0, c8[6:7], c8[7:8])
        h3 = jnp.where(rows == 0, c8[5:6], jnp.where(rows == 1, c8[6:7], c8[7:8]))
        carry_sc[...] = x[Q - 8:Q]
    else:
        cs_ref[...] = _maskdot(sel_ref[...], x)
        st = [hist_ref[j] for j in range(3)]
        put = functools.partial(_place_rows, Q, ns, Lb)
        h1 = put(0, st[2])
        h2 = put(0, st[1]) + put(1, st[2])
        h3 = put(0, st[0]) + put(1, st[1]) + put(2, st[2])
    cw = cw_ref[...]
    xc = (cw[0:1] * _shifted(x, 3, tpos, h3) + cw[1:2] * _shifted(x, 2, tpos, h2)
          + cw[2:3] * _shifted(x, 1, tpos, h1) + cw[3:4] * x) + cb_ref[...]
    xc = xc * _sigmoid(xc)
    dtr = dt_ref[...]
    xs = xc[:, 0:SSD_WIDTH]
    bm = xc[:, SSD_WIDTH:SSD_WIDTH + 2 * SSD_STATE]
    cm = xc[:, SSD_WIDTH + 2 * SSD_STATE:]

    dt = _softplus(dtr + dtb_ref[...])
    da = dt * an_ref[...]

    ri = _iota2((Q, Q), 0)
    ci = _iota2((Q, Q), 1)
    same = (ri >> lb) == (ci >> lb)
    tril = same & (ci <= ri)
    sel = ci == (((ri >> lb) << lb) + (Lb - 1))
    acs = _maskdot(tril.astype(BF16), da)
    acs_t = acs.T
    acs_last = _maskdot(sel.astype(BF16), acs)
    dec_end = jnp.exp(acs_last - acs)
    eacs = jnp.exp(acs)
    seqid = rows >> lb

    lane = _iota2((Q, LANES), 1)
    lo_half = lane < 64
    prow = _iota2((LANES, LANES), 0)
    dsk = dsk_ref[...]

    ys = []
    for q in range(4):
        g = q // 2
        h0, h1i = 2 * q, 2 * q + 1
        if q % 2 == 0:
            cg = cm[:, g * SSD_STATE:(g + 1) * SSD_STATE]
            bg = bm[:, g * SSD_STATE:(g + 1) * SSD_STATE]
            cb_g = _bdot(cg, bg, _NT)
            if ns == 1:
                cexp, bexp = cg.astype(BF16), bg.astype(BF16)
            else:
                cexp = jnp.concatenate([jnp.where(seqid == b, cg, 0.0).astype(BF16) for b in range(ns)], axis=1)
                bexp = jnp.concatenate([jnp.where(seqid == b, bg, 0.0).astype(BF16) for b in range(ns)], axis=1)
        m0 = jnp.where(tril, cb_g * jnp.exp(acs[:, h0:h0 + 1] - acs_t[h0:h0 + 1, :]), 0.0)
        m1 = jnp.where(tril, cb_g * jnp.exp(acs[:, h1i:h1i + 1] - acs_t[h1i:h1i + 1, :]), 0.0)
        xp = xs[:, q * LANES:(q + 1) * LANES]
        xdt = xp * jnp.where(lo_half, dt[:, h0:h0 + 1], dt[:, h1i:h1i + 1])
        xdt0 = jnp.where(lo_half, xdt, 0.0)
        xdt1 = jnp.where(lo_half, 0.0, xdt)
        ydiag = _bdot(jnp.concatenate([m0, m1], axis=1), jnp.concatenate([xdt0, xdt1], axis=0))
        if prompt:
            hst = h_sc[q]
        else:
            hst = jnp.concatenate([hin_ref[b, q] for b in range(ns)], axis=1)
        ecs = jnp.where(lo_half, eacs[:, h0:h0 + 1], eacs[:, h1i:h1i + 1])
        yoff = _bdot(cexp, hst, _NT) * ecs
        xd = xdt * jnp.where(lo_half, dec_end[:, h0:h0 + 1], dec_end[:, h1i:h1i + 1])
        incr = _bdot(xd, bexp, _TN)
        scales = []
        for b in range(ns):
            r = b * Lb + Lb - 1
            e = eacs[r:r + 1, :]
            scales.append(jnp.where(prow < 64, e[:, h0:h0 + 1], e[:, h1i:h1i + 1]))
        scale = scales[0] if ns == 1 else jnp.concatenate(scales, axis=1)
        hnew = hst * scale + incr
        if prompt:
            h_sc[q] = hnew
            hout_ref[0, q] = hnew
        else:
            for b in range(ns):
                hout_ref[b, q] = hnew[:, b * LANES:(b + 1) * LANES]
        ys.append(ydiag + yoff + dsk[:, q * LANES:(q + 1) * LANES] * xp)

    y = jnp.concatenate(ys, axis=1)
    y = y * z_gate
    half = SSD_WIDTH // 2
    outs = []
    for g in range(2):
        yg = y[:, g * half:(g + 1) * half]
        outs.append(yg * lax.rsqrt(jnp.mean(yg * yg, axis=-1, keepdims=True) + EPS))
    y_ref[...] = (jnp.concatenate(outs, axis=1) * nw_ref[...]).astype(y_ref.dtype)


def _ssd_params(p):
    an = jnp.zeros((1, LANES), F32).at[0, :SSD_HEADS].set(-jnp.exp(p["ssd_a_log"]))
    dtb = jnp.zeros((1, LANES), F32).at[0, :SSD_HEADS].set(p["ssd_dt_bias"])
    dsk = jnp.repeat(p["ssd_d"], SSD_WIDTH // SSD_HEADS).reshape(1, SSD_WIDTH)
    return [p["ssd_conv_w"], p["ssd_conv_b"].reshape(1, SSD_XBC), dtb, an, dsk,
            p["ssd_norm_w"].reshape(1, SSD_WIDTH)]


def _const_spec(a):
    nd = a.ndim
    return pl.BlockSpec(a.shape, lambda *_: (0,) * nd)


def _ssd_prompt(u, dt, p, batch, seq, Q=128, G=4):
    nc = seq // Q
    params = _ssd_params(p)
    y, hout = pl.pallas_call(
        functools.partial(_ssd_kernel, Q=Q, Lb=Q, prompt=True, G=G),
        grid=(batch // G, nc),
        in_specs=[pl.BlockSpec((G, Q, SEG), lambda b, c: (b, c, 0)),
                  pl.BlockSpec((G, Q, LANES), lambda b, c: (b, c, 0))] + [_const_spec(a) for a in params],
        out_specs=[pl.BlockSpec((G, Q, SSD_WIDTH), lambda b, c: (b, c, 0)),
                   pl.BlockSpec((G, 4, LANES, LANES), lambda b, c: (b, 0, 0, 0))],
        out_shape=[jax.ShapeDtypeStruct((batch, seq, SSD_WIDTH), BF16),
                   jax.ShapeDtypeStruct((batch, 4, LANES, LANES), F32)],
        scratch_shapes=[pltpu.VMEM((G, 8, SSD_XBC), F32), pltpu.VMEM((4 * G, LANES, LANES), F32)],
        compiler_params=_cparams("parallel", "arbitrary"),
        name="ssd_prompt",
    )(u.reshape(batch, seq, 2 * SEG), dt.reshape(batch, seq, LANES), *params)
    return y.reshape(batch * seq, SSD_WIDTH), hout


def _row_selector(nseq, Lb, offsets):
    k = len(offsets)
    rows = jnp.arange(nseq * k)
    target = (rows // k) * Lb + jnp.asarray(offsets)[rows % k]
    return (target[:, None] == jnp.arange(nseq * Lb)[None, :]).astype(BF16)


def _ssd_sample(u, dt, hist, hin, p, Lb, Q=64):
    n = u.shape[0]
    ns = Q // Lb
    params = _ssd_params(p)
    sel = _row_selector(ns, Lb, (Lb - 3, Lb - 2, Lb - 1))
    return pl.pallas_call(
        functools.partial(_ssd_kernel, Q=Q, Lb=Lb, prompt=False),
        grid=(n // Q,),
        in_specs=[pl.BlockSpec((Q, SEG), lambda i: (i, 0)),
                  pl.BlockSpec((Q, LANES), lambda i: (i, 0)),
                  pl.BlockSpec((3, ns, SSD_XBC), lambda i: (0, i, 0)),
                  pl.BlockSpec((ns, 4, LANES, LANES), lambda i: (i, 0, 0, 0)),
                  _const_spec(sel)]
                 + [_const_spec(a) for a in params],
        out_specs=[pl.BlockSpec((Q, SSD_WIDTH), lambda i: (i, 0)),
                   pl.BlockSpec((ns, 4, LANES, LANES), lambda i: (i, 0, 0, 0)),
                   pl.BlockSpec((ns * 3, SSD_XBC), lambda i: (i, 0))],
        out_shape=[jax.ShapeDtypeStruct((n, SSD_WIDTH), BF16),
                   jax.ShapeDtypeStruct(hin.shape, F32),
                   jax.ShapeDtypeStruct((n // Lb * 3, SSD_XBC), F32)],
        compiler_params=_cparams("parallel"),
        name="ssd_sample",
    )(u, dt, hist, hin, sel, *params)


def _dot3s(a, b, dims=_NN):
    ka = dims[0][0][0]
    kb = dims[0][1][0]
    lhs = jnp.concatenate([a[0], a[0], a[1]], axis=ka)
    rhs = jnp.concatenate([b[0], b[1], b[0]], axis=kb)
    return lax.dot_general(lhs, rhs, dims, preferred_element_type=F32)


def _cat2(parts, axis):
    return (jnp.concatenate([p[0] for p in parts], axis=axis), jnp.concatenate([p[1] for p in parts], axis=axis))


def _tri_inverse(mats, lb):
    n = mats[0].shape[0]
    ri = _iota2((n, n), 0)
    ci = _iota2((n, n), 1)
    off1 = ((ri >> 1) == (ci >> 1)) & ((ri & 1) == 1) & ((ci & 1) == 0)
    eye = jnp.where(ri == ci, 1.0, 0.0)
    ts = [eye + jnp.where(off1, a, 0.0) for a in mats]
    for lvl in range(1, lb):
        m = 1 << lvl
        off = ((ri >> (lvl + 1)) == (ci >> (lvl + 1))) & ((ri & (2 * m - 1)) >= m) & ((ci & (2 * m - 1)) < m)
        tsb = [t.astype(BF16) for t in ts]
        ws = [jnp.dot(jnp.where(off, a, 0.0).astype(BF16), tb, preferred_element_type=F32)
              for a, tb in zip(mats, tsb)]
        ts = [t + jnp.dot(tb, w.astype(BF16), preferred_element_type=F32) for t, tb, w in zip(ts, tsb, ws)]
    return [t.astype(BF16) for t in ts]


def _refined_solve(tinvs, mats, rhss):
    d = functools.partial(jnp.dot, preferred_element_type=F32)
    n = range(len(mats))
    rs = [_split2(r) for r in rhss]
    u0 = [d(jnp.concatenate([tinvs[i], tinvs[i]], axis=1), jnp.concatenate(rs[i], axis=0)) for i in n]
    au = [_dot3s(_split2(mats[i]), _split2(u0[i])) for i in n]
    res = [((rhss[i] - u0[i]) + au[i]).astype(BF16) for i in n]
    return [u0[i] + d(tinvs[i], res[i]) for i in n]


def _wkv_kernel(*refs, C, Lb, prompt, G):
    nb = C // Lb
    lb = int(math.log2(Lb))
    R = 2 * C
    T = G * C
    if prompt:
        (u_ref, mu_ref, w0_ref, w2_ref, a0_ref, a2_ref, g2_ref, kk_ref, ka_ref, rk_ref, lnw_ref, lnb_ref,
         y_ref, sout_ref, carry_sc, s_sc) = refs
        c = pl.program_id(1)

        @pl.when(c == 0)
        def _():
            carry_sc[...] = jnp.zeros_like(carry_sc)
            s_sc[...] = jnp.zeros_like(s_sc)
    else:
        (u_ref, hist_ref, sin_ref, sel_ref, mu_ref, w0_ref, w2_ref, a0_ref, a2_ref, g2_ref, kk_ref, ka_ref,
         rk_ref, lnw_ref, lnb_ref, y_ref, sout_ref, shift_ref) = refs

    rows = _iota2((T, 1), 0)
    tpos = rows & (Lb - 1)
    if prompt:
        u = u_ref[...].astype(F32).reshape(T, SEG)
        hist = jnp.concatenate([jnp.broadcast_to(carry_sc[g, 7:8], (C, SEG)) for g in range(G)], axis=0)
        for g in range(G):
            carry_sc[g] = u[g * C + C - 8:(g + 1) * C]
    else:
        u = u_ref[...].astype(F32)
        hist = _place_rows(T, G * nb, Lb, 0, hist_ref[...])
        shift_ref[...] = _maskdot(sel_ref[...], u)
    um = u + (_shifted(u, 1, tpos, hist) - u) * mu_ref[...]

    W = RWKV_WIDTH
    r = um[:, 0:W]
    k = um[:, W:2 * W]
    v = um[:, 2 * W:3 * W]
    t12 = um[:, 3 * W:3 * W + LANES]
    lg = um[:, 3 * W + LANES:3 * W + 2 * LANES]

    wl = w0_ref[...] + _bdot(jnp.tanh(t12), w2_ref[...])
    logw = -math.exp(-0.5) * _sigmoid(wl)
    a = _sigmoid(a0_ref[...] + _bdot(t12, a2_ref[...]))
    out_gate = _bdot(_sigmoid(lg), g2_ref[...])

    hi = _iota2((W, W), 0)
    hj = _iota2((W, W), 1)
    headsum = ((hi >> 6) == (hj >> 6)).astype(BF16)

    kk = k * kk_ref[...]
    kk = kk * lax.rsqrt(jnp.maximum(_bdot(kk * kk, headsum), 1e-24))
    kmod = k * (1.0 + (a - 1.0) * ka_ref[...])
    beta = kk * a

    ri = _iota2((T, T), 0)
    ci = _iota2((T, T), 1)
    same_c = (ri >> lb) == (ci >> lb)
    lc = _maskdot((same_c & (ci <= ri)).astype(BF16), logw)
    e_pos = jnp.exp(lc)
    e_neg = jnp.exp(-lc)
    at = -kk * jnp.exp(lc - logw)
    rt = r * e_pos
    bt = beta * e_neg
    kt = kmod * e_neg

    si = _iota2((R, R), 0)
    sj = _iota2((R, R), 1)
    same_s = (si >> lb) == (sj >> lb)
    strict = same_s & (sj < si)
    incl = same_s & (sj <= si)
    incl2 = jnp.concatenate([incl, incl], axis=1)
    lane_row = _iota2((1, LANES), 1)
    m_lo = jnp.where(lane_row < 64, 1.0, 0.0).astype(BF16)
    m_hi = jnp.where(lane_row < 64, 0.0, 1.0).astype(BF16)
    if nb > 1:
        seq_s = (_iota2((R, LANES), 0) & (C - 1)) >> lb
        seq_masks = [jnp.where(seq_s == b, 1.0, 0.0).astype(BF16) for b in range(nb)]

    def stack(xp):
        return jnp.concatenate([xp * m_lo, xp * m_hi], axis=0)

    def expand(xs_):
        if nb == 1:
            return xs_
        return jnp.concatenate([xs_ * seq_masks[b] for b in range(nb)], axis=1)

    probs = [(g, p) for g in range(G) for p in range(4)]
    NP = range(len(probs))

    def tile(x, i):
        g, p = probs[i]
        return x[g * C:(g + 1) * C, p * LANES:(p + 1) * LANES]

    def stacked(pair, i):
        return (stack(tile(pair[0], i)), stack(tile(pair[1], i)))

    at2, v2, bt2, kt2 = (_split2(x) for x in (at, v, bt, kt))
    rt_b = rt.astype(BF16)
    a_s = [stacked(at2, i) for i in NP]
    r_s = [stack(tile(rt_b, i)) for i in NP]
    v_s = [stacked(v2, i) for i in NP]
    bk_s = [_cat2([stacked(bt2, i), stacked(kt2, i)], 0) for i in NP]
    gm_a = [_dot3s(a_s[i], bk_s[i], _NT) for i in NP]
    gm_r = [lax.dot_general(r_s[i], bk_s[i][0], _NT, preferred_element_type=F32) for i in NP]
    a_ab = [jnp.where(strict, gm_a[i][:, 0:R], 0.0) for i in NP]
    a_ak = [_split2(jnp.where(strict, gm_a[i][:, R:2 * R], 0.0)) for i in NP]
    a_r = [jnp.where(incl2, gm_r[i], 0.0).astype(BF16) for i in NP]
    akv = [_dot3s(a_ak[i], v_s[i]) for i in NP]
    tinv = _tri_inverse(a_ab, lb)

    if prompt:
        sst = [s_sc[i] for i in NP]
    else:
        z64 = jnp.zeros((64, 64), F32)

        def pair_blockdiag(b, p):
            top = jnp.concatenate([sin_ref[b, 2 * p], z64], axis=1)
            bot = jnp.concatenate([z64, sin_ref[b, 2 * p + 1]], axis=1)
            return jnp.concatenate([top, bot], axis=0)

        sst = [jnp.concatenate([pair_blockdiag(g * nb + b, p) for b in range(nb)], axis=1) for g, p in probs]
    ss = [_split2(s) for s in sst]
    ar0_a = [_dot3s((expand(a_s[i][0]), expand(a_s[i][1])), ss[i], _NT) for i in NP]
    ar0_r = [lax.dot_general(expand(r_s[i]), ss[i][0], _NT, preferred_element_type=F32) for i in NP]
    us = [_split2(x) for x in _refined_solve(tinv, a_ab, [ar0_a[i] + akv[i] for i in NP])]
    uv = [_cat2([us[i], v_s[i]], 0) for i in NP]
    yst = [ar0_r[i] + jnp.dot(a_r[i], uv[i][0], preferred_element_type=F32) for i in NP]
    ys = [yst[i][0:C] + yst[i][C:R] for i in NP]
    for i in NP:
        g, p = probs[i]
        pl_lanes = tile(e_pos, i)
        plast = jnp.concatenate([pl_lanes[b * Lb + Lb - 1:b * Lb + Lb] for b in range(nb)], axis=1)
        bk_x = tuple(jnp.concatenate([expand(x[0:R]), expand(x[R:2 * R])], axis=0) for x in bk_s[i])
        snew = (sst[i] + _dot3s(uv[i], bk_x, _TN)) * plast
        if prompt:
            s_sc[i] = snew
            sout_ref[g, p] = snew
        else:
            for b in range(nb):
                sout_ref[g * nb + b, 2 * p] = snew[0:64, b * LANES:b * LANES + 64]
                sout_ref[g * nb + b, 2 * p + 1] = snew[64:128, b * LANES + 64:(b + 1) * LANES]

    y = jnp.concatenate([jnp.concatenate(ys[4 * g:4 * g + 4], axis=1) for g in range(G)], axis=0)
    inv_d = 1.0 / 64.0
    mean = _bdot(y, headsum) * inv_d
    yc = y - mean
    var = _bdot(yc * yc, headsum) * inv_d
    yn = yc * lax.rsqrt(var + GN_EPS) * lnw_ref[...] + lnb_ref[...]
    yn = yn + _bdot(r * kmod * rk_ref[...], headsum) * v
    y_ref[...] = (yn * out_gate).astype(y_ref.dtype).reshape(y_ref.shape)


def _wkv_params(p):
    z64 = jnp.zeros((64, RWKV_WIDTH), F32)
    w2 = jnp.concatenate([p["rwkv_w2"], z64], axis=0).astype(BF16)
    a2 = jnp.concatenate([z64, p["rwkv_a2"]], axis=0).astype(BF16)
    row = lambda a: a.reshape(1, -1)
    return [row(p["rwkv_mu"]), row(p["rwkv_w0"]), w2, row(p["rwkv_a0"]), a2, p["rwkv_g2"].astype(BF16),
            row(p["rwkv_k_k"]), row(p["rwkv_k_a"]), row(p["rwkv_r_k"]), row(p["rwkv_ln_w"]), row(p["rwkv_ln_b"])]


def _wkv_prompt(u, p, batch, seq, C=64, G=8):
    nc = seq // C
    params = _wkv_params(p)
    y, sout = pl.pallas_call(
        functools.partial(_wkv_kernel, C=C, Lb=C, prompt=True, G=G),
        grid=(batch // G, nc),
        in_specs=[pl.BlockSpec((G, C, SEG), lambda b, c: (b, c, 1))] + [_const_spec(a) for a in params],
        out_specs=[pl.BlockSpec((G, C, RWKV_WIDTH), lambda b, c: (b, c, 0)),
                   pl.BlockSpec((G, 4, LANES, LANES), lambda b, c: (b, 0, 0, 0))],
        out_shape=[jax.ShapeDtypeStruct((batch, seq, RWKV_WIDTH), BF16),
                   jax.ShapeDtypeStruct((batch, 4, LANES, LANES), F32)],
        scratch_shapes=[pltpu.VMEM((G, 8, SEG), F32), pltpu.VMEM((4 * G, LANES, LANES), F32)],
        compiler_params=_cparams("parallel", "arbitrary"),
        name="wkv_prompt",
    )(u.reshape(batch, seq, 2 * SEG), *params)
    return y.reshape(batch * seq, RWKV_WIDTH), sout


def _wkv_sample(u, hist, sin, p, Lb, C=64, G=2):
    n = u.shape[0]
    nb = C // Lb
    T = G * C
    params = _wkv_params(p)
    sel = _row_selector(G * nb, Lb, (Lb - 1,))
    return pl.pallas_call(
        functools.partial(_wkv_kernel, C=C, Lb=Lb, prompt=False, G=G),
        grid=(n // T,),
        in_specs=[pl.BlockSpec((T, SEG), lambda i: (i, 1)),
                  pl.BlockSpec((G * nb, SEG), lambda i: (i, 0)),
                  pl.BlockSpec((G * nb, 8, 64, 64), lambda i: (i, 0, 0, 0)),
                  _const_spec(sel)]
                 + [_const_spec(a) for a in params],
        out_specs=[pl.BlockSpec((T, RWKV_WIDTH), lambda i: (i, 0)),
                   pl.BlockSpec((G * nb, 8, 64, 64), lambda i: (i, 0, 0, 0)),
                   pl.BlockSpec((G * nb, SEG), lambda i: (i, 0))],
        out_shape=[jax.ShapeDtypeStruct((n, RWKV_WIDTH), BF16),
                   jax.ShapeDtypeStruct(sin.shape, F32),
                   jax.ShapeDtypeStruct((n // Lb, SEG), F32)],
        compiler_params=_cparams("parallel"),
        name="wkv_sample",
    )(u, hist, sin, sel, *params)


def _outproj_kernel(x_ref, ys_ref, yr_ref, wo_ref, g_ref, wq_ref, x1_ref, q_ref):
    wo = wo_ref[...]
    x1 = x_ref[...] + (_bdot(ys_ref[...], wo[0:SSD_WIDTH]) + _bdot(yr_ref[...], wo[SSD_WIDTH:]))
    x1_ref[...] = x1
    q_ref[...] = _bdot(_rms(x1, g_ref[...]), wq_ref[...])


def _outproj(x, y_ssd, y_rw, w_out, g, w_q, tm):
    n, d = x.shape
    return pl.pallas_call(
        _outproj_kernel,
        grid=(n // tm,),
        in_specs=[pl.BlockSpec((tm, d), lambda i: (i, 0)),
                  pl.BlockSpec((tm, SSD_WIDTH), lambda i: (i, 0)),
                  pl.BlockSpec((tm, RWKV_WIDTH), lambda i: (i, 0)),
                  _const_spec(w_out), pl.BlockSpec((1, d), lambda i: (0, 0)), _const_spec(w_q)],
        out_specs=[pl.BlockSpec((tm, d), lambda i: (i, 0)), pl.BlockSpec((tm, d), lambda i: (i, 0))],
        out_shape=[jax.ShapeDtypeStruct((n, d), F32), jax.ShapeDtypeStruct((n, d), F32)],
        compiler_params=_cparams("parallel"),
        name="outproj_q",
    )(x, y_ssd, y_rw, w_out, g.reshape(1, d), w_q)


CACHE_BUFFERS = 3


def _xattn_cache_kernel(q_ref, k_hbm, v_hbm, o_ref, kbuf, vbuf, sem, *, seq_len):
    tq = q_ref.shape[0]
    nmem = tq // seq_len
    step = pl.program_id(0)
    nsteps = pl.num_programs(0)

    def fetch(s, slot):
        src = pl.ds(s * nmem, nmem)
        return (pltpu.make_async_copy(k_hbm.at[src], kbuf.at[slot], sem.at[0, slot]),
                pltpu.make_async_copy(v_hbm.at[src], vbuf.at[slot], sem.at[1, slot]))

    @pl.when(step == 0)
    def _():
        for s in range(CACHE_BUFFERS - 1):
            for c in fetch(s, s):
                c.start()

    ahead = step + (CACHE_BUFFERS - 1)

    @pl.when(ahead < nsteps)
    def _():
        for c in fetch(ahead, ahead % CACHE_BUFFERS):
            c.start()

    slot = step % CACHE_BUFFERS
    for c in fetch(step, slot):
        c.wait()
    k_ref = kbuf.at[slot]
    v_ref = vbuf.at[slot]
    rows = N_MEM * XA_HEADS
    cols = XA_HEADS * tq
    q = q_ref[...] * (XA_HEAD_DIM ** -0.5)
    qblk = jnp.concatenate([q[:, h * XA_HEAD_DIM:(h + 1) * XA_HEAD_DIM] for h in range(XA_HEADS)], axis=0)
    key_head = _iota2((cols, rows), 1) & (XA_HEADS - 1)
    qrow = _iota2((cols, 1), 0)
    head_ok = key_head == (qrow >> int(math.log2(tq)))
    out_mem = _iota2((tq, 1), 0) >> int(math.log2(seq_len))
    o = None
    for m in range(nmem):
        kall = k_ref[m].reshape(rows, XA_HEAD_DIM)
        vall = v_ref[m].reshape(rows, XA_HEAD_DIM)
        s = jnp.where(head_ok, _bdot(qblk, kall, _NT), -jnp.inf)
        e = jnp.exp(s - jnp.max(s, axis=-1, keepdims=True))
        pr = e / jnp.sum(e, axis=-1, keepdims=True)
        om = _bdot(pr, vall)
        om = jnp.concatenate([om[h * tq:(h + 1) * tq] for h in range(XA_HEADS)], axis=1)
        o = om if o is None else jnp.where(out_mem == m, om, o)
    o_ref[...] = o.astype(o_ref.dtype)


def _xattn_cache(q, ck, cv, tq, seq_len):
    n, d = q.shape
    nmem = tq // seq_len
    assert n // tq >= CACHE_BUFFERS - 1
    row_spec = pl.BlockSpec((tq, d), lambda i: (i, 0))
    hbm_spec = pl.BlockSpec(memory_space=pl.ANY)
    slot_shape = (CACHE_BUFFERS, nmem, N_MEM, XA_HEADS, XA_HEAD_DIM)
    return pl.pallas_call(
        functools.partial(_xattn_cache_kernel, seq_len=seq_len),
        grid=(n // tq,),
        in_specs=[row_spec, hbm_spec, hbm_spec],
        out_specs=row_spec,
        out_shape=jax.ShapeDtypeStruct((n, d), BF16),
        scratch_shapes=[pltpu.VMEM(slot_shape, F32), pltpu.VMEM(slot_shape, F32),
                        pltpu.SemaphoreType.DMA((2, CACHE_BUFFERS))],
        compiler_params=_cparams("arbitrary"),
        name="xattn_cache",
    )(q, ck, cv)


FF_SUB = 256
FF_SKEW = 1


def _ffn_kernel(*refs, tm, Lb, prompt, blocks_per_seq):
    assert not prompt
    x1_ref, o_ref, wo_ref, hist_ref, g_ref, wu_ref, cw_ref, cb_ref, wd_ref, fg_ref, y_ref, s_ref = refs
    x2 = x1_ref[...] + jnp.dot(o_ref[...], wo_ref[...], preferred_element_type=F32)
    _ffn_body(x2, hist_ref, g_ref, wu_ref, cw_ref, cb_ref, wd_ref, fg_ref, y_ref, s_ref, None,
              tm=tm, Lb=Lb, prompt=False, blocks_per_seq=blocks_per_seq)


def _ffn_body(x, hist_ref, g_ref, wu_ref, cw_ref, cb_ref, wd_ref, fg_ref, y_ref, s_ref, carry_sc,
              *, tm, Lb, prompt, blocks_per_seq):
    if prompt:
        @pl.when(pl.program_id(0) % blocks_per_seq == 0)
        def _():
            carry_sc[...] = jnp.zeros_like(carry_sc)

    hn = _rms(x, g_ref[...]).astype(BF16)
    nsub = D_FF // FF_SUB
    r8 = _iota2((8, 1), 0)
    tpos = _iota2((tm, 1), 0) & (Lb - 1)
    if not prompt:
        trow = _iota2((tm, tm // Lb), 0)
        seq0 = _iota2((tm, tm // Lb), 1) * Lb
        at_row0 = (trow == seq0).astype(BF16)
        at_row1 = (trow == seq0 + 1).astype(BF16)

    def conv(up, cols):
        cw = cw_ref[:, cols]
        cb = cb_ref[:, cols]
        if prompt:
            c8 = carry_sc[:, cols]
            carry_sc[:, cols] = up[tm - 8:tm]
            s_ref[0, :, cols] = up[tm - 2:tm]
            top = up[0:8]
            p1 = jnp.where(r8 >= 1, pltpu.roll(top, 1, 0), c8[7:8])
            p2 = jnp.where(r8 >= 2, pltpu.roll(top, 2, 0), jnp.where(r8 == 0, c8[6:7], c8[7:8]))
            head = (cw[0:1] * p2 + cw[1:2] * p1 + cw[2:3] * top) + cb
            body = (cw[0:1] * pltpu.roll(up, 2, 0) + cw[1:2] * pltpu.roll(up, 1, 0) + cw[2:3] * up) + cb
            return jnp.concatenate([head, body[8:]], axis=0)
        for b in range(tm // Lb):
            s_ref[b, :, cols] = up[b * Lb + Lb - 2:b * Lb + Lb]
        st0, st1 = hist_ref[0, :, cols], hist_ref[1, :, cols]
        h1 = _maskdot(at_row0, st1)
        h2 = _maskdot(at_row0, st0) + _maskdot(at_row1, st1)
        return (cw[0:1] * _shifted(up, 2, tpos, h2) + cw[1:2] * _shifted(up, 1, tpos, h1) + cw[2:3] * up) + cb

    def up_proj(k):
        cg = slice(k * FF_SUB, (k + 1) * FF_SUB)
        cv = slice(D_FF + k * FF_SUB, D_FF + (k + 1) * FF_SUB)
        return (jnp.dot(hn, wu_ref[:, cg], preferred_element_type=F32),
                jnp.dot(hn, wu_ref[:, cv], preferred_element_type=F32))

    def gate_mul(k, ug, uv):
        cg = slice(k * FF_SUB, (k + 1) * FF_SUB)
        cv = slice(D_FF + k * FF_SUB, D_FF + (k + 1) * FF_SUB)
        gate = conv(ug, cg)
        return (gate * _sigmoid(gate) * conv(uv, cv)).astype(BF16)

    ups = {}
    acts = []
    for k in range(nsub + FF_SKEW):
        if k < nsub:
            ups[k] = up_proj(k)
        if k >= FF_SKEW:
            acts.append(gate_mul(k - FF_SKEW, *ups.pop(k - FF_SKEW)))
    acc = x + jnp.dot(jnp.concatenate(acts, axis=1), wd_ref[...], preferred_element_type=F32)
    y_ref[...] = _rms(acc, fg_ref[...])


def _resident_spec(a):
    nd = a.ndim
    return pl.BlockSpec(a.shape, lambda *_: (0,) * nd, pipeline_mode=pl.Buffered(1))


def _ffn_short(x1, o, w_o, hist, p, final_g, tm, Lb):
    n, d = x1.shape
    w_up = p["ffn_w_up"].astype(BF16)
    w_down = p["ffn_w_down"].astype(BF16)
    cw, cb = p["ffn_conv_w"], p["ffn_conv_b"].reshape(1, 2 * D_FF)
    row_spec = pl.BlockSpec((tm, d), lambda i: (i, 0))
    seq_spec = lambda lead: pl.BlockSpec((lead, tm // Lb, 2 * D_FF), lambda i: (0, i, 0))
    consts = [p["norm_ffn_w"].reshape(1, d), w_up, cw, cb, w_down, final_g.reshape(1, d)]
    return pl.pallas_call(
        functools.partial(_ffn_kernel, tm=tm, Lb=Lb, prompt=False, blocks_per_seq=1),
        grid=(n // tm,),
        in_specs=[row_spec, row_spec, _resident_spec(w_o), seq_spec(2)] + [_resident_spec(a) for a in consts],
        out_specs=[row_spec, pl.BlockSpec((tm // Lb, 2, 2 * D_FF), lambda i: (i, 0, 0))],
        out_shape=[jax.ShapeDtypeStruct((n, d), F32), jax.ShapeDtypeStruct((n // Lb, 2, 2 * D_FF), F32)],
        compiler_params=_cparams("arbitrary"),
        name="convffn",
    )(x1, o, w_o, hist, *consts)


def _post_mix_kernel(x_ref, ys_ref, yr_ref, k_ref, v_ref, wo_ref, gxa_ref, wq_ref, wao_ref,
                     g_ref, wu_ref, cw_ref, cb_ref, wd_ref, fg_ref, y_ref, s_ref, carry_sc,
                     *, tm, seq_len):
    wo = wo_ref[...]
    x1 = x_ref[...] + (jnp.dot(ys_ref[...], wo[0:SSD_WIDTH], preferred_element_type=F32)
                       + jnp.dot(yr_ref[...], wo[SSD_WIDTH:], preferred_element_type=F32))
    q = (_bdot(_rms(x1, gxa_ref[...]), wq_ref[...]) * (XA_HEAD_DIM ** -0.5)).astype(BF16)
    outs = []
    for h in range(XA_HEADS):
        sl = slice(h * XA_HEAD_DIM, (h + 1) * XA_HEAD_DIM)
        s = lax.dot_general(q[:, sl], k_ref[0, :, sl].astype(BF16), _NT, preferred_element_type=F32)
        s = s - jnp.max(s, axis=-1, keepdims=True)
        e = jnp.exp(s)
        pr = e / jnp.sum(e, axis=-1, keepdims=True)
        outs.append(_bdot(pr, v_ref[0, :, sl]))
    x2 = x1 + _bdot(jnp.concatenate(outs, axis=1), wao_ref[...])
    _ffn_body(x2, None, g_ref, wu_ref, cw_ref, cb_ref, wd_ref, fg_ref, y_ref, s_ref, carry_sc,
              tm=tm, Lb=seq_len, prompt=True, blocks_per_seq=seq_len // tm)


def _post_mix(x, y_ssd, y_rw, mk, mv, w_out, g_xa, w_q, w_o, p, final_g, tm, seq_len):
    n, d = x.shape
    bps = seq_len // tm
    row = lambda w: pl.BlockSpec((tm, w), lambda i: (i, 0))
    kv_spec = pl.BlockSpec((1, N_MEM, d), lambda i: (i // bps, 0, 0))
    consts = [w_out, g_xa.reshape(1, d), w_q, w_o, p["norm_ffn_w"].reshape(1, d), p["ffn_w_up"].astype(BF16),
              p["ffn_conv_w"], p["ffn_conv_b"].reshape(1, 2 * D_FF), p["ffn_w_down"].astype(BF16),
              final_g.reshape(1, d)]
    y, st = pl.pallas_call(
        functools.partial(_post_mix_kernel, tm=tm, seq_len=seq_len),
        grid=(n // tm,),
        in_specs=[row(d), row(SSD_WIDTH), row(RWKV_WIDTH), kv_spec, kv_spec] + [_resident_spec(a) for a in consts],
        out_specs=[row(d), pl.BlockSpec((1, 2, 2 * D_FF), lambda i: (i, 0, 0))],
        out_shape=[jax.ShapeDtypeStruct((n, d), F32), jax.ShapeDtypeStruct((n // tm, 2, 2 * D_FF), F32)],
        scratch_shapes=[pltpu.VMEM((8, 2 * D_FF), F32)],
        compiler_params=_cparams("arbitrary"),
        name="post_mix",
    )(x, y_ssd, y_rw, mk, mv, *consts)
    return y, st[bps - 1::bps]


def _pair_blocks(sbd):
    b = sbd.shape[0]
    s0 = sbd[:, :, 0:64, 0:64]
    s1 = sbd[:, :, 64:128, 64:128]
    return jnp.stack([s0, s1], axis=2).reshape(b, 8, 64, 64)


def kernel(x_prompt, x_sample, mem_prompt, state_ssm_conv, state_ssm, state_shift, state_wkv, state_ffn_conv, cache_mem_k, cache_mem_v, norm_mix_w, w_in, ssd_conv_w, ssd_conv_b, ssd_dt_bias, ssd_a_log, ssd_d, ssd_norm_w, rwkv_mu, rwkv_w0, rwkv_w2, rwkv_a0, rwkv_a2, rwkv_g2, rwkv_k_k, rwkv_k_a, rwkv_r_k, rwkv_ln_w, rwkv_ln_b, w_out, norm_xa_w, mem_norm_w, xa_w_q, xa_w_k, xa_w_v, xa_w_o, norm_ffn_w, ffn_w_up, ffn_conv_w, ffn_conv_b, ffn_w_down, final_norm_w):
    depth = w_in.shape[0]
    assert depth == 1, "final rmsnorm is fused into the (single) layer's ConvFFN kernel"
    bp, lp, d = x_prompt.shape
    bs, ls, _ = x_sample.shape
    i = 0
    p = dict(ssd_conv_w=ssd_conv_w[i], ssd_conv_b=ssd_conv_b[i], ssd_dt_bias=ssd_dt_bias[i],
             ssd_a_log=ssd_a_log[i], ssd_d=ssd_d[i], ssd_norm_w=ssd_norm_w[i], rwkv_mu=rwkv_mu[i],
             rwkv_w0=rwkv_w0[i], rwkv_w2=rwkv_w2[i], rwkv_a0=rwkv_a0[i], rwkv_a2=rwkv_a2[i],
             rwkv_g2=rwkv_g2[i], rwkv_k_k=rwkv_k_k[i], rwkv_k_a=rwkv_k_a[i],
             rwkv_r_k=rwkv_r_k[i].reshape(-1), rwkv_ln_w=rwkv_ln_w[i], rwkv_ln_b=rwkv_ln_b[i],
             norm_ffn_w=norm_ffn_w[i], ffn_w_up=ffn_w_up[i], ffn_conv_w=ffn_conv_w[i],
             ffn_conv_b=ffn_conv_b[i], ffn_w_down=ffn_w_down[i])

    w_in_t = jnp.swapaxes(w_in[i], 0, 1)
    w_in_p = jnp.concatenate([w_in_t[:SSD_PROJ], jnp.zeros((SEG - SSD_PROJ, d), F32),
                              w_in_t[SSD_PROJ:]], axis=0).astype(BF16)
    w_out_b = w_out[i].astype(BF16)
    w_q_b = (xa_w_q[i]).astype(BF16)
    w_o_b = xa_w_o[i].astype(BF16)
    w_kv_b = jnp.concatenate([xa_w_k[i], xa_w_v[i]], axis=1).astype(BF16)

    xp = x_prompt.reshape(bp * lp, d)
    xs = x_sample.reshape(bs * ls, d)

    mk, mv, mk4, mv4 = _mem_kv(mem_prompt.reshape(bp * N_MEM, d), mem_norm_w[i], w_kv_b, 512)
    mk = mk.reshape(bp, N_MEM, d)
    mv = mv.reshape(bp, N_MEM, d)
    tm_p = min(1024, lp)
    u_p, dt_p, tail_p = _norm_proj(xp, norm_mix_w[i], w_in_p, tm_p, BF16)
    y_ssd_p, ssm_p = _ssd_prompt(u_p, dt_p, p, bp, lp, G=math.gcd(4, bp))
    y_rw_p, wkv_bd_p = _wkv_prompt(u_p, p, bp, lp, G=math.gcd(8, bp))
    y_p, ffn_conv_p = _post_mix(xp, y_ssd_p, y_rw_p, mk, mv, w_out_b, norm_xa_w[i], w_q_b, w_o_b, p,
                                final_norm_w, 512, lp)
    seq_tail = tail_p[lp // tm_p - 1::lp // tm_p]
    ssm_conv_p = seq_tail[:, 5:, SSD_WIDTH:SSD_WIDTH + SSD_XBC]
    shift_p = seq_tail[:, 7, SEG:]

    u_s, dt_s, _ = _norm_proj(xs, norm_mix_w[i], w_in_p, 512, F32)
    assert ls >= 3, "the new conv state is read from the last three rows of each sample sequence"
    y_ssd_s, ssm_s, conv_rows_s = _ssd_sample(u_s, dt_s, jnp.swapaxes(state_ssm_conv[i], 0, 1),
                                              state_ssm[i].reshape(bs, 4, LANES, LANES), p, ls)
    ssm_conv_s = conv_rows_s.reshape(bs, 3, SSD_XBC)
    y_rw_s, wkv_s, shift_s = _wkv_sample(u_s, state_shift[i], state_wkv[i], p, ls)
    x1_s, q_s = _outproj(xs, y_ssd_s, y_rw_s, w_out_b, norm_xa_w[i], w_q_b, 512)
    o_s = _xattn_cache(q_s, cache_mem_k[i], cache_mem_v[i], 32, ls)
    y_s, ffn_conv_s = _ffn_short(x1_s, o_s, w_o_b, jnp.swapaxes(state_ffn_conv[i], 0, 1), p, final_norm_w, 256, ls)

    e = lambda a: a[None]
    return (y_p.reshape(bp, lp, d), y_s.reshape(bs, ls, d),
            e(ssm_conv_p), e(ssm_conv_s),
            e(ssm_p.reshape(bp, SSD_HEADS, 64, SSD_STATE)), e(ssm_s.reshape(bs, SSD_HEADS, 64, SSD_STATE)),
            e(shift_p), e(shift_s),
            e(_pair_blocks(wkv_bd_p)), e(wkv_s),
            e(ffn_conv_p), e(ffn_conv_s),
            e(mk4.reshape(bp, N_MEM, XA_HEADS, XA_HEAD_DIM)), e(mv4.reshape(bp, N_MEM, XA_HEADS, XA_HEAD_DIM)))
```
